```python
import jax
import jax.numpy as jnp
from jax import lax
import numpy as np

D_MODEL = 1024
BATCH = 8
SEQ = 2048
DEPTH = 2

GRID_W = 64
CTX_LEN = 256

CHUNK = 128
CM_GROUPS = 4
CM_WIDTH = 512
CM_GROUP_W = CM_WIDTH // CM_GROUPS

NA_HEADS = 8
NA_HEAD_DIM = 64
NA_WIDTH = NA_HEADS * NA_HEAD_DIM
WIN_R = 8
WIN_C = 16
Q_COLS = 16
KB_COLS = Q_COLS + WIN_C

RET_HEADS = 4
RET_DK = 64
RET_DV = 128
RET_QK_W = RET_HEADS * RET_DK
RET_V_W = RET_HEADS * RET_DV
ROPE_BASE = 10000.0

N_BRANCH = 3
BR_W = CM_WIDTH
IN_W = 2 * CM_WIDTH + 3 * NA_WIDTH + 2 * RET_QK_W + 2 * RET_V_W + N_BRANCH * D_MODEL

N_EXPERTS = 64
TOP_K = 8
N_GROUPS = 8
TOPK_GROUPS = 4
EXPERT_W = 256
SHARED_W = 256
ROUTED_SCALE = 2.5
MOE_BLOCK = 128

EPS = 1e-6
NEG = -1e30

kernel_name = 'hybrid_dit_gmlp_natten_retention_moe'


def rms_norm(x, g):
    x32 = x.astype(jnp.float32)
    y = x32 * lax.rsqrt(jnp.mean(x32 * x32, axis=-1, keepdims=True) + EPS)
    return (y * g.astype(jnp.float32)).astype(x.dtype)


def layer_norm(x, g, b):
    x32 = x.astype(jnp.float32)
    xc = x32 - jnp.mean(x32, axis=-1, keepdims=True)
    y = xc * lax.rsqrt(jnp.mean(xc * xc, axis=-1, keepdims=True) + EPS)
    return (y * g.astype(jnp.float32) + b.astype(jnp.float32)).astype(x.dtype)


def split_heads(t, n_heads):
    return t.reshape(t.shape[:-1] + (n_heads, t.shape[-1] // n_heads))


def split_in_proj(z):
    sizes = [CM_WIDTH, CM_WIDTH, NA_WIDTH, NA_WIDTH, NA_WIDTH, RET_QK_W, RET_QK_W, RET_V_W, RET_V_W,
             D_MODEL, D_MODEL, D_MODEL]
    return jnp.split(z, np.cumsum(sizes)[:-1].tolist(), axis=-1)


def chunk_spatial_gating(u, v, ln_g, ln_b, w_s, b_s):
    bsz, n, _ = u.shape
    u = jax.nn.gelu(u)
    v = layer_norm(jax.nn.gelu(v), ln_g, ln_b)
    v = v.reshape(bsz, n // CHUNK, CHUNK, CM_GROUPS, CM_GROUP_W)
    s = jnp.einsum('gpq,bnqgc->bnpgc', w_s, v) + jnp.transpose(b_s)[None, None, :, :, None]
    return u * s.reshape(bsz, n, CM_WIDTH)


def na_geometry(rows):
    wr = min(WIN_R, rows)
    r = np.arange(rows)
    row_start = np.clip(r - WIN_R // 2, 0, rows - wr)
    row_off = row_start[:, None] + np.arange(wr)[None, :] - r[:, None]
    nqb = GRID_W // Q_COLS
    qcol = np.arange(nqb)[:, None] * Q_COLS + np.arange(Q_COLS)[None, :]
    kb_start = np.clip(np.arange(nqb) * Q_COLS - WIN_C // 2, 0, GRID_W - KB_COLS)
    kcol = kb_start[:, None] + np.arange(KB_COLS)[None, :]
    col_start = np.clip(qcol - WIN_C // 2, 0, GRID_W - WIN_C)
    col_mask = ((kcol[:, None, :] >= col_start[:, :, None]) &
                (kcol[:, None, :] < col_start[:, :, None] + WIN_C))
    col_off = np.clip(kcol[:, None, :] - qcol[:, :, None], -(WIN_C - 1), WIN_C - 1)
    return wr, row_start, row_off, kcol, col_off, col_mask


def neighbourhood_attention(q, k, v, k_ctx, v_ctx, rpb):
    bsz, n, h, dh = q.shape
    rows = n // GRID_W
    wr, row_start, row_off, kcol, col_off, col_mask = na_geometry(rows)
    nqb = GRID_W // Q_COLS
    scale = dh ** -0.5
    kg = k.reshape(bsz, rows, GRID_W, h, dh)
    vg = v.reshape(bsz, rows, GRID_W, h, dh)
    bias = rpb[:, (row_off + WIN_R - 1)[:, None, None, :, None], (col_off + WIN_C - 1)[None, :, :, None, :]]
    bias = jnp.where(col_mask[None, None, :, :, None, :], bias.astype(jnp.float32), NEG)
    bias = jnp.moveaxis(bias, 0, 1)
    q_rows = jnp.moveaxis(q.reshape(bsz, rows, nqb, Q_COLS, h, dh), 1, 0)
    n_loc = wr * KB_COLS

    def one_row(args):
        r0, q_row, b_row = args
        k_win = lax.dynamic_slice_in_dim(kg, r0, wr, axis=1)[:, :, kcol]
        v_win = lax.dynamic_slice_in_dim(vg, r0, wr, axis=1)[:, :, kcol]
        s_loc = jnp.einsum('bjqhd,bijkhd->bhjqik', q_row, k_win,
                           preferred_element_type=jnp.float32) * scale + b_row[None]
        s_ctx = jnp.einsum('bjqhd,bmhd->bhjqm', q_row, k_ctx,
                           preferred_element_type=jnp.float32) * scale
        s = jnp.concatenate([s_loc.reshape(s_loc.shape[:4] + (n_loc,)), s_ctx], axis=-1)
        p = jax.nn.softmax(s, axis=-1).astype(v.dtype)
        p_loc = p[..., :n_loc].reshape(s_loc.shape)
        return (jnp.einsum('bhjqik,bijkhd->bjqhd', p_loc, v_win) +
                jnp.einsum('bhjqm,bmhd->bjqhd', p[..., n_loc:], v_ctx))

    out = lax.map(one_row, (jnp.asarray(row_start, dtype=jnp.int32), q_rows, bias))
    return jnp.moveaxis(out, 0, 1).reshape(bsz, n, h * dh)


def context_attention(q, k, v):
    scale = q.shape[-1] ** -0.5
    s = jnp.einsum('bqhd,bkhd->bhqk', q, k, preferred_element_type=jnp.float32) * scale
    p = jax.nn.softmax(s, axis=-1).astype(v.dtype)
    o = jnp.einsum('bhqk,bkhd->bqhd', p, v)
    return o.reshape(o.shape[:2] + (o.shape[2] * o.shape[3],))


def axial_rope(x):
    n, d = x.shape[1], x.shape[-1]
    half = d // 2
    nf = half // 2
    t = jnp.arange(n)
    inv = ROPE_BASE ** (-jnp.arange(nf, dtype=jnp.float32) / nf)

    def rotate(xp, pos):
        ang = pos.astype(jnp.float32)[:, None] * inv[None, :]
        cos = jnp.cos(ang)[None, :, None, :].astype(x.dtype)
        sin = jnp.sin(ang)[None, :, None, :].astype(x.dtype)
        x1, x2 = xp[..., :nf], xp[..., nf:]
        return jnp.concatenate([x1 * cos - x2 * sin, x1 * sin + x2 * cos], axis=-1)

    return jnp.concatenate([rotate(x[..., :half], t // GRID_W), rotate(x[..., half:], t % GRID_W)], axis=-1)


def retention_scan(q, k, v, log_gamma, state0, include_diag):
    bsz, n, h, _ = q.shape
    dv = v.shape[-1]
    pos = jnp.arange(CHUNK, dtype=jnp.float32)
    dist = pos[:, None] - pos[None, :]
    tri = np.tril(np.ones((CHUNK, CHUNK), dtype=bool), 0 if include_diag else -1)
    intra = jnp.where(tri[None], jnp.exp(jnp.maximum(dist, 0.0)[None] * log_gamma[:, None, None]), 0.0)
    q_decay = jnp.exp((pos + 1.0)[None, :] * log_gamma[:, None])
    k_decay = jnp.exp((CHUNK - 1.0 - pos)[None, :] * log_gamma[:, None])
    chunk_decay = jnp.exp(CHUNK * log_gamma)

    def to_chunks(t):
        return jnp.moveaxis(t.astype(jnp.float32).reshape(bsz, n // CHUNK, CHUNK, h, t.shape[-1]), 1, 0)

    def step(state, xs):
        qc, kc, vc = xs
        inner = jnp.einsum('bhij,bjhe->bihe', jnp.einsum('bihd,bjhd->bhij', qc, kc) * intra, vc)
        cross = jnp.einsum('bihd,bhde->bihe', qc, state) * q_decay.T[None, :, :, None]
        state = (state * chunk_decay[None, :, None, None] +
                 jnp.einsum('bjhd,bjhe->bhde', kc * k_decay.T[None, :, :, None], vc))
        return state, inner + cross

    state, out = lax.scan(step, state0, (to_chunks(q), to_chunks(k), to_chunks(v)))
    return jnp.moveaxis(out, 0, 1).reshape(bsz, n, h, dv), state


def flip_seq(t):
    return jnp.flip(t, axis=1)


def bidirectional_retention(q, k, v, q_c, k_c, v_c, decay_logit):
    log_g = jax.nn.log_sigmoid(decay_logit.astype(jnp.float32))
    k = k * RET_DK ** -0.5
    k_c = k_c * RET_DK ** -0.5
    state0 = jnp.zeros((q.shape[0], RET_HEADS, RET_DK, RET_DV), jnp.float32)
    out_cf, st_f = retention_scan(q_c, k_c, v_c, log_g[0], state0, True)
    out_cb, st_b = retention_scan(flip_seq(q_c), flip_seq(k_c), flip_seq(v_c), log_g[1], state0, False)
    out_f, _ = retention_scan(q, k, v, log_g[0], st_f, True)
    out_b, _ = retention_scan(flip_seq(q), flip_seq(k), flip_seq(v), log_g[1], st_b, False)
    return out_f + flip_seq(out_b), out_cf + flip_seq(out_cb)


def retention_readout(o, gate):
    o = o * lax.rsqrt(jnp.mean(o * o, axis=-1, keepdims=True) + EPS)
    return o.reshape(o.shape[:2] + (RET_V_W,)).astype(gate.dtype) * jax.nn.silu(gate)


def merge_branches(br_a, br_b, br_c, g_a, g_b, g_c, w_branch, w_out):
    m = (jax.nn.sigmoid(g_a) * (br_a @ w_branch[0]) +
         jax.nn.sigmoid(g_b) * (br_b @ w_branch[1]) +
         jax.nn.sigmoid(g_c) * (br_c @ w_branch[2]))
    return m @ w_out


def hybrid_mixer(z, zc, cm_ln_g, cm_ln_b, cm_w_s, cm_b_s, na_rpb, ret_decay_logit, w_branch, w_out, need_ctx):
    (a_u, a_v, n_q, n_k, n_v, r_q, r_k, r_v, r_g, g_a, g_b, g_c) = split_in_proj(z)
    (ca_u, ca_v, cn_q, cn_k, cn_v, cr_q, cr_k, cr_v, cr_g, cg_a, cg_b, cg_c) = split_in_proj(zc)
    k_ctx = split_heads(cn_k, NA_HEADS)
    v_ctx = split_heads(cn_v, NA_HEADS)
    a_lat = chunk_spatial_gating(a_u, a_v, cm_ln_g, cm_ln_b, cm_w_s, cm_b_s)
    b_lat = neighbourhood_attention(split_heads(n_q, NA_HEADS), split_heads(n_k, NA_HEADS),
                                    split_heads(n_v, NA_HEADS), k_ctx, v_ctx, na_rpb)
    ret_lat, ret_ctx = bidirectional_retention(
        axial_rope(split_heads(r_q, RET_HEADS)), axial_rope(split_heads(r_k, RET_HEADS)),
        split_heads(r_v, RET_HEADS), split_heads(cr_q, RET_HEADS), split_heads(cr_k, RET_HEADS),
        split_heads(cr_v, RET_HEADS), ret_decay_logit)
    c_lat = retention_readout(ret_lat, r_g)
    out = merge_branches(a_lat, b_lat, c_lat, g_a, g_b, g_c, w_branch, w_out)
    if not need_ctx:
        return out, None
    a_c = chunk_spatial_gating(ca_u, ca_v, cm_ln_g, cm_ln_b, cm_w_s, cm_b_s)
    b_c = context_attention(split_heads(cn_q, NA_HEADS), k_ctx, v_ctx)
    c_c = retention_readout(ret_ctx, cr_g)
    out_c = merge_branches(a_c, b_c, c_c, cg_a, cg_b, cg_c, w_branch, w_out)
    return out, out_c


def moe(h, router_w, router_bias, w_gate, w_up, w_down, s_gate, s_up, s_down):
    n, d = h.shape
    scores = jax.nn.sigmoid((h @ router_w).astype(jnp.float32))
    biased = scores + router_bias.astype(jnp.float32)
    grp = biased.reshape(n, N_GROUPS, N_EXPERTS // N_GROUPS)
    grp_score = jnp.sum(lax.top_k(grp, 2)[0], axis=-1)
    top_groups = lax.top_k(grp_score, TOPK_GROUPS)[1]
    grp_sel = jnp.any(top_groups[:, :, None] == jnp.arange(N_GROUPS)[None, None, :], axis=1)
    cand = jnp.where(grp_sel[:, :, None], grp, -jnp.inf).reshape(n, N_EXPERTS)
    idx = lax.top_k(cand, TOP_K)[1]
    w = jnp.take_along_axis(scores, idx, axis=-1)
    w = w / jnp.sum(w, axis=-1, keepdims=True) * ROUTED_SCALE
    combine = jnp.einsum('nk,nke->ne', w, jax.nn.one_hot(idx, N_EXPERTS, dtype=jnp.float32)).astype(h.dtype)
    hb = h.reshape(n // MOE_BLOCK, MOE_BLOCK, d)
    cb = combine.reshape(n // MOE_BLOCK, MOE_BLOCK, N_EXPERTS)

    def block(args):
        xb, gb = args
        a = jnp.einsum('td,edf->tef', xb, w_gate)
        u = jnp.einsum('td,edf->tef', xb, w_up)
        return jnp.einsum('tef,efd->td', jax.nn.silu(a) * u * gb[..., None], w_down)

    routed = lax.map(block, (hb, cb)).reshape(n, d)
    shared = (jax.nn.silu(h @ s_gate) * (h @ s_up)) @ s_down
    return routed + shared


def setup_inputs(seed: int = 0) -> dict:
    key = jax.random.key(seed)
    ks = jax.random.split(key, 32)
    f32 = jnp.float32
    D = D_MODEL

    def nrm(k, shape, scale):
        return jax.random.normal(k, shape, f32) * scale

    a = 5.0 + np.arange(RET_HEADS, dtype=np.float32)
    decay_base = jnp.asarray(np.log(2.0 ** a - 1.0), dtype=f32)
    return {
        'x': nrm(ks[0], (BATCH, SEQ, D), 1.0),
        'c': nrm(ks[1], (BATCH, D), 1.0),
        'ctx': nrm(ks[2], (BATCH, CTX_LEN, D), 1.0),
        'c_ctx': nrm(ks[3], (D,), 1.0),
        'w_ada': nrm(ks[4], (DEPTH, D, 6 * D), 0.5 * D ** -0.5),
        'b_ada': nrm(ks[5], (DEPTH, 6 * D), 0.02),
        'norm_mix_g': 1.0 + nrm(ks[6], (DEPTH, D), 0.02),
        'norm_ffn_g': 1.0 + nrm(ks[7], (DEPTH, D), 0.02),
        'w_in': nrm(ks[8], (DEPTH, D, IN_W), D ** -0.5),
        'cm_ln_g': 1.0 + nrm(ks[9], (DEPTH, CM_WIDTH), 0.02),
        'cm_ln_b': nrm(ks[10], (DEPTH, CM_WIDTH), 0.02),
        'cm_w_s': nrm(ks[11], (DEPTH, CM_GROUPS, CHUNK, CHUNK), CHUNK ** -0.5),
        'cm_b_s': 1.0 + nrm(ks[12], (DEPTH, CM_GROUPS, CHUNK), 0.02),
        'na_rpb': nrm(ks[13], (DEPTH, NA_HEADS, 2 * WIN_R - 1, 2 * WIN_C - 1), 0.02),
        'ret_decay_logit': decay_base + nrm(ks[14], (DEPTH, 2, RET_HEADS), 0.05),
        'w_branch': nrm(ks[15], (DEPTH, N_BRANCH, BR_W, D), BR_W ** -0.5),
        'w_out': nrm(ks[16], (DEPTH, D, D), D ** -0.5),
        'router_w': nrm(ks[17], (DEPTH, D, N_EXPERTS), D ** -0.5),
        'router_bias': nrm(ks[18], (DEPTH, N_EXPERTS), 0.01),
        'exp_w_gate': nrm(ks[19], (DEPTH, N_EXPERTS, D, EXPERT_W), D ** -0.5),
        'exp_w_up': nrm(ks[20], (DEPTH, N_EXPERTS, D, EXPERT_W), D ** -0.5),
        'exp_w_down': nrm(ks[21], (DEPTH, N_EXPERTS, EXPERT_W, D), EXPERT_W ** -0.5),
        'sh_w_gate': nrm(ks[22], (DEPTH, D, SHARED_W), D ** -0.5),
        'sh_w_up': nrm(ks[23], (DEPTH, D, SHARED_W), D ** -0.5),
        'sh_w_down': nrm(ks[24], (DEPTH, SHARED_W, D), SHARED_W ** -0.5),
        'final_g': 1.0 + nrm(ks[25], (D,), 0.02),
    }


def reference(x, c, ctx, c_ctx, w_ada, b_ada, norm_mix_g, norm_ffn_g, w_in, cm_ln_g, cm_ln_b, cm_w_s, cm_b_s,
              na_rpb, ret_decay_logit, w_branch, w_out, router_w, router_bias, exp_w_gate, exp_w_up, exp_w_down,
              sh_w_gate, sh_w_up, sh_w_down, final_g):
    bsz, n, d = x.shape
    xc = ctx
    for l in range(DEPTH):
        need_ctx = l < DEPTH - 1
        mod = jax.nn.silu(c) @ w_ada[l] + b_ada[l]
        mod_c = jax.nn.silu(c_ctx) @ w_ada[l] + b_ada[l]
        sh1, sc1, ga1, sh2, sc2, ga2 = jnp.split(mod[:, None, :], 6, axis=-1)
        sh1c, sc1c, ga1c, sh2c, sc2c, ga2c = jnp.split(mod_c, 6)
        h = rms_norm(x, norm_mix_g[l]) * (1.0 + sc1) + sh1
        hc = rms_norm(xc, norm_mix_g[l]) * (1.0 + sc1c) + sh1c
        mix, mix_c = hybrid_mixer(h @ w_in[l], hc @ w_in[l], cm_ln_g[l], cm_ln_b[l], cm_w_s[l], cm_b_s[l],
                                  na_rpb[l], ret_decay_logit[l], w_branch[l], w_out[l], need_ctx)
        x = x + ga1 * mix
        h = rms_norm(x, norm_ffn_g[l]) * (1.0 + sc2) + sh2
        y = moe(h.reshape(bsz * n, d), router_w[l], router_bias[l], exp_w_gate[l], exp_w_up[l], exp_w_down[l],
                sh_w_gate[l], sh_w_up[l], sh_w_down[l])
        x = x + ga2 * y.reshape(bsz, n, d)
        if need_ctx:
            xc = xc + ga1c * mix_c
            hc = rms_norm(xc, norm_ffn_g[l]) * (1.0 + sc2c) + sh2c
            yc = moe(hc.reshape(-1, d), router_w[l], router_bias[l], exp_w_gate[l], exp_w_up[l], exp_w_down[l],
                     sh_w_gate[l], sh_w_up[l], sh_w_down[l])
            xc = xc + ga2c * yc.reshape(xc.shape)
    return rms_norm(x, final_g)
```

```python
import functools

import jax
import jax.numpy as jnp
import numpy as np
from jax import lax
from jax.experimental import pallas as pl
from jax.experimental.pallas import tpu as pltpu

F32 = jnp.float32
BF16 = jnp.bfloat16

D_MODEL = 1024
GRID_W = 64
CHUNK = 128
CM_GROUPS = 4
CM_WIDTH = 512
NA_HEADS = 8
NA_HEAD_DIM = 64
NA_WIDTH = NA_HEADS * NA_HEAD_DIM
WIN_R = 8
WIN_C = 16
RET_HEADS = 4
RET_DK = 64
RET_DV = 128
RET_QK_W = RET_HEADS * RET_DK
RET_V_W = RET_HEADS * RET_DV
ROPE_BASE = 10000.0
IN_W = 2 * CM_WIDTH + 3 * NA_WIDTH + 2 * RET_QK_W + 2 * RET_V_W + 3 * D_MODEL
N_EXPERTS = 64
TOP_K = 8
N_GROUPS = 8
TOPK_GROUPS = 4
EXPERT_W = 256
ROUTED_SCALE = 2.5
EPS = 1e-6
NEG = -1e30

OFF_AU, OFF_AV = 0, 512
OFF_NQ, OFF_NK, OFF_NV = 1024, 1536, 2048
OFF_RQ, OFF_RK, OFF_RV, OFF_RG = 2560, 2816, 3072, 3584
OFF_GA, OFF_GB, OFF_GC = 4096, 5120, 6144

MOD_ROWS = 16
VMEM_LIMIT = 56 * 1024 * 1024


def _params(sem, vmem=None):
    return pltpu.CompilerParams(dimension_semantics=sem, vmem_limit_bytes=vmem)


def _silu(v):
    return v * jax.nn.sigmoid(v)


def _mm(a, b):
    return jnp.dot(a, b, preferred_element_type=F32)


def _mm_nt(a, b):
    return lax.dot_general(a, b, (((1,), (1,)), ((), ())), preferred_element_type=F32)


def _mod_row(nb):
    return lambda b, j, *_: (jnp.where(j == 0, nb, b), 0, 0)


def _adaln_kernel(c_ref, w_ref, b_ref, o_ref):
    s = _silu(c_ref[...])
    o_ref[0] = _mm(s, w_ref[0]) + b_ref[0]


def _adaln(cc, w_ada, b_ada):
    depth, d, n6 = w_ada.shape
    tn = 1536
    return pl.pallas_call(
        _adaln_kernel,
        grid=(depth, n6 // tn),
        in_specs=[
            pl.BlockSpec((MOD_ROWS, d), lambda l, n: (0, 0)),
            pl.BlockSpec((1, d, tn), lambda l, n: (l, 0, n)),
            pl.BlockSpec((1, 1, tn), lambda l, n: (l, 0, n)),
        ],
        out_specs=pl.BlockSpec((1, MOD_ROWS, tn), lambda l, n: (l, 0, n)),
        out_shape=jax.ShapeDtypeStruct((depth, MOD_ROWS, n6), F32),
        compiler_params=_params(("parallel", "parallel"), VMEM_LIMIT),
        name="adaln",
    )(cc, w_ada, b_ada.reshape(depth, 1, n6))


def _modulated_norm(x, g, shift, scale):
    y = x * lax.rsqrt(jnp.mean(x * x, axis=-1, keepdims=True) + EPS) * g
    return y * (1.0 + scale) + shift


def _inproj_kernel(x_ref, mod_ref, g_ref, w_ref, z_ref):
    h = _modulated_norm(x_ref[0], g_ref[...], mod_ref[0, 0:1, :], mod_ref[0, 1:2, :])
    hb = h.astype(BF16)
    nw = 512
    for n in range(IN_W // nw):
        z_ref[0, :, n * nw:(n + 1) * nw] = _mm(hb, w_ref[:, n * nw:(n + 1) * nw]).astype(BF16)


def _inproj(xs, mod, g, w_in_b, tm):
    nb, t, d = xs.shape
    return pl.pallas_call(
        _inproj_kernel,
        grid=(nb, t // tm),
        in_specs=[
            pl.BlockSpec((1, tm, d), lambda b, j: (b, j, 0)),
            pl.BlockSpec((1, 6, d), _mod_row(nb)),
            pl.BlockSpec((1, d), lambda b, j: (0, 0)),
            pl.BlockSpec((d, IN_W), lambda b, j: (0, 0)),
        ],
        out_specs=pl.BlockSpec((1, tm, IN_W), lambda b, j: (b, j, 0)),
        out_shape=jax.ShapeDtypeStruct((nb, t, IN_W), BF16),
        compiler_params=_params(("parallel", "parallel"), VMEM_LIMIT),
        name="inproj",
    )(xs, mod, g.reshape(1, d), w_in_b)


def _gmlp_kernel(u_ref, v_ref, lng_ref, lnb_ref, ws_ref, bs_ref, o_ref):
    u = jax.nn.gelu(u_ref[0].astype(F32))
    v = jax.nn.gelu(v_ref[0].astype(F32))
    vc = v - jnp.mean(v, axis=-1, keepdims=True)
    vn = vc * lax.rsqrt(jnp.mean(vc * vc, axis=-1, keepdims=True) + EPS) * lng_ref[...] + lnb_ref[...]
    vb = vn.astype(BF16)
    gw = CM_WIDTH // CM_GROUPS
    for g in range(CM_GROUPS):
        s = _mm(ws_ref[g], vb[:, g * gw:(g + 1) * gw]) + bs_ref[:, g:g + 1]
        o_ref[0, :, g * gw:(g + 1) * gw] = (u[:, g * gw:(g + 1) * gw] * s).astype(BF16)


def _gmlp(z, ln_g, ln_b, ws_b, bs_t):
    nb, t, _ = z.shape
    return pl.pallas_call(
        _gmlp_kernel,
        grid=(nb, t // CHUNK),
        in_specs=[
            pl.BlockSpec((1, CHUNK, CM_WIDTH), lambda b, j: (b, j, OFF_AU // CM_WIDTH)),
            pl.BlockSpec((1, CHUNK, CM_WIDTH), lambda b, j: (b, j, OFF_AV // CM_WIDTH)),
            pl.BlockSpec((1, CM_WIDTH), lambda b, j: (0, 0)),
            pl.BlockSpec((1, CM_WIDTH), lambda b, j: (0, 0)),
            pl.BlockSpec((CM_GROUPS, CHUNK, CHUNK), lambda b, j: (0, 0, 0)),
            pl.BlockSpec((CHUNK, CM_GROUPS), lambda b, j: (0, 0)),
        ],
        out_specs=pl.BlockSpec((1, CHUNK, CM_WIDTH), lambda b, j: (b, j, 0)),
        out_shape=jax.ShapeDtypeStruct((nb, t, CM_WIDTH), BF16),
        compiler_params=_params(("parallel", "parallel")),
        name="gmlp",
    )(z, z, ln_g.reshape(1, -1), ln_b.reshape(1, -1), ws_b, bs_t)


def _na_row_classes(rows):
    half = WIN_R // 2
    wr = min(WIN_R, rows)
    reps = list(range(half)) + [half] + list(range(rows - (wr - half) + 1, rows))
    cls = np.zeros((rows,), np.int32)
    for r in range(rows):
        if r < half:
            cls[r] = r
        elif r <= rows - (wr - half):
            cls[r] = half
        else:
            cls[r] = half + 1 + (r - (rows - (wr - half) + 1))
    return reps, cls


def _na_bias_table(rpb, rows):
    wr = min(WIN_R, rows)
    reps, _ = _na_row_classes(rows)
    qc = np.arange(GRID_W)[:, None]
    kc = np.arange(GRID_W)[None, :]
    col_start = np.clip(qc - WIN_C // 2, 0, GRID_W - WIN_C)
    valid = (kc >= col_start) & (kc < col_start + WIN_C)
    col_off = np.clip(kc - qc, -(WIN_C - 1), WIN_C - 1) + WIN_C - 1
    tabs = []
    for r in reps:
        r0 = int(np.clip(r - WIN_R // 2, 0, rows - wr))
        row_off = r0 + np.arange(wr) - r + WIN_R - 1
        b = rpb[:, row_off[None, :, None], col_off[:, None, :]]
        b = jnp.where(valid[None, :, None, :], b.astype(F32), NEG)
        tabs.append(b.reshape(rpb.shape[0], GRID_W, wr * GRID_W))
    return jnp.stack(tabs)


def _na_kernel(q_ref, k_ref, v_ref, bias_ref, o_ref, *, lc, rows):
    j = pl.program_id(1)
    n_ctx_blk = lc // GRID_W
    wr = min(WIN_R, rows)
    lane = lax.broadcasted_iota(jnp.int32, (GRID_W, 128), 1)
    scale = NA_HEAD_DIM ** -0.5

    def attend(local):
        q = q_ref[0]
        if local:
            r0 = jnp.clip(j - n_ctx_blk - WIN_R // 2, 0, rows - wr)
            start = pl.multiple_of(lc + r0 * GRID_W, GRID_W)
            k_loc = k_ref[0, pl.ds(start, wr * GRID_W), :]
            v_loc = v_ref[0, pl.ds(start, wr * GRID_W), :]
        k_ctx = k_ref[0, 0:lc, :]
        v_ctx = v_ref[0, 0:lc, :]
        for p in range(NA_HEADS // 2):
            cs = slice(p * 128, (p + 1) * 128)
            q2 = q[:, cs] * scale
            outs = []
            for hh in range(2):
                h = 2 * p + hh
                qm = jnp.where((lane >= 64 * hh) & (lane < 64 * (hh + 1)), q2, 0).astype(BF16)
                s_ctx = _mm_nt(qm, k_ctx[:, cs])
                m = jnp.max(s_ctx, axis=-1, keepdims=True)
                if local:
                    s_loc = _mm_nt(qm, k_loc[:, cs]) + bias_ref[0, h]
                    m = jnp.maximum(m, jnp.max(s_loc, axis=-1, keepdims=True))
                p_ctx = jnp.exp(s_ctx - m)
                den = jnp.sum(p_ctx, axis=-1, keepdims=True)
                o = _mm(p_ctx.astype(BF16), v_ctx[:, cs])
                if local:
                    p_loc = jnp.exp(s_loc - m)
                    den = den + jnp.sum(p_loc, axis=-1, keepdims=True)
                    o = o + _mm(p_loc.astype(BF16), v_loc[:, cs])
                outs.append(o / den)
            o_ref[0, :, cs] = jnp.where(lane < 64, outs[0], outs[1]).astype(BF16)

    @pl.when(j < n_ctx_blk)
    def _():
        attend(False)

    @pl.when(j >= n_ctx_blk)
    def _():
        attend(True)


def _nattn(z, bias_tab, lc):
    nb, t, _ = z.shape
    rows = (t - lc) // GRID_W
    n_ctx_blk = lc // GRID_W
    _, cls = _na_row_classes(rows)
    cls_arr = jnp.asarray(np.concatenate([np.zeros((n_ctx_blk,), np.int32), cls]))
    wr = min(WIN_R, rows)
    grid_spec = pltpu.PrefetchScalarGridSpec(
        num_scalar_prefetch=1,
        grid=(nb, t // GRID_W),
        in_specs=[
            pl.BlockSpec((1, GRID_W, NA_WIDTH), lambda b, j, c: (b, j, OFF_NQ // NA_WIDTH)),
            pl.BlockSpec((1, t, NA_WIDTH), lambda b, j, c: (b, 0, OFF_NK // NA_WIDTH)),
            pl.BlockSpec((1, t, NA_WIDTH), lambda b, j, c: (b, 0, OFF_NV // NA_WIDTH)),
            pl.BlockSpec((1, NA_HEADS, GRID_W, wr * GRID_W), lambda b, j, c: (c[j], 0, 0, 0)),
        ],
        out_specs=pl.BlockSpec((1, GRID_W, NA_WIDTH), lambda b, j, c: (b, j, 0)),
    )
    return pl.pallas_call(
        functools.partial(_na_kernel_sp, lc=lc, rows=rows),
        grid_spec=grid_spec,
        out_shape=jax.ShapeDtypeStruct((nb, t, NA_WIDTH), BF16),
        compiler_params=_params(("parallel", "arbitrary"), VMEM_LIMIT),
        name="nattn",
    )(cls_arr, z, z, z, bias_tab)


def _na_kernel_sp(cls_ref, q_ref, k_ref, v_ref, bias_ref, o_ref, *, lc, rows):
    del cls_ref
    _na_kernel(q_ref, k_ref, v_ref, bias_ref, o_ref, lc=lc, rows=rows)


def _rope_tables(lc, l):
    half = RET_DK // 2
    nf = half // 2
    tpos = np.arange(l)
    inv = ROPE_BASE ** (-jnp.arange(nf, dtype=F32) / nf)

    def part(pos):
        ang = jnp.asarray(pos, F32)[:, None] * inv[None, :]
        c, s = jnp.cos(ang), jnp.sin(ang)
        return jnp.concatenate([c, c], -1), jnp.concatenate([-s, s], -1)

    c_r, s_r = part(tpos // GRID_W)
    c_c, s_c = part(tpos % GRID_W)
    cos = jnp.concatenate([c_r, c_c], -1)
    sin = jnp.concatenate([s_r, s_c], -1)
    cos = jnp.concatenate([jnp.ones((lc, RET_DK), F32), cos], 0)
    sin = jnp.concatenate([jnp.zeros((lc, RET_DK), F32), sin], 0)
    return jnp.tile(cos, (1, RET_HEADS)), jnp.tile(sin, (1, RET_HEADS))


def _ret_kernel(dl_ref, q_ref, k_ref, v_ref, cos_ref, sin_ref, o_ref, st_ref):
    d = pl.program_id(0)
    t = pl.program_id(2)

    @pl.when(t == 0)
    def _():
        st_ref[...] = jnp.zeros_like(st_ref)

    cos = cos_ref[...]
    sin = sin_ref[...]
    lane_qk = lax.broadcasted_iota(jnp.int32, (CHUNK, RET_QK_W), 1)
    first = (lane_qk % (RET_DK // 2)) < (RET_DK // 4)

    def rope(a):
        a = a.astype(F32)
        partner = jnp.where(first, pltpu.roll(a, RET_QK_W - RET_DK // 4, axis=1),
                            pltpu.roll(a, RET_DK // 4, axis=1))
        return a * cos + partner * sin

    q = rope(q_ref[0])
    k = rope(k_ref[0]) * (RET_DK ** -0.5)
    v = v_ref[0]

    ri = lax.broadcasted_iota(jnp.int32, (CHUNK, CHUNK), 0)
    ci = lax.broadcasted_iota(jnp.int32, (CHUNK, CHUNK), 1)
    fwd = d == 0
    dist = jnp.where(fwd, ri - ci, ci - ri)
    keep = dist >= d
    dist_f = jnp.maximum(dist, 0).astype(F32)
    q_exp = jnp.where(fwd, ri + 1, CHUNK - ri).astype(F32)
    k_exp = jnp.where(fwd, CHUNK - 1 - ri, ri).astype(F32)
    lane_half = ci >= RET_DK
    row_half = ri >= RET_DK

    def log_gamma(h):
        xv = jnp.full((CHUNK, CHUNK), dl_ref[d, h], F32)
        return jnp.minimum(xv, 0.0) - jnp.log1p(jnp.exp(-jnp.abs(xv)))

    for p in range(RET_HEADS // 2):
        cs = slice(p * 128, (p + 1) * 128)
        lg = [log_gamma(2 * p), log_gamma(2 * p + 1)]
        q2 = q[:, cs]
        k2 = k[:, cs]
        k2b = k2.astype(BF16)
        kd_t = jnp.transpose(k2 * jnp.exp(k_exp * jnp.where(lane_half, lg[1], lg[0])))
        s2 = st_ref[p]
        s2b = s2.astype(BF16)
        upd = s2 * jnp.exp(float(CHUNK) * jnp.where(row_half, lg[1], lg[0]))
        for hh in range(2):
            h = 2 * p + hh
            vh = v[:, h * RET_DV:(h + 1) * RET_DV]
            sel = lane_half if hh else jnp.logical_not(lane_half)
            qm = jnp.where(sel, q2, 0.0).astype(BF16)
            a = _mm_nt(qm, k2b)
            a = jnp.where(keep, a * jnp.exp(dist_f * lg[hh]), 0.0)
            inner = _mm(a.astype(BF16), vh)
            cross = _mm(qm, s2b) * jnp.exp(q_exp * lg[hh])
            o_ref[0, 0, :, h * RET_DV:(h + 1) * RET_DV] = (inner + cross).astype(o_ref.dtype)
            rsel = row_half if hh else jnp.logical_not(row_half)
            upd = upd + _mm(jnp.where(rsel, kd_t, 0.0).astype(BF16), vh)
        st_ref[p] = upd


def _retention(z, decay_logit, cos_t, sin_t, lc):
    nb, t, _ = z.shape
    nc = t // CHUNK
    ncc = lc // CHUNK

    def chunk(d, s):
        back = jnp.where(s < ncc, ncc - 1 - s, nc - 1 - (s - ncc))
        return jnp.where(d == 0, s, back)

    grid_spec = pltpu.PrefetchScalarGridSpec(
        num_scalar_prefetch=1,
        grid=(2, nb, nc),
        in_specs=[
            pl.BlockSpec((1, CHUNK, RET_QK_W), lambda d, b, s, dl: (b, chunk(d, s), OFF_RQ // RET_QK_W)),
            pl.BlockSpec((1, CHUNK, RET_QK_W), lambda d, b, s, dl: (b, chunk(d, s), OFF_RK // RET_QK_W)),
            pl.BlockSpec((1, CHUNK, RET_V_W), lambda d, b, s, dl: (b, chunk(d, s), OFF_RV // RET_V_W)),
            pl.BlockSpec((CHUNK, RET_QK_W), lambda d, b, s, dl: (chunk(d, s), 0)),
            pl.BlockSpec((CHUNK, RET_QK_W), lambda d, b, s, dl: (chunk(d, s), 0)),
        ],
        out_specs=pl.BlockSpec((1, 1, CHUNK, RET_V_W), lambda d, b, s, dl: (d, b, chunk(d, s), 0)),
        scratch_shapes=[pltpu.VMEM((RET_HEADS // 2, 2 * RET_DK, RET_DV), F32)],
    )
    return pl.pallas_call(
        _ret_kernel,
        grid_spec=grid_spec,
        out_shape=jax.ShapeDtypeStruct((2, nb, t, RET_V_W), BF16),
        compiler_params=_params(("parallel", "parallel", "arbitrary")),
        name="retention",
    )(decay_logit, z, z, z, cos_t, sin_t)


def _merge_kernel(a_ref, b_ref, rf_ref, rb_ref, rg_ref, ga_ref, gb_ref, gc_ref, x_ref, mod_ref,
                  wb_ref, wo_ref, o_ref):
    o = rf_ref[0, 0].astype(F32) + rb_ref[0, 0].astype(F32)
    parts = []
    for h in range(RET_HEADS):
        oh = o[:, h * RET_DV:(h + 1) * RET_DV]
        parts.append(oh * lax.rsqrt(jnp.mean(oh * oh, axis=-1, keepdims=True) + EPS))
    c = jnp.concatenate(parts, axis=-1) * _silu(rg_ref[0].astype(F32))
    m = (jax.nn.sigmoid(ga_ref[0].astype(F32)) * _mm(a_ref[0], wb_ref[0])
         + jax.nn.sigmoid(gb_ref[0].astype(F32)) * _mm(b_ref[0], wb_ref[1])
         + jax.nn.sigmoid(gc_ref[0].astype(F32)) * _mm(c.astype(BF16), wb_ref[2]))
    mix = _mm(m.astype(BF16), wo_ref[...])
    o_ref[0] = x_ref[0] + mod_ref[0, 2:3, :] * mix


def _merge(a_lat, b_lat, ret, z, xs, mod, wb_b, wo_b, tm):
    nb, t, d = xs.shape
    bw = CM_WIDTH
    return pl.pallas_call(
        _merge_kernel,
        grid=(nb, t // tm),
        in_specs=[
            pl.BlockSpec((1, tm, bw), lambda b, j: (b, j, 0)),
            pl.BlockSpec((1, tm, bw), lambda b, j: (b, j, 0)),
            pl.BlockSpec((1, 1, tm, bw), lambda b, j: (0, b, j, 0)),
            pl.BlockSpec((1, 1, tm, bw), lambda b, j: (1, b, j, 0)),
            pl.BlockSpec((1, tm, RET_V_W), lambda b, j: (b, j, OFF_RG // RET_V_W)),
            pl.BlockSpec((1, tm, d), lambda b, j: (b, j, OFF_GA // d)),
            pl.BlockSpec((1, tm, d), lambda b, j: (b, j, OFF_GB // d)),
            pl.BlockSpec((1, tm, d), lambda b, j: (b, j, OFF_GC // d)),
            pl.BlockSpec((1, tm, d), lambda b, j: (b, j, 0)),
            pl.BlockSpec((1, 6, d), _mod_row(nb)),
            pl.BlockSpec((3, bw, d), lambda b, j: (0, 0, 0)),
            pl.BlockSpec((d, d), lambda b, j: (0, 0)),
        ],
        out_specs=pl.BlockSpec((1, tm, d), lambda b, j: (b, j, 0)),
        out_shape=jax.ShapeDtypeStruct((nb, t, d), F32),
        compiler_params=_params(("parallel", "parallel"), VMEM_LIMIT),
        name="merge",
    )(a_lat, b_lat, ret, ret, z, z, z, z, xs, mod, wb_b, wo_b)


def _first_index(mask, idx, big):
    return jnp.min(jnp.where(mask, idx, big), axis=0, keepdims=True)


def _router_kernel(x_ref, mod_ref, g_ref, rwt_ref, rb_ref, h_ref, comb_ref):
    h = _modulated_norm(x_ref[0], g_ref[...], mod_ref[0, 3:4, :], mod_ref[0, 4:5, :])
    h_ref[0] = h.astype(BF16)
    tm = h.shape[0]
    logits = _mm_nt(rwt_ref[...], h)
    scores = jax.nn.sigmoid(logits)
    biased = scores + rb_ref[:, 0:1]
    gsz = N_EXPERTS // N_GROUPS
    sub = lax.broadcasted_iota(jnp.int32, (gsz, tm), 0)
    gscore = []
    for g in range(N_GROUPS):
        blk = biased[g * gsz:(g + 1) * gsz]
        m1 = jnp.max(blk, axis=0, keepdims=True)
        i1 = _first_index(blk == m1, sub, gsz)
        m2 = jnp.max(jnp.where(sub == i1, -jnp.inf, blk), axis=0, keepdims=True)
        gscore.append(m1 + m2)
    gs = jnp.concatenate(gscore, axis=0)
    gidx = lax.broadcasted_iota(jnp.int32, (N_GROUPS, tm), 0)
    gsel = jnp.zeros((N_GROUPS, tm), jnp.int32)
    for _ in range(TOPK_GROUPS):
        m = jnp.max(gs, axis=0, keepdims=True)
        pick = gidx == _first_index(gs == m, gidx, N_GROUPS)
        gsel = jnp.where(pick, 1, gsel)
        gs = jnp.where(pick, -jnp.inf, gs)
    cand = jnp.concatenate(
        [jnp.where(gsel[g:g + 1] > 0, biased[g * gsz:(g + 1) * gsz], -jnp.inf) for g in range(N_GROUPS)],
        axis=0)
    eidx = lax.broadcasted_iota(jnp.int32, (N_EXPERTS, tm), 0)
    w = jnp.zeros((N_EXPERTS, tm), F32)
    for _ in range(TOP_K):
        m = jnp.max(cand, axis=0, keepdims=True)
        pick = eidx == _first_index(cand == m, eidx, N_EXPERTS)
        w = jnp.where(pick, scores, w)
        cand = jnp.where(pick, -jnp.inf, cand)
    w = w / jnp.sum(w, axis=0, keepdims=True) * ROUTED_SCALE
    wp = jnp.concatenate([w, jnp.zeros((128 - N_EXPERTS, tm), F32)], axis=0)
    comb_ref[0] = jnp.transpose(wp)


def _router(xs, mod, g, rw_t, rb, tm):
    nb, t, d = xs.shape
    return pl.pallas_call(
        _router_kernel,
        grid=(nb, t // tm),
        in_specs=[
            pl.BlockSpec((1, tm, d), lambda b, j: (b, j, 0)),
            pl.BlockSpec((1, 6, d), _mod_row(nb)),
            pl.BlockSpec((1, d), lambda b, j: (0, 0)),
            pl.BlockSpec((N_EXPERTS, d), lambda b, j: (0, 0)),
            pl.BlockSpec((N_EXPERTS, 1), lambda b, j: (0, 0)),
        ],
        out_specs=[
            pl.BlockSpec((1, tm, d), lambda b, j: (b, j, 0)),
            pl.BlockSpec((1, tm, 128), lambda b, j: (b, j, 0)),
        ],
        out_shape=[jax.ShapeDtypeStruct((nb, t, d), BF16), jax.ShapeDtypeStruct((nb, t, 128), F32)],
        compiler_params=_params(("parallel", "parallel")),
        name="router",
    )(xs, mod, g.reshape(1, d), rw_t, rb.reshape(N_EXPERTS, 1))


def _moe_kernel(h_ref, comb_ref, x_ref, mod_ref, modc_ref, wg_ref, wu_ref, wd_ref, sg_ref, su_ref, sd_ref,
                o_ref, acc_ref, *, lc):
    e = pl.program_id(2)
    h = h_ref[0]

    @pl.when(e == 0)
    def _():
        a = _mm(h, sg_ref[...])
        u = _mm(h, su_ref[...])
        acc_ref[...] = _mm((_silu(a) * u).astype(BF16), sd_ref[...])

    a = _mm(h, wg_ref[0])
    u = _mm(h, wu_ref[0])
    comb = comb_ref[0]
    lane = lax.broadcasted_iota(jnp.int32, comb.shape, 1)
    cw = jnp.sum(jnp.where(lane == e, comb, 0.0), axis=-1, keepdims=True)
    acc_ref[...] += _mm((_silu(a) * u * cw).astype(BF16), wd_ref[0])

    @pl.when(e == pl.num_programs(2) - 1)
    def _():
        row = lax.broadcasted_iota(jnp.int32, acc_ref.shape, 0) + pl.program_id(1) * acc_ref.shape[0]
        gate = jnp.where(row < lc, modc_ref[0, 5:6, :], mod_ref[0, 5:6, :])
        o_ref[0] = x_ref[0] + gate * acc_ref[...]


def _moe(h2, comb, xs, mod, wg, wu, wd, sg, su, sd, tm, lc):
    nb, t, d = xs.shape
    ne, _, ew = wg.shape
    return pl.pallas_call(
        functools.partial(_moe_kernel, lc=lc),
        grid=(nb, t // tm, ne),
        in_specs=[
            pl.BlockSpec((1, tm, d), lambda b, j, e: (b, j, 0)),
            pl.BlockSpec((1, tm, 128), lambda b, j, e: (b, j, 0)),
            pl.BlockSpec((1, tm, d), lambda b, j, e: (b, j, 0)),
            pl.BlockSpec((1, 6, d), lambda b, j, e: (b, 0, 0)),
            pl.BlockSpec((1, 6, d), lambda b, j, e: (nb, 0, 0)),
            pl.BlockSpec((1, d, ew), lambda b, j, e: (e, 0, 0)),
            pl.BlockSpec((1, d, ew), lambda b, j, e: (e, 0, 0)),
            pl.BlockSpec((1, ew, d), lambda b, j, e: (e, 0, 0)),
            pl.BlockSpec(sg.shape, lambda b, j, e: (0, 0)),
            pl.BlockSpec(su.shape, lambda b, j, e: (0, 0)),
            pl.BlockSpec(sd.shape, lambda b, j, e: (0, 0)),
        ],
        out_specs=pl.BlockSpec((1, tm, d), lambda b, j, e: (b, j, 0)),
        out_shape=jax.ShapeDtypeStruct((nb, t, d), F32),
        scratch_shapes=[pltpu.VMEM((tm, d), F32)],
        compiler_params=_params(("parallel", "parallel", "arbitrary"), VMEM_LIMIT),
        name="moe",
    )(h2, comb, xs, mod, mod, wg, wu, wd, sg, su, sd)


def _final_kernel(x_ref, g_ref, o_ref):
    x = x_ref[0]
    o_ref[0] = x * lax.rsqrt(jnp.mean(x * x, axis=-1, keepdims=True) + EPS) * g_ref[...]


def _final_norm(xs, g, lc, tm):
    nb, t, d = xs.shape
    l = t - lc
    return pl.pallas_call(
        _final_kernel,
        grid=(nb, l // tm),
        in_specs=[
            pl.BlockSpec((1, tm, d), lambda b, j: (b, j + lc // tm, 0)),
            pl.BlockSpec((1, d), lambda b, j: (0, 0)),
        ],
        out_specs=pl.BlockSpec((1, tm, d), lambda b, j: (b, j, 0)),
        out_shape=jax.ShapeDtypeStruct((nb, l, d), F32),
        compiler_params=_params(("parallel", "parallel")),
        name="final_norm",
    )(xs, g.reshape(1, d))


def kernel(x, c, ctx, c_ctx, w_ada, b_ada, norm_mix_g, norm_ffn_g, w_in, cm_ln_g, cm_ln_b, cm_w_s, cm_b_s,
           na_rpb, ret_decay_logit, w_branch, w_out, router_w, router_bias, exp_w_gate, exp_w_up,
           exp_w_down, sh_w_gate, sh_w_up, sh_w_down, final_g):
    nb, l, d = x.shape
    lc = ctx.shape[1]
    depth = w_ada.shape[0]
    assert d == D_MODEL and nb < MOD_ROWS and l % GRID_W == 0 and lc % CHUNK == 0 and l % CHUNK == 0
    tm = 256
    assert lc % tm == 0 and l % tm == 0

    cc = jnp.zeros((MOD_ROWS, d), F32).at[:nb].set(c).at[nb].set(c_ctx)
    mod_all = _adaln(cc, w_ada, b_ada).reshape(depth, MOD_ROWS, 6, d)
    cos_t, sin_t = _rope_tables(lc, l)
    xs = jnp.concatenate([ctx, x], axis=1)

    for layer in range(depth):
        mod = mod_all[layer]
        z = _inproj(xs, mod, norm_mix_g[layer], w_in[layer].astype(BF16), tm)
        a_lat = _gmlp(z, cm_ln_g[layer], cm_ln_b[layer], cm_w_s[layer].astype(BF16),
                      jnp.transpose(cm_b_s[layer]))
        b_lat = _nattn(z, _na_bias_table(na_rpb[layer], l // GRID_W), lc)
        ret = _retention(z, ret_decay_logit[layer], cos_t, sin_t, lc)
        xs = _merge(a_lat, b_lat, ret, z, xs, mod, w_branch[layer].astype(BF16),
                    w_out[layer].astype(BF16), tm)
        h2, comb = _router(xs, mod, norm_ffn_g[layer], jnp.transpose(router_w[layer]),
                           router_bias[layer], tm)
        xs = _moe(h2, comb, xs, mod, exp_w_gate[layer].astype(BF16), exp_w_up[layer].astype(BF16),
                  exp_w_down[layer].astype(BF16), sh_w_gate[layer].astype(BF16),
                  sh_w_up[layer].astype(BF16), sh_w_down[layer].astype(BF16), 768, lc)
    return _final_norm(xs, final_g, lc, tm)
```

```python
import functools

import jax
import jax.numpy as jnp
import numpy as np
from jax import lax
from jax.experimental import pallas as pl
from jax.experimental.pallas import tpu as pltpu

F32 = jnp.float32
BF16 = jnp.bfloat16

D_MODEL = 1024
GRID_W = 64
CHUNK = 128
CM_GROUPS = 4
CM_WIDTH = 512
NA_HEADS = 8
NA_HEAD_DIM = 64
NA_WIDTH = NA_HEADS * NA_HEAD_DIM
WIN_R = 8
WIN_C = 16
RET_HEADS = 4
RET_DK = 64
RET_DV = 128
RET_QK_W = RET_HEADS * RET_DK
RET_V_W = RET_HEADS * RET_DV
ROPE_BASE = 10000.0
IN_W = 2 * CM_WIDTH + 3 * NA_WIDTH + 2 * RET_QK_W + 2 * RET_V_W + 3 * D_MODEL
N_EXPERTS = 64
TOP_K = 8
N_GROUPS = 8
TOPK_GROUPS = 4
EXPERT_W = 256
ROUTED_SCALE = 2.5
EPS = 1e-6
NEG = -1e30

OFF_AU, OFF_AV = 0, 512
OFF_NQ, OFF_NK, OFF_NV = 1024, 1536, 2048
OFF_RQ, OFF_RK, OFF_RV, OFF_RG = 2560, 2816, 3072, 3584
OFF_GA, OFF_GB, OFF_GC = 4096, 5120, 6144

MOD_ROWS = 16
VMEM_LIMIT = 56 * 1024 * 1024


def _params(sem, vmem=None):
    return pltpu.CompilerParams(dimension_semantics=sem, vmem_limit_bytes=vmem)


def _silu(v):
    return v * jax.nn.sigmoid(v)


def _mm(a, b):
    return jnp.dot(a, b, preferred_element_type=F32)


def _mm_nt(a, b):
    return lax.dot_general(a, b, (((1,), (1,)), ((), ())), preferred_element_type=F32)


def _mod_row(nb):
    return lambda b, j, *_: (jnp.where(j == 0, nb, b), 0, 0)


def _adaln_kernel(c_ref, w_ref, b_ref, o_ref):
    s = _silu(c_ref[...])
    o_ref[0] = _mm(s, w_ref[0]) + b_ref[0]


def _adaln(cc, w_ada, b_ada):
    depth, d, n6 = w_ada.shape
    tn = 1536
    return pl.pallas_call(
        _adaln_kernel,
        grid=(depth, n6 // tn),
        in_specs=[
            pl.BlockSpec((MOD_ROWS, d), lambda l, n: (0, 0)),
            pl.BlockSpec((1, d, tn), lambda l, n: (l, 0, n)),
            pl.BlockSpec((1, 1, tn), lambda l, n: (l, 0, n)),
        ],
        out_specs=pl.BlockSpec((1, MOD_ROWS, tn), lambda l, n: (l, 0, n)),
        out_shape=jax.ShapeDtypeStruct((depth, MOD_ROWS, n6), F32),
        compiler_params=_params(("parallel", "parallel"), VMEM_LIMIT),
        name="adaln",
    )(cc, w_ada, b_ada.reshape(depth, 1, n6))


def _modulated_norm(x, g, shift, scale):
    y = x * lax.rsqrt(jnp.mean(x * x, axis=-1, keepdims=True) + EPS) * g
    return y * (1.0 + scale) + shift


def _inproj_kernel(x_ref, mod_ref, g_ref, w_ref, z_ref):
    h = _modulated_norm(x_ref[0], g_ref[...], mod_ref[0, 0:1, :], mod_ref[0, 1:2, :])
    hb = h.astype(BF16)
    nw = 512
    for n in range(IN_W // nw):
        z_ref[0, :, n * nw:(n + 1) * nw] = _mm(hb, w_ref[:, n * nw:(n + 1) * nw]).astype(BF16)


def _inproj(xs, mod, g, w_in_b, tm):
    nb, t, d = xs.shape
    return pl.pallas_call(
        _inproj_kernel,
        grid=(nb, t // tm),
        in_specs=[
            pl.BlockSpec((1, tm, d), lambda b, j: (b, j, 0)),
            pl.BlockSpec((1, 6, d), _mod_row(nb)),
            pl.BlockSpec((1, d), lambda b, j: (0, 0)),
            pl.BlockSpec((d, IN_W), lambda b, j: (0, 0)),
        ],
        out_specs=pl.BlockSpec((1, tm, IN_W), lambda b, j: (b, j, 0)),
        out_shape=jax.ShapeDtypeStruct((nb, t, IN_W), BF16),
        compiler_params=_params(("parallel", "parallel"), VMEM_LIMIT),
        name="inproj",
    )(xs, mod, g.reshape(1, d), w_in_b)


def _gmlp_kernel(u_ref, v_ref, lng_ref, lnb_ref, ws_ref, bs_ref, o_ref):
    u = jax.nn.gelu(u_ref[0].astype(F32))
    v = jax.nn.gelu(v_ref[0].astype(F32))
    vc = v - jnp.mean(v, axis=-1, keepdims=True)
    vn = vc * lax.rsqrt(jnp.mean(vc * vc, axis=-1, keepdims=True) + EPS) * lng_ref[...] + lnb_ref[...]
    vb = vn.astype(BF16)
    gw = CM_WIDTH // CM_GROUPS
    for g in range(CM_GROUPS):
        s = _mm(ws_ref[g], vb[:, g * gw:(g + 1) * gw]) + bs_ref[:, g:g + 1]
        o_ref[0, :, g * gw:(g + 1) * gw] = (u[:, g * gw:(g + 1) * gw] * s).astype(BF16)


def _gmlp(z, ln_g, ln_b, ws_b, bs_t):
    nb, t, _ = z.shape
    return pl.pallas_call(
        _gmlp_kernel,
        grid=(nb, t // CHUNK),
        in_specs=[
            pl.BlockSpec((1, CHUNK, CM_WIDTH), lambda b, j: (b, j, OFF_AU // CM_WIDTH)),
            pl.BlockSpec((1, CHUNK, CM_WIDTH), lambda b, j: (b, j, OFF_AV // CM_WIDTH)),
            pl.BlockSpec((1, CM_WIDTH), lambda b, j: (0, 0)),
            pl.BlockSpec((1, CM_WIDTH), lambda b, j: (0, 0)),
            pl.BlockSpec((CM_GROUPS, CHUNK, CHUNK), lambda b, j: (0, 0, 0)),
            pl.BlockSpec((CHUNK, CM_GROUPS), lambda b, j: (0, 0)),
        ],
        out_specs=pl.BlockSpec((1, CHUNK, CM_WIDTH), lambda b, j: (b, j, 0)),
        out_shape=jax.ShapeDtypeStruct((nb, t, CM_WIDTH), BF16),
        compiler_params=_params(("parallel", "parallel")),
        name="gmlp",
    )(z, z, ln_g.reshape(1, -1), ln_b.reshape(1, -1), ws_b, bs_t)


def _na_row_classes(rows):
    half = WIN_R // 2
    wr = min(WIN_R, rows)
    reps = list(range(half)) + [half] + list(range(rows - (wr - half) + 1, rows))
    cls = np.zeros((rows,), np.int32)
    for r in range(rows):
        if r < half:
            cls[r] = r
        elif r <= rows - (wr - half):
            cls[r] = half
        else:
            cls[r] = half + 1 + (r - (rows - (wr - half) + 1))
    return reps, cls


def _na_bias_table(rpb, rows):
    wr = min(WIN_R, rows)
    reps, _ = _na_row_classes(rows)
    nh = rpb.shape[0]
    qc = np.arange(GRID_W)[:, None]
    kc = np.arange(GRID_W)[None, :]
    col_start = np.clip(qc - WIN_C // 2, 0, GRID_W - WIN_C)
    valid = (kc >= col_start) & (kc < col_start + WIN_C)
    pad = GRID_W - WIN_C
    edge = jnp.concatenate([jnp.repeat(rpb[..., :1], pad, axis=-1), rpb.astype(F32),
                            jnp.repeat(rpb[..., -1:], pad, axis=-1)], axis=-1)
    toep = jnp.stack([edge[..., GRID_W - 1 - q:2 * GRID_W - 1 - q] for q in range(GRID_W)], axis=1)
    toep = jnp.where(valid[None, :, None, :], toep, NEG)
    tabs = []
    for r in reps:
        r0 = int(np.clip(r - WIN_R // 2, 0, rows - wr))
        lo = r0 - r + WIN_R - 1
        tabs.append(toep[:, :, lo:lo + wr, :].reshape(nh, GRID_W, wr * GRID_W))
    return jnp.stack(tabs)


def _na_kernel(q_ref, k_ref, v_ref, bias_ref, o_ref, *, lc, rows):
    j = pl.program_id(1)
    n_ctx_blk = lc // GRID_W
    wr = min(WIN_R, rows)
    lane = lax.broadcasted_iota(jnp.int32, (GRID_W, 128), 1)
    scale = NA_HEAD_DIM ** -0.5

    def attend(local):
        q = q_ref[0]
        if local:
            r0 = jnp.clip(j - n_ctx_blk - WIN_R // 2, 0, rows - wr)
            start = pl.multiple_of(lc + r0 * GRID_W, GRID_W)
            k_loc = k_ref[0, pl.ds(start, wr * GRID_W), :]
            v_loc = v_ref[0, pl.ds(start, wr * GRID_W), :]
        k_ctx = k_ref[0, 0:lc, :]
        v_ctx = v_ref[0, 0:lc, :]
        for p in range(NA_HEADS // 2):
            cs = slice(p * 128, (p + 1) * 128)
            q2 = q[:, cs] * scale
            outs = []
            for hh in range(2):
                h = 2 * p + hh
                qm = jnp.where((lane >= 64 * hh) & (lane < 64 * (hh + 1)), q2, 0).astype(BF16)
                s_ctx = _mm_nt(qm, k_ctx[:, cs])
                m = jnp.max(s_ctx, axis=-1, keepdims=True)
                if local:
                    s_loc = _mm_nt(qm, k_loc[:, cs]) + bias_ref[0, h]
                    m = jnp.maximum(m, jnp.max(s_loc, axis=-1, keepdims=True))
                p_ctx = jnp.exp(s_ctx - m)
                den = jnp.sum(p_ctx, axis=-1, keepdims=True)
                o = _mm(p_ctx.astype(BF16), v_ctx[:, cs])
                if local:
                    p_loc = jnp.exp(s_loc - m)
                    den = den + jnp.sum(p_loc, axis=-1, keepdims=True)
                    o = o + _mm(p_loc.astype(BF16), v_loc[:, cs])
                outs.append(o / den)
            o_ref[0, :, cs] = jnp.where(lane < 64, outs[0], outs[1]).astype(BF16)

    @pl.when(j < n_ctx_blk)
    def _():
        attend(False)

    @pl.when(j >= n_ctx_blk)
    def _():
        attend(True)


def _nattn(z, bias_tab, lc):
    nb, t, _ = z.shape
    rows = (t - lc) // GRID_W
    n_ctx_blk = lc // GRID_W
    _, cls = _na_row_classes(rows)
    cls_arr = jnp.asarray(np.concatenate([np.zeros((n_ctx_blk,), np.int32), cls]))
    wr = min(WIN_R, rows)
    grid_spec = pltpu.PrefetchScalarGridSpec(
        num_scalar_prefetch=1,
        grid=(nb, t // GRID_W),
        in_specs=[
            pl.BlockSpec((1, GRID_W, NA_WIDTH), lambda b, j, c: (b, j, OFF_NQ // NA_WIDTH)),
            pl.BlockSpec((1, t, NA_WIDTH), lambda b, j, c: (b, 0, OFF_NK // NA_WIDTH)),
            pl.BlockSpec((1, t, NA_WIDTH), lambda b, j, c: (b, 0, OFF_NV // NA_WIDTH)),
            pl.BlockSpec((1, NA_HEADS, GRID_W, wr * GRID_W), lambda b, j, c: (c[j], 0, 0, 0)),
        ],
        out_specs=pl.BlockSpec((1, GRID_W, NA_WIDTH), lambda b, j, c: (b, j, 0)),
    )
    return pl.pallas_call(
        functools.partial(_na_kernel_sp, lc=lc, rows=rows),
        grid_spec=grid_spec,
        out_shape=jax.ShapeDtypeStruct((nb, t, NA_WIDTH), BF16),
        compiler_params=_params(("parallel", "arbitrary"), VMEM_LIMIT),
        name="nattn",
    )(cls_arr, z, z, z, bias_tab)


def _na_kernel_sp(cls_ref, q_ref, k_ref, v_ref, bias_ref, o_ref, *, lc, rows):
    del cls_ref
    _na_kernel(q_ref, k_ref, v_ref, bias_ref, o_ref, lc=lc, rows=rows)


def _rope_tables(lc, l):
    half = RET_DK // 2
    nf = half // 2
    tpos = np.arange(l)
    inv = ROPE_BASE ** (-jnp.arange(nf, dtype=F32) / nf)

    def part(pos):
        ang = jnp.asarray(pos, F32)[:, None] * inv[None, :]
        c, s = jnp.cos(ang), jnp.sin(ang)
        return jnp.concatenate([c, c], -1), jnp.concatenate([-s, s], -1)

    c_r, s_r = part(tpos // GRID_W)
    c_c, s_c = part(tpos % GRID_W)
    cos = jnp.concatenate([c_r, c_c], -1)
    sin = jnp.concatenate([s_r, s_c], -1)
    cos = jnp.concatenate([jnp.ones((lc, RET_DK), F32), cos], 0)
    sin = jnp.concatenate([jnp.zeros((lc, RET_DK), F32), sin], 0)
    return jnp.tile(cos, (1, RET_HEADS)), jnp.tile(sin, (1, RET_HEADS))


def _ret_kernel(dl_ref, q_ref, k_ref, v_ref, cos_ref, sin_ref, o_ref, st_ref):
    d = pl.program_id(0)
    t = pl.program_id(2)

    @pl.when(t == 0)
    def _():
        st_ref[...] = jnp.zeros_like(st_ref)

    cos = cos_ref[...]
    sin = sin_ref[...]
    lane_qk = lax.broadcasted_iota(jnp.int32, (CHUNK, RET_QK_W), 1)
    first = (lane_qk % (RET_DK // 2)) < (RET_DK // 4)

    def rope(a):
        a = a.astype(F32)
        partner = jnp.where(first, pltpu.roll(a, RET_QK_W - RET_DK // 4, axis=1),
                            pltpu.roll(a, RET_DK // 4, axis=1))
        return a * cos + partner * sin

    q = rope(q_ref[0])
    k = rope(k_ref[0]) * (RET_DK ** -0.5)
    v = v_ref[0]

    ri = lax.broadcasted_iota(jnp.int32, (CHUNK, CHUNK), 0)
    ci = lax.broadcasted_iota(jnp.int32, (CHUNK, CHUNK), 1)
    fwd = d == 0
    dist = jnp.where(fwd, ri - ci, ci - ri)
    keep = dist >= d
    dist_f = jnp.maximum(dist, 0).astype(F32)
    q_exp = jnp.where(fwd, ri + 1, CHUNK - ri).astype(F32)
    k_exp = jnp.where(fwd, CHUNK - 1 - ri, ri).astype(F32)
    lane_half = ci >= RET_DK
    row_half = ri >= RET_DK

    def log_gamma(h):
        xv = jnp.full((CHUNK, CHUNK), dl_ref[d, h], F32)
        return jnp.minimum(xv, 0.0) - jnp.log1p(jnp.exp(-jnp.abs(xv)))

    for p in range(RET_HEADS // 2):
        cs = slice(p * 128, (p + 1) * 128)
        lg = [log_gamma(2 * p), log_gamma(2 * p + 1)]
        q2 = q[:, cs]
        k2 = k[:, cs]
        k2b = k2.astype(BF16)
        kd_t = jnp.transpose(k2 * jnp.exp(k_exp * jnp.where(lane_half, lg[1], lg[0])))
        s2 = st_ref[p]
        s2b = s2.astype(BF16)
        upd = s2 * jnp.exp(float(CHUNK) * jnp.where(row_half, lg[1], lg[0]))
        for hh in range(2):
            h = 2 * p + hh
            vh = v[:, h * RET_DV:(h + 1) * RET_DV]
            sel = lane_half if hh else jnp.logical_not(lane_half)
            qm = jnp.where(sel, q2, 0.0).astype(BF16)
            a = _mm_nt(qm, k2b)
            a = jnp.where(keep, a * jnp.exp(dist_f * lg[hh]), 0.0)
            inner = _mm(a.astype(BF16), vh)
            cross = _mm(qm, s2b) * jnp.exp(q_exp * lg[hh])
            o_ref[0, 0, :, h * RET_DV:(h + 1) * RET_DV] = (inner + cross).astype(o_ref.dtype)
            rsel = row_half if hh else jnp.logical_not(row_half)
            upd = upd + _mm(jnp.where(rsel, kd_t, 0.0).astype(BF16), vh)
        st_ref[p] = upd


def _retention(z, decay_logit, cos_t, sin_t, lc):
    nb, t, _ = z.shape
    nc = t // CHUNK
    ncc = lc // CHUNK

    def chunk(d, s):
        back = jnp.where(s < ncc, ncc - 1 - s, nc - 1 - (s - ncc))
        return jnp.where(d == 0, s, back)

    grid_spec = pltpu.PrefetchScalarGridSpec(
        num_scalar_prefetch=1,
        grid=(2, nb, nc),
        in_specs=[
            pl.BlockSpec((1, CHUNK, RET_QK_W), lambda d, b, s, dl: (b, chunk(d, s), OFF_RQ // RET_QK_W)),
            pl.BlockSpec((1, CHUNK, RET_QK_W), lambda d, b, s, dl: (b, chunk(d, s), OFF_RK // RET_QK_W)),
            pl.BlockSpec((1, CHUNK, RET_V_W), lambda d, b, s, dl: (b, chunk(d, s), OFF_RV // RET_V_W)),
            pl.BlockSpec((CHUNK, RET_QK_W), lambda d, b, s, dl: (chunk(d, s), 0)),
            pl.BlockSpec((CHUNK, RET_QK_W), lambda d, b, s, dl: (chunk(d, s), 0)),
        ],
        out_specs=pl.BlockSpec((1, 1, CHUNK, RET_V_W), lambda d, b, s, dl: (d, b, chunk(d, s), 0)),
        scratch_shapes=[pltpu.VMEM((RET_HEADS // 2, 2 * RET_DK, RET_DV), F32)],
    )
    return pl.pallas_call(
        _ret_kernel,
        grid_spec=grid_spec,
        out_shape=jax.ShapeDtypeStruct((2, nb, t, RET_V_W), BF16),
        compiler_params=_params(("parallel", "parallel", "arbitrary")),
        name="retention",
    )(decay_logit, z, z, z, cos_t, sin_t)


def _merge_kernel(a_ref, b_ref, rf_ref, rb_ref, rg_ref, ga_ref, gb_ref, gc_ref, x_ref, mod_ref,
                  wb_ref, wo_ref, o_ref):
    o = rf_ref[0, 0].astype(F32) + rb_ref[0, 0].astype(F32)
    parts = []
    for h in range(RET_HEADS):
        oh = o[:, h * RET_DV:(h + 1) * RET_DV]
        parts.append(oh * lax.rsqrt(jnp.mean(oh * oh, axis=-1, keepdims=True) + EPS))
    c = jnp.concatenate(parts, axis=-1) * _silu(rg_ref[0].astype(F32))
    m = (jax.nn.sigmoid(ga_ref[0].astype(F32)) * _mm(a_ref[0], wb_ref[0])
         + jax.nn.sigmoid(gb_ref[0].astype(F32)) * _mm(b_ref[0], wb_ref[1])
         + jax.nn.sigmoid(gc_ref[0].astype(F32)) * _mm(c.astype(BF16), wb_ref[2]))
    mix = _mm(m.astype(BF16), wo_ref[...])
    o_ref[0] = x_ref[0] + mod_ref[0, 2:3, :] * mix


def _merge(a_lat, b_lat, ret, z, xs, mod, wb_b, wo_b, tm):
    nb, t, d = xs.shape
    bw = CM_WIDTH
    return pl.pallas_call(
        _merge_kernel,
        grid=(nb, t // tm),
        in_specs=[
            pl.BlockSpec((1, tm, bw), lambda b, j: (b, j, 0)),
            pl.BlockSpec((1, tm, bw), lambda b, j: (b, j, 0)),
            pl.BlockSpec((1, 1, tm, bw), lambda b, j: (0, b, j, 0)),
            pl.BlockSpec((1, 1, tm, bw), lambda b, j: (1, b, j, 0)),
            pl.BlockSpec((1, tm, RET_V_W), lambda b, j: (b, j, OFF_RG // RET_V_W)),
            pl.BlockSpec((1, tm, d), lambda b, j: (b, j, OFF_GA // d)),
            pl.BlockSpec((1, tm, d), lambda b, j: (b, j, OFF_GB // d)),
            pl.BlockSpec((1, tm, d), lambda b, j: (b, j, OFF_GC // d)),
            pl.BlockSpec((1, tm, d), lambda b, j: (b, j, 0)),
            pl.BlockSpec((1, 6, d), _mod_row(nb)),
            pl.BlockSpec((3, bw, d), lambda b, j: (0, 0, 0)),
            pl.BlockSpec((d, d), lambda b, j: (0, 0)),
        ],
        out_specs=pl.BlockSpec((1, tm, d), lambda b, j: (b, j, 0)),
        out_shape=jax.ShapeDtypeStruct((nb, t, d), F32),
        compiler_params=_params(("parallel", "parallel"), VMEM_LIMIT),
        name="merge",
    )(a_lat, b_lat, ret, ret, z, z, z, z, xs, mod, wb_b, wo_b)


def _first_index(mask, idx, big):
    return jnp.min(jnp.where(mask, idx, big), axis=0, keepdims=True)


def _router_kernel(x_ref, mod_ref, g_ref, rwt_ref, rb_ref, h_ref, comb_ref):
    h = _modulated_norm(x_ref[0], g_ref[...], mod_ref[0, 3:4, :], mod_ref[0, 4:5, :])
    h_ref[0] = h.astype(BF16)
    tm = h.shape[0]
    logits = _mm_nt(rwt_ref[...], h)
    scores = jax.nn.sigmoid(logits)
    biased = scores + rb_ref[:, 0:1]
    gsz = N_EXPERTS // N_GROUPS
    sub = lax.broadcasted_iota(jnp.int32, (gsz, tm), 0)
    gscore = []
    for g in range(N_GROUPS):
        blk = biased[g * gsz:(g + 1) * gsz]
        m1 = jnp.max(blk, axis=0, keepdims=True)
        i1 = _first_index(blk == m1, sub, gsz)
        m2 = jnp.max(jnp.where(sub == i1, -jnp.inf, blk), axis=0, keepdims=True)
        gscore.append(m1 + m2)
    gs = jnp.concatenate(gscore, axis=0)
    gidx = lax.broadcasted_iota(jnp.int32, (N_GROUPS, tm), 0)
    gsel = jnp.zeros((N_GROUPS, tm), jnp.int32)
    for _ in range(TOPK_GROUPS):
        m = jnp.max(gs, axis=0, keepdims=True)
        pick = gidx == _first_index(gs == m, gidx, N_GROUPS)
        gsel = jnp.where(pick, 1, gsel)
        gs = jnp.where(pick, -jnp.inf, gs)
    cand = jnp.concatenate(
        [jnp.where(gsel[g:g + 1] > 0, biased[g * gsz:(g + 1) * gsz], -jnp.inf) for g in range(N_GROUPS)],
        axis=0)
    eidx = lax.broadcasted_iota(jnp.int32, (N_EXPERTS, tm), 0)
    w = jnp.zeros((N_EXPERTS, tm), F32)
    for _ in range(TOP_K):
        m = jnp.max(cand, axis=0, keepdims=True)
        pick = eidx == _first_index(cand == m, eidx, N_EXPERTS)
        w = jnp.where(pick, scores, w)
        cand = jnp.where(pick, -jnp.inf, cand)
    w = w / jnp.sum(w, axis=0, keepdims=True) * ROUTED_SCALE
    wp = jnp.concatenate([w, jnp.zeros((128 - N_EXPERTS, tm), F32)], axis=0)
    comb_ref[0] = jnp.transpose(wp)


def _router(xs, mod, g, rw_t, rb, tm):
    nb, t, d = xs.shape
    return pl.pallas_call(
        _router_kernel,
        grid=(nb, t // tm),
        in_specs=[
            pl.BlockSpec((1, tm, d), lambda b, j: (b, j, 0)),
            pl.BlockSpec((1, 6, d), _mod_row(nb)),
            pl.BlockSpec((1, d), lambda b, j: (0, 0)),
            pl.BlockSpec((N_EXPERTS, d), lambda b, j: (0, 0)),
            pl.BlockSpec((N_EXPERTS, 1), lambda b, j: (0, 0)),
        ],
        out_specs=[
            pl.BlockSpec((1, tm, d), lambda b, j: (b, j, 0)),
            pl.BlockSpec((1, tm, 128), lambda b, j: (b, j, 0)),
        ],
        out_shape=[jax.ShapeDtypeStruct((nb, t, d), BF16), jax.ShapeDtypeStruct((nb, t, 128), F32)],
        compiler_params=_params(("parallel", "parallel")),
        name="router",
    )(xs, mod, g.reshape(1, d), rw_t, rb.reshape(N_EXPERTS, 1))


def _moe_kernel(h_ref, comb_ref, x_ref, mod_ref, modc_ref, wg_ref, wu_ref, wd_ref, sg_ref, su_ref, sd_ref,
                o_ref, acc_ref, *, lc):
    e = pl.program_id(2)
    h = h_ref[0]

    @pl.when(e == 0)
    def _():
        a = _mm(h, sg_ref[...])
        u = _mm(h, su_ref[...])
        acc_ref[...] = _mm((_silu(a) * u).astype(BF16), sd_ref[...])

    a = _mm(h, wg_ref[0])
    u = _mm(h, wu_ref[0])
    comb = comb_ref[0]
    lane = lax.broadcasted_iota(jnp.int32, comb.shape, 1)
    cw = jnp.sum(jnp.where(lane == e, comb, 0.0), axis=-1, keepdims=True)
    acc_ref[...] += _mm((_silu(a) * u * cw).astype(BF16), wd_ref[0])

    @pl.when(e == pl.num_programs(2) - 1)
    def _():
        row = lax.broadcasted_iota(jnp.int32, acc_ref.shape, 0) + pl.program_id(1) * acc_ref.shape[0]
        gate = jnp.where(row < lc, modc_ref[0, 5:6, :], mod_ref[0, 5:6, :])
        o_ref[0] = x_ref[0] + gate * acc_ref[...]


def _moe(h2, comb, xs, mod, wg, wu, wd, sg, su, sd, tm, lc):
    nb, t, d = xs.shape
    ne, _, ew = wg.shape
    return pl.pallas_call(
        functools.partial(_moe_kernel, lc=lc),
        grid=(nb, t // tm, ne),
        in_specs=[
            pl.BlockSpec((1, tm, d), lambda b, j, e: (b, j, 0)),
            pl.BlockSpec((1, tm, 128), lambda b, j, e: (b, j, 0)),
            pl.BlockSpec((1, tm, d), lambda b, j, e: (b, j, 0)),
            pl.BlockSpec((1, 6, d), lambda b, j, e: (b, 0, 0)),
            pl.BlockSpec((1, 6, d), lambda b, j, e: (nb, 0, 0)),
            pl.BlockSpec((1, d, ew), lambda b, j, e: (e, 0, 0)),
            pl.BlockSpec((1, d, ew), lambda b, j, e: (e, 0, 0)),
            pl.BlockSpec((1, ew, d), lambda b, j, e: (e, 0, 0)),
            pl.BlockSpec(sg.shape, lambda b, j, e: (0, 0)),
            pl.BlockSpec(su.shape, lambda b, j, e: (0, 0)),
            pl.BlockSpec(sd.shape, lambda b, j, e: (0, 0)),
        ],
        out_specs=pl.BlockSpec((1, tm, d), lambda b, j, e: (b, j, 0)),
        out_shape=jax.ShapeDtypeStruct((nb, t, d), F32),
        scratch_shapes=[pltpu.VMEM((tm, d), F32)],
        compiler_params=_params(("parallel", "parallel", "arbitrary"), VMEM_LIMIT),
        name="moe",
    )(h2, comb, xs, mod, mod, wg, wu, wd, sg, su, sd)


def _final_kernel(x_ref, g_ref, o_ref):
    x = x_ref[0]
    o_ref[0] = x * lax.rsqrt(jnp.mean(x * x, axis=-1, keepdims=True) + EPS) * g_ref[...]


def _final_norm(xs, g, lc, tm):
    nb, t, d = xs.shape
    l = t - lc
    return pl.pallas_call(
        _final_kernel,
        grid=(nb, l // tm),
        in_specs=[
            pl.BlockSpec((1, tm, d), lambda b, j: (b, j + lc // tm, 0)),
            pl.BlockSpec((1, d), lambda b, j: (0, 0)),
        ],
        out_specs=pl.BlockSpec((1, tm, d), lambda b, j: (b, j, 0)),
        out_shape=jax.ShapeDtypeStruct((nb, l, d), F32),
        compiler_params=_params(("parallel", "parallel")),
        name="final_norm",
    )(xs, g.reshape(1, d))


def kernel(x, c, ctx, c_ctx, w_ada, b_ada, norm_mix_g, norm_ffn_g, w_in, cm_ln_g, cm_ln_b, cm_w_s, cm_b_s,
           na_rpb, ret_decay_logit, w_branch, w_out, router_w, router_bias, exp_w_gate, exp_w_up,
           exp_w_down, sh_w_gate, sh_w_up, sh_w_down, final_g):
    nb, l, d = x.shape
    lc = ctx.shape[1]
    depth = w_ada.shape[0]
    assert d == D_MODEL and nb < MOD_ROWS and l % GRID_W == 0 and lc % CHUNK == 0 and l % CHUNK == 0
    tm = 256
    assert lc % tm == 0 and l % tm == 0

    cc = jnp.zeros((MOD_ROWS, d), F32).at[:nb].set(c).at[nb].set(c_ctx)
    mod_all = _adaln(cc, w_ada, b_ada).reshape(depth, MOD_ROWS, 6, d)
    cos_t, sin_t = _rope_tables(lc, l)
    xs = jnp.concatenate([ctx, x], axis=1)

    for layer in range(depth):
        mod = mod_all[layer]
        z = _inproj(xs, mod, norm_mix_g[layer], w_in[layer].astype(BF16), tm)
        a_lat = _gmlp(z, cm_ln_g[layer], cm_ln_b[layer], cm_w_s[layer].astype(BF16),
                      jnp.transpose(cm_b_s[layer]))
        b_lat = _nattn(z, _na_bias_table(na_rpb[layer], l // GRID_W), lc)
        ret = _retention(z, ret_decay_logit[layer], cos_t, sin_t, lc)
        xs = _merge(a_lat, b_lat, ret, z, xs, mod, w_branch[layer].astype(BF16),
                    w_out[layer].astype(BF16), tm)
        h2, comb = _router(xs, mod, norm_ffn_g[layer], jnp.transpose(router_w[layer]),
                           router_bias[layer], tm)
        xs = _moe(h2, comb, xs, mod, exp_w_gate[layer].astype(BF16), exp_w_up[layer].astype(BF16),
                  exp_w_down[layer].astype(BF16), sh_w_gate[layer].astype(BF16),
                  sh_w_up[layer].astype(BF16), sh_w_down[layer].astype(BF16), 768, lc)
    return _final_norm(xs, final_g, lc, tm)
```

```python
import functools

import jax
import jax.numpy as jnp
import numpy as np
from jax import lax
from jax.experimental import pallas as pl
from jax.experimental.pallas import tpu as pltpu

F32 = jnp.float32
BF16 = jnp.bfloat16

D_MODEL = 1024
GRID_W = 64
CHUNK = 128
CM_GROUPS = 4
CM_WIDTH = 512
NA_HEADS = 8
NA_HEAD_DIM = 64
NA_WIDTH = NA_HEADS * NA_HEAD_DIM
WIN_R = 8
WIN_C = 16
RET_HEADS = 4
RET_DK = 64
RET_DV = 128
RET_QK_W = RET_HEADS * RET_DK
RET_V_W = RET_HEADS * RET_DV
ROPE_BASE = 10000.0
IN_W = 2 * CM_WIDTH + 3 * NA_WIDTH + 2 * RET_QK_W + 2 * RET_V_W + 3 * D_MODEL
N_EXPERTS = 64
TOP_K = 8
N_GROUPS = 8
TOPK_GROUPS = 4
EXPERT_W = 256
ROUTED_SCALE = 2.5
EPS = 1e-6
NEG = -1e30

OFF_AU, OFF_AV = 0, 512
OFF_NQ, OFF_NK, OFF_NV = 1024, 1536, 2048
OFF_RQ, OFF_RK, OFF_RV, OFF_RG = 2560, 2816, 3072, 3584
OFF_GA, OFF_GB, OFF_GC = 4096, 5120, 6144

MOD_ROWS = 16
VMEM_LIMIT = 56 * 1024 * 1024


def _params(sem, vmem=None):
    return pltpu.CompilerParams(dimension_semantics=sem, vmem_limit_bytes=vmem)


def _silu(v):
    return v * jax.nn.sigmoid(v)


def _mm(a, b):
    return jnp.dot(a, b, preferred_element_type=F32)


def _mm_nt(a, b):
    return lax.dot_general(a, b, (((1,), (1,)), ((), ())), preferred_element_type=F32)


def _mod_row(nb):
    return lambda b, j, *_: (jnp.where(j == 0, nb, b), 0, 0)


def _adaln_kernel(c_ref, w_ref, b_ref, o_ref):
    s = _silu(c_ref[...])
    o_ref[0] = _mm(s, w_ref[0]) + b_ref[0]


def _adaln(cc, w_ada, b_ada):
    depth, d, n6 = w_ada.shape
    tn = 1536
    return pl.pallas_call(
        _adaln_kernel,
        grid=(depth, n6 // tn),
        in_specs=[
            pl.BlockSpec((MOD_ROWS, d), lambda l, n: (0, 0)),
            pl.BlockSpec((1, d, tn), lambda l, n: (l, 0, n)),
            pl.BlockSpec((1, 1, tn), lambda l, n: (l, 0, n)),
        ],
        out_specs=pl.BlockSpec((1, MOD_ROWS, tn), lambda l, n: (l, 0, n)),
        out_shape=jax.ShapeDtypeStruct((depth, MOD_ROWS, n6), F32),
        compiler_params=_params(("parallel", "parallel"), VMEM_LIMIT),
        name="adaln",
    )(cc, w_ada, b_ada.reshape(depth, 1, n6))


def _modulated_norm(x, g, shift, scale):
    y = x * lax.rsqrt(jnp.mean(x * x, axis=-1, keepdims=True) + EPS) * g
    return y * (1.0 + scale) + shift


def _inproj_kernel(x_ref, mod_ref, g_ref, w_ref, z_ref):
    h = _modulated_norm(x_ref[0], g_ref[...], mod_ref[0, 0:1, :], mod_ref[0, 1:2, :])
    hb = h.astype(BF16)
    nw = 512
    for n in range(IN_W // nw):
        z_ref[0, :, n * nw:(n + 1) * nw] = _mm(hb, w_ref[:, n * nw:(n + 1) * nw]).astype(BF16)


def _inproj(xs, mod, g, w_in_b, tm):
    nb, t, d = xs.shape
    return pl.pallas_call(
        _inproj_kernel,
        grid=(nb, t // tm),
        in_specs=[
            pl.BlockSpec((1, tm, d), lambda b, j: (b, j, 0)),
            pl.BlockSpec((1, 6, d), _mod_row(nb)),
            pl.BlockSpec((1, d), lambda b, j: (0, 0)),
            pl.BlockSpec((d, IN_W), lambda b, j: (0, 0)),
        ],
        out_specs=pl.BlockSpec((1, tm, IN_W), lambda b, j: (b, j, 0)),
        out_shape=jax.ShapeDtypeStruct((nb, t, IN_W), BF16),
        compiler_params=_params(("parallel", "parallel"), VMEM_LIMIT),
        name="inproj",
    )(xs, mod, g.reshape(1, d), w_in_b)


def _gmlp_kernel(u_ref, v_ref, lng_ref, lnb_ref, ws_ref, bs_ref, o_ref):
    u = jax.nn.gelu(u_ref[0].astype(F32))
    v = jax.nn.gelu(v_ref[0].astype(F32))
    vc = v - jnp.mean(v, axis=-1, keepdims=True)
    vn = vc * lax.rsqrt(jnp.mean(vc * vc, axis=-1, keepdims=True) + EPS) * lng_ref[...] + lnb_ref[...]
    vb = vn.astype(BF16)
    gw = CM_WIDTH // CM_GROUPS
    for g in range(CM_GROUPS):
        s = _mm(ws_ref[g], vb[:, g * gw:(g + 1) * gw]) + bs_ref[:, g:g + 1]
        o_ref[0, :, g * gw:(g + 1) * gw] = (u[:, g * gw:(g + 1) * gw] * s).astype(BF16)


def _gmlp(z, ln_g, ln_b, ws_b, bs_t):
    nb, t, _ = z.shape
    return pl.pallas_call(
        _gmlp_kernel,
        grid=(nb, t // CHUNK),
        in_specs=[
            pl.BlockSpec((1, CHUNK, CM_WIDTH), lambda b, j: (b, j, OFF_AU // CM_WIDTH)),
            pl.BlockSpec((1, CHUNK, CM_WIDTH), lambda b, j: (b, j, OFF_AV // CM_WIDTH)),
            pl.BlockSpec((1, CM_WIDTH), lambda b, j: (0, 0)),
            pl.BlockSpec((1, CM_WIDTH), lambda b, j: (0, 0)),
            pl.BlockSpec((CM_GROUPS, CHUNK, CHUNK), lambda b, j: (0, 0, 0)),
            pl.BlockSpec((CHUNK, CM_GROUPS), lambda b, j: (0, 0)),
        ],
        out_specs=pl.BlockSpec((1, CHUNK, CM_WIDTH), lambda b, j: (b, j, 0)),
        out_shape=jax.ShapeDtypeStruct((nb, t, CM_WIDTH), BF16),
        compiler_params=_params(("parallel", "parallel")),
        name="gmlp",
    )(z, z, ln_g.reshape(1, -1), ln_b.reshape(1, -1), ws_b, bs_t)


def _na_row_classes(rows):
    half = WIN_R // 2
    wr = min(WIN_R, rows)
    reps = list(range(half)) + [half] + list(range(rows - (wr - half) + 1, rows))
    cls = np.zeros((rows,), np.int32)
    for r in range(rows):
        if r < half:
            cls[r] = r
        elif r <= rows - (wr - half):
            cls[r] = half
        else:
            cls[r] = half + 1 + (r - (rows - (wr - half) + 1))
    return reps, cls


def _na_bias_table(rpb, rows):
    wr = min(WIN_R, rows)
    reps, _ = _na_row_classes(rows)
    nh = rpb.shape[0]
    qc = np.arange(GRID_W)[:, None]
    kc = np.arange(GRID_W)[None, :]
    col_start = np.clip(qc - WIN_C // 2, 0, GRID_W - WIN_C)
    valid = (kc >= col_start) & (kc < col_start + WIN_C)
    pad = GRID_W - WIN_C
    edge = jnp.concatenate([jnp.repeat(rpb[..., :1], pad, axis=-1), rpb.astype(F32),
                            jnp.repeat(rpb[..., -1:], pad, axis=-1)], axis=-1)
    toep = jnp.stack([edge[..., GRID_W - 1 - q:2 * GRID_W - 1 - q] for q in range(GRID_W)], axis=1)
    toep = jnp.where(valid[None, :, None, :], toep, NEG)
    tabs = []
    for r in reps:
        r0 = int(np.clip(r - WIN_R // 2, 0, rows - wr))
        lo = r0 - r + WIN_R - 1
        tabs.append(toep[:, :, lo:lo + wr, :].reshape(nh, GRID_W, wr * GRID_W))
    return jnp.stack(tabs)


def _na_kernel(q_ref, k_ref, v_ref, bias_ref, o_ref, *, lc, rows):
    j = pl.program_id(1)
    n_ctx_blk = lc // GRID_W
    wr = min(WIN_R, rows)
    lane = lax.broadcasted_iota(jnp.int32, (GRID_W, 128), 1)
    scale = NA_HEAD_DIM ** -0.5

    def attend(local):
        q = q_ref[0]
        if local:
            r0 = jnp.clip(j - n_ctx_blk - WIN_R // 2, 0, rows - wr)
            start = pl.multiple_of(lc + r0 * GRID_W, GRID_W)
            k_loc = k_ref[0, pl.ds(start, wr * GRID_W), :]
            v_loc = v_ref[0, pl.ds(start, wr * GRID_W), :]
        k_ctx = k_ref[0, 0:lc, :]
        v_ctx = v_ref[0, 0:lc, :]
        for p in range(NA_HEADS // 2):
            cs = slice(p * 128, (p + 1) * 128)
            q2 = q[:, cs] * scale
            outs = []
            for hh in range(2):
                h = 2 * p + hh
                qm = jnp.where((lane >= 64 * hh) & (lane < 64 * (hh + 1)), q2, 0).astype(BF16)
                s_ctx = _mm_nt(qm, k_ctx[:, cs])
                m = jnp.max(s_ctx, axis=-1, keepdims=True)
                if local:
                    s_loc = _mm_nt(qm, k_loc[:, cs]) + bias_ref[0, h]
                    m = jnp.maximum(m, jnp.max(s_loc, axis=-1, keepdims=True))
                p_ctx = jnp.exp(s_ctx - m)
                den = jnp.sum(p_ctx, axis=-1, keepdims=True)
                o = _mm(p_ctx.astype(BF16), v_ctx[:, cs])
                if local:
                    p_loc = jnp.exp(s_loc - m)
                    den = den + jnp.sum(p_loc, axis=-1, keepdims=True)
                    o = o + _mm(p_loc.astype(BF16), v_loc[:, cs])
                outs.append(o / den)
            o_ref[0, :, cs] = jnp.where(lane < 64, outs[0], outs[1]).astype(BF16)

    @pl.when(j < n_ctx_blk)
    def _():
        attend(False)

    @pl.when(j >= n_ctx_blk)
    def _():
        attend(True)


def _nattn(z, bias_tab, lc):
    nb, t, _ = z.shape
    rows = (t - lc) // GRID_W
    n_ctx_blk = lc // GRID_W
    _, cls = _na_row_classes(rows)
    cls_arr = jnp.asarray(np.concatenate([np.zeros((n_ctx_blk,), np.int32), cls]))
    wr = min(WIN_R, rows)
    grid_spec = pltpu.PrefetchScalarGridSpec(
        num_scalar_prefetch=1,
        grid=(nb, t // GRID_W),
        in_specs=[
            pl.BlockSpec((1, GRID_W, NA_WIDTH), lambda b, j, c: (b, j, OFF_NQ // NA_WIDTH)),
            pl.BlockSpec((1, t, NA_WIDTH), lambda b, j, c: (b, 0, OFF_NK // NA_WIDTH)),
            pl.BlockSpec((1, t, NA_WIDTH), lambda b, j, c: (b, 0, OFF_NV // NA_WIDTH)),
            pl.BlockSpec((1, NA_HEADS, GRID_W, wr * GRID_W), lambda b, j, c: (c[j], 0, 0, 0)),
        ],
        out_specs=pl.BlockSpec((1, GRID_W, NA_WIDTH), lambda b, j, c: (b, j, 0)),
    )
    return pl.pallas_call(
        functools.partial(_na_kernel_sp, lc=lc, rows=rows),
        grid_spec=grid_spec,
        out_shape=jax.ShapeDtypeStruct((nb, t, NA_WIDTH), BF16),
        compiler_params=_params(("parallel", "arbitrary"), VMEM_LIMIT),
        name="nattn",
    )(cls_arr, z, z, z, bias_tab)


def _na_kernel_sp(cls_ref, q_ref, k_ref, v_ref, bias_ref, o_ref, *, lc, rows):
    del cls_ref
    _na_kernel(q_ref, k_ref, v_ref, bias_ref, o_ref, lc=lc, rows=rows)


def _rope_tables(lc, l):
    half = RET_DK // 2
    nf = half // 2
    tpos = np.arange(l)
    inv = ROPE_BASE ** (-jnp.arange(nf, dtype=F32) / nf)

    def part(pos):
        ang = jnp.asarray(pos, F32)[:, None] * inv[None, :]
        c, s = jnp.cos(ang), jnp.sin(ang)
        return jnp.concatenate([c, c], -1), jnp.concatenate([-s, s], -1)

    c_r, s_r = part(tpos // GRID_W)
    c_c, s_c = part(tpos % GRID_W)
    cos = jnp.concatenate([c_r, c_c], -1)
    sin = jnp.concatenate([s_r, s_c], -1)
    cos = jnp.concatenate([jnp.ones((lc, RET_DK), F32), cos], 0)
    sin = jnp.concatenate([jnp.zeros((lc, RET_DK), F32), sin], 0)
    return jnp.tile(cos, (1, RET_HEADS)), jnp.tile(sin, (1, RET_HEADS))


def _ret_kernel(dl_ref, q_ref, k_ref, v_ref, cos_ref, sin_ref, o_ref, st_ref):
    d = pl.program_id(0)
    t = pl.program_id(2)

    @pl.when(t == 0)
    def _():
        st_ref[...] = jnp.zeros_like(st_ref)

    cos = cos_ref[...]
    sin = sin_ref[...]
    lane_qk = lax.broadcasted_iota(jnp.int32, (CHUNK, RET_QK_W), 1)
    first = (lane_qk % (RET_DK // 2)) < (RET_DK // 4)

    def rope(a):
        a = a.astype(F32)
        partner = jnp.where(first, pltpu.roll(a, RET_QK_W - RET_DK // 4, axis=1),
                            pltpu.roll(a, RET_DK // 4, axis=1))
        return a * cos + partner * sin

    q = rope(q_ref[0])
    k = rope(k_ref[0]) * (RET_DK ** -0.5)
    v = v_ref[0]

    ri = lax.broadcasted_iota(jnp.int32, (CHUNK, CHUNK), 0)
    ci = lax.broadcasted_iota(jnp.int32, (CHUNK, CHUNK), 1)
    fwd = d == 0
    dist = jnp.where(fwd, ri - ci, ci - ri)
    keep = dist >= d
    dist_f = jnp.maximum(dist, 0).astype(F32)
    q_exp = jnp.where(fwd, ri + 1, CHUNK - ri).astype(F32)
    k_exp = jnp.where(fwd, CHUNK - 1 - ri, ri).astype(F32)
    lane_half = ci >= RET_DK
    row_half = ri >= RET_DK

    def log_gamma(h):
        xv = jnp.full((CHUNK, CHUNK), dl_ref[d, h], F32)
        return jnp.minimum(xv, 0.0) - jnp.log1p(jnp.exp(-jnp.abs(xv)))

    for p in range(RET_HEADS // 2):
        cs = slice(p * 128, (p + 1) * 128)
        lg = [log_gamma(2 * p), log_gamma(2 * p + 1)]
        q2 = q[:, cs]
        k2 = k[:, cs]
        k2b = k2.astype(BF16)
        kd_t = jnp.transpose(k2 * jnp.exp(k_exp * jnp.where(lane_half, lg[1], lg[0])))
        s2 = st_ref[p]
        s2b = s2.astype(BF16)
        upd = s2 * jnp.exp(float(CHUNK) * jnp.where(row_half, lg[1], lg[0]))
        for hh in range(2):
            h = 2 * p + hh
            vh = v[:, h * RET_DV:(h + 1) * RET_DV]
            sel = lane_half if hh else jnp.logical_not(lane_half)
            qm = jnp.where(sel, q2, 0.0).astype(BF16)
            a = _mm_nt(qm, k2b)
            a = jnp.where(keep, a * jnp.exp(dist_f * lg[hh]), 0.0)
            inner = _mm(a.astype(BF16), vh)
            cross = _mm(qm, s2b) * jnp.exp(q_exp * lg[hh])
            o_ref[0, 0, :, h * RET_DV:(h + 1) * RET_DV] = (inner + cross).astype(o_ref.dtype)
            rsel = row_half if hh else jnp.logical_not(row_half)
            upd = upd + _mm(jnp.where(rsel, kd_t, 0.0).astype(BF16), vh)
        st_ref[p] = upd


def _retention(z, decay_logit, cos_t, sin_t, lc):
    nb, t, _ = z.shape
    nc = t // CHUNK
    ncc = lc // CHUNK

    def chunk(d, s):
        back = jnp.where(s < ncc, ncc - 1 - s, nc - 1 - (s - ncc))
        return jnp.where(d == 0, s, back)

    grid_spec = pltpu.PrefetchScalarGridSpec(
        num_scalar_prefetch=1,
        grid=(2, nb, nc),
        in_specs=[
            pl.BlockSpec((1, CHUNK, RET_QK_W), lambda d, b, s, dl: (b, chunk(d, s), OFF_RQ // RET_QK_W)),
            pl.BlockSpec((1, CHUNK, RET_QK_W), lambda d, b, s, dl: (b, chunk(d, s), OFF_RK // RET_QK_W)),
            pl.BlockSpec((1, CHUNK, RET_V_W), lambda d, b, s, dl: (b, chunk(d, s), OFF_RV // RET_V_W)),
            pl.BlockSpec((CHUNK, RET_QK_W), lambda d, b, s, dl: (chunk(d, s), 0)),
            pl.BlockSpec((CHUNK, RET_QK_W), lambda d, b, s, dl: (chunk(d, s), 0)),
        ],
        out_specs=pl.BlockSpec((1, 1, CHUNK, RET_V_W), lambda d, b, s, dl: (d, b, chunk(d, s), 0)),
        scratch_shapes=[pltpu.VMEM((RET_HEADS // 2, 2 * RET_DK, RET_DV), F32)],
    )
    return pl.pallas_call(
        _ret_kernel,
        grid_spec=grid_spec,
        out_shape=jax.ShapeDtypeStruct((2, nb, t, RET_V_W), BF16),
        compiler_params=_params(("parallel", "parallel", "arbitrary")),
        name="retention",
    )(decay_logit, z, z, z, cos_t, sin_t)


def _merge_kernel(a_ref, b_ref, rf_ref, rb_ref, rg_ref, ga_ref, gb_ref, gc_ref, x_ref, mod_ref,
                  wb_ref, wo_ref, o_ref):
    o = rf_ref[0, 0].astype(F32) + rb_ref[0, 0].astype(F32)
    parts = []
    for h in range(RET_HEADS):
        oh = o[:, h * RET_DV:(h + 1) * RET_DV]
        parts.append(oh * lax.rsqrt(jnp.mean(oh * oh, axis=-1, keepdims=True) + EPS))
    c = jnp.concatenate(parts, axis=-1) * _silu(rg_ref[0].astype(F32))
    m = (jax.nn.sigmoid(ga_ref[0].astype(F32)) * _mm(a_ref[0], wb_ref[0])
         + jax.nn.sigmoid(gb_ref[0].astype(F32)) * _mm(b_ref[0], wb_ref[1])
         + jax.nn.sigmoid(gc_ref[0].astype(F32)) * _mm(c.astype(BF16), wb_ref[2]))
    mix = _mm(m.astype(BF16), wo_ref[...])
    o_ref[0] = x_ref[0] + mod_ref[0, 2:3, :] * mix


def _merge(a_lat, b_lat, ret, z, xs, mod, wb_b, wo_b, tm):
    nb, t, d = xs.shape
    bw = CM_WIDTH
    return pl.pallas_call(
        _merge_kernel,
        grid=(nb, t // tm),
        in_specs=[
            pl.BlockSpec((1, tm, bw), lambda b, j: (b, j, 0)),
            pl.BlockSpec((1, tm, bw), lambda b, j: (b, j, 0)),
            pl.BlockSpec((1, 1, tm, bw), lambda b, j: (0, b, j, 0)),
            pl.BlockSpec((1, 1, tm, bw), lambda b, j: (1, b, j, 0)),
            pl.BlockSpec((1, tm, RET_V_W), lambda b, j: (b, j, OFF_RG // RET_V_W)),
            pl.BlockSpec((1, tm, d), lambda b, j: (b, j, OFF_GA // d)),
            pl.BlockSpec((1, tm, d), lambda b, j: (b, j, OFF_GB // d)),
            pl.BlockSpec((1, tm, d), lambda b, j: (b, j, OFF_GC // d)),
            pl.BlockSpec((1, tm, d), lambda b, j: (b, j, 0)),
            pl.BlockSpec((1, 6, d), _mod_row(nb)),
            pl.BlockSpec((3, bw, d), lambda b, j: (0, 0, 0)),
            pl.BlockSpec((d, d), lambda b, j: (0, 0)),
        ],
        out_specs=pl.BlockSpec((1, tm, d), lambda b, j: (b, j, 0)),
        out_shape=jax.ShapeDtypeStruct((nb, t, d), F32),
        compiler_params=_params(("parallel", "parallel"), VMEM_LIMIT),
        name="merge",
    )(a_lat, b_lat, ret, ret, z, z, z, z, xs, mod, wb_b, wo_b)


def _first_index(mask, idx, big):
    return jnp.min(jnp.where(mask, idx, big), axis=0, keepdims=True)


def _router_kernel(x_ref, mod_ref, g_ref, rwt_ref, rb_ref, h_ref, wt_ref):
    h = _modulated_norm(x_ref[0], g_ref[...], mod_ref[0, 3:4, :], mod_ref[0, 4:5, :])
    h_ref[0] = h
    tm = h.shape[0]
    logits = _mm_nt(rwt_ref[...], h)
    scores = jax.nn.sigmoid(logits)
    biased = scores + rb_ref[:, 0:1]
    gsz = N_EXPERTS // N_GROUPS
    sub = lax.broadcasted_iota(jnp.int32, (gsz, tm), 0)
    gscore = []
    for g in range(N_GROUPS):
        blk = biased[g * gsz:(g + 1) * gsz]
        m1 = jnp.max(blk, axis=0, keepdims=True)
        i1 = _first_index(blk == m1, sub, gsz)
        m2 = jnp.max(jnp.where(sub == i1, -jnp.inf, blk), axis=0, keepdims=True)
        gscore.append(m1 + m2)
    gs = jnp.concatenate(gscore, axis=0)
    gidx = lax.broadcasted_iota(jnp.int32, (N_GROUPS, tm), 0)
    gsel = jnp.zeros((N_GROUPS, tm), jnp.int32)
    for _ in range(TOPK_GROUPS):
        m = jnp.max(gs, axis=0, keepdims=True)
        pick = gidx == _first_index(gs == m, gidx, N_GROUPS)
        gsel = jnp.where(pick, 1, gsel)
        gs = jnp.where(pick, -jnp.inf, gs)
    cand = jnp.concatenate(
        [jnp.where(gsel[g:g + 1] > 0, biased[g * gsz:(g + 1) * gsz], -jnp.inf) for g in range(N_GROUPS)],
        axis=0)
    eidx = lax.broadcasted_iota(jnp.int32, (N_EXPERTS, tm), 0)
    w = jnp.zeros((N_EXPERTS, tm), F32)
    for _ in range(TOP_K):
        m = jnp.max(cand, axis=0, keepdims=True)
        pick = eidx == _first_index(cand == m, eidx, N_EXPERTS)
        w = jnp.where(pick, scores, w)
        cand = jnp.where(pick, -jnp.inf, cand)
    wt_ref[0] = w / jnp.sum(w, axis=0, keepdims=True) * ROUTED_SCALE


def _router(xs, mod, g, rw_t, rb, tm):
    nb, t, d = xs.shape
    return pl.pallas_call(
        _router_kernel,
        grid=(nb, t // tm),
        in_specs=[
            pl.BlockSpec((1, tm, d), lambda b, j: (b, j, 0)),
            pl.BlockSpec((1, 6, d), _mod_row(nb)),
            pl.BlockSpec((1, d), lambda b, j: (0, 0)),
            pl.BlockSpec((N_EXPERTS, d), lambda b, j: (0, 0)),
            pl.BlockSpec((N_EXPERTS, 1), lambda b, j: (0, 0)),
        ],
        out_specs=[
            pl.BlockSpec((1, tm, d), lambda b, j: (b, j, 0)),
            pl.BlockSpec((1, N_EXPERTS, tm), lambda b, j: (b, 0, j)),
        ],
        out_shape=[jax.ShapeDtypeStruct((nb, t, d), F32), jax.ShapeDtypeStruct((nb, N_EXPERTS, t), F32)],
        compiler_params=_params(("parallel", "parallel")),
        name="router",
    )(xs, mod, g.reshape(1, d), rw_t, rb.reshape(N_EXPERTS, 1))


ROW_SLAB = D_MODEL // 128
MOE_TILE = 384
MOE_GROUP = 8


def _dispatch(w_t):
    t = w_t.shape[-1]
    pos = lax.broadcasted_iota(jnp.int32, w_t.shape, 2)
    routed = w_t > 0
    skey, sw = lax.sort((jnp.where(routed, pos, pos + t), w_t), dimension=2, num_keys=1)
    idx = jnp.where(skey >= t, skey - t, skey)
    return idx, sw, jnp.sum(routed, axis=-1, dtype=jnp.int32)


def _experts_kernel(cnt_ref, idx_ref, w_ref, src_ref, wg_ref, wu_ref, wd_ref, acc_ref, stage_ref, ostage_ref):
    ne = pl.num_programs(1)
    e = pl.program_id(1)
    count = cnt_ref[pl.program_id(0) * ne + e]

    @pl.when(e == 0)
    def _():
        acc_ref[...] = jnp.zeros_like(acc_ref)
        stage_ref[...] = jnp.zeros_like(stage_ref)

    def slab(r):
        return pl.ds(pl.multiple_of(r * ROW_SLAB, ROW_SLAB), ROW_SLAB)

    def tile(ti, carry):
        base = ti * MOE_TILE
        ngrp = (jnp.minimum(count - base, MOE_TILE) + MOE_GROUP - 1) // MOE_GROUP

        def gather(g, c):
            for u in range(MOE_GROUP):
                r = g * MOE_GROUP + u
                stage_ref[slab(r), :] = src_ref[0, slab(idx_ref[0, 0, base + r]), :]
            return c

        lax.fori_loop(0, ngrp, gather, 0)
        x = jnp.concatenate([stage_ref[pl.ds(j, MOE_TILE, stride=ROW_SLAB), :] for j in range(ROW_SLAB)],
                            axis=-1).astype(BF16)
        act = _silu(_mm(x, wg_ref[0])) * _mm(x, wu_ref[0])
        wcol = jnp.concatenate(
            [jnp.transpose(jnp.broadcast_to(
                w_ref[0, :, pl.ds(pl.multiple_of(base + 128 * k, 128), 128)], (128, 128)))
             for k in range(MOE_TILE // 128)], axis=0)
        act = act * jnp.concatenate([wcol] * (EXPERT_W // 128), axis=-1)
        y = _mm(act.astype(BF16), wd_ref[0])
        for j in range(ROW_SLAB):
            ostage_ref[pl.ds(j, MOE_TILE, stride=ROW_SLAB), :] = y[:, j * 128:(j + 1) * 128]

        def scatter(g, c):
            rows = [g * MOE_GROUP + u for u in range(MOE_GROUP)]
            dst = [slab(idx_ref[0, 0, base + r]) for r in rows]
            vals = [acc_ref[0, d, :] + ostage_ref[slab(r), :] for d, r in zip(dst, rows)]
            for d, v in zip(dst, vals):
                acc_ref[0, d, :] = v
            return c

        lax.fori_loop(0, ngrp, scatter, 0)
        return carry

    lax.fori_loop(0, (count + MOE_TILE - 1) // MOE_TILE, tile, 0)


def _experts(h2, idx, sw, cnt, wg, wu, wd):
    nb, t, d = h2.shape
    ne, _, ew = wg.shape
    assert d == D_MODEL and ew == EXPERT_W and t % MOE_TILE == 0
    grid_spec = pltpu.PrefetchScalarGridSpec(
        num_scalar_prefetch=1,
        grid=(nb, ne),
        in_specs=[
            pl.BlockSpec((1, 1, t), lambda b, e, c: (b * ne + e, 0, 0), memory_space=pltpu.SMEM),
            pl.BlockSpec((1, 1, t), lambda b, e, c: (b * ne + e, 0, 0)),
            pl.BlockSpec((1, t * ROW_SLAB, 128), lambda b, e, c: (b, 0, 0)),
            pl.BlockSpec((1, d, ew), lambda b, e, c: (e, 0, 0)),
            pl.BlockSpec((1, d, ew), lambda b, e, c: (e, 0, 0)),
            pl.BlockSpec((1, ew, d), lambda b, e, c: (e, 0, 0)),
        ],
        out_specs=pl.BlockSpec((1, t * ROW_SLAB, 128), lambda b, e, c: (b, 0, 0)),
        scratch_shapes=[pltpu.VMEM((MOE_TILE * ROW_SLAB, 128), F32),
                        pltpu.VMEM((MOE_TILE * ROW_SLAB, 128), F32)],
    )
    acc = pl.pallas_call(
        _experts_kernel,
        grid_spec=grid_spec,
        out_shape=jax.ShapeDtypeStruct((nb, t * ROW_SLAB, 128), F32),
        compiler_params=_params(("parallel", "arbitrary"), VMEM_LIMIT),
        name="experts",
    )(cnt.reshape(nb * ne), idx.reshape(nb * ne, 1, t), sw.reshape(nb * ne, 1, t),
      h2.reshape(nb, t * ROW_SLAB, 128), wg, wu, wd)
    return acc.reshape(nb, t, d)


def _moe_out_kernel(h_ref, r_ref, x_ref, mod_ref, sg_ref, su_ref, sd_ref, o_ref):
    h = h_ref[0].astype(BF16)
    shared = _mm((_silu(_mm(h, sg_ref[...])) * _mm(h, su_ref[...])).astype(BF16), sd_ref[...])
    o_ref[0] = x_ref[0] + mod_ref[0, 5:6, :] * (r_ref[0] + shared)


def _moe_out(h2, routed, xs, mod, sg, su, sd, tm):
    nb, t, d = xs.shape
    return pl.pallas_call(
        _moe_out_kernel,
        grid=(nb, t // tm),
        in_specs=[
            pl.BlockSpec((1, tm, d), lambda b, j: (b, j, 0)),
            pl.BlockSpec((1, tm, d), lambda b, j: (b, j, 0)),
            pl.BlockSpec((1, tm, d), lambda b, j: (b, j, 0)),
            pl.BlockSpec((1, 6, d), _mod_row(nb)),
            pl.BlockSpec(sg.shape, lambda b, j: (0, 0)),
            pl.BlockSpec(su.shape, lambda b, j: (0, 0)),
            pl.BlockSpec(sd.shape, lambda b, j: (0, 0)),
        ],
        out_specs=pl.BlockSpec((1, tm, d), lambda b, j: (b, j, 0)),
        out_shape=jax.ShapeDtypeStruct((nb, t, d), F32),
        compiler_params=_params(("parallel", "parallel")),
        name="moe_out",
    )(h2, routed, xs, mod, sg, su, sd)


def _final_kernel(x_ref, g_ref, o_ref):
    x = x_ref[0]
    o_ref[0] = x * lax.rsqrt(jnp.mean(x * x, axis=-1, keepdims=True) + EPS) * g_ref[...]


def _final_norm(xs, g, lc, tm):
    nb, t, d = xs.shape
    l = t - lc
    return pl.pallas_call(
        _final_kernel,
        grid=(nb, l // tm),
        in_specs=[
            pl.BlockSpec((1, tm, d), lambda b, j: (b, j + lc // tm, 0)),
            pl.BlockSpec((1, d), lambda b, j: (0, 0)),
        ],
        out_specs=pl.BlockSpec((1, tm, d), lambda b, j: (b, j, 0)),
        out_shape=jax.ShapeDtypeStruct((nb, l, d), F32),
        compiler_params=_params(("parallel", "parallel")),
        name="final_norm",
    )(xs, g.reshape(1, d))


def kernel(x, c, ctx, c_ctx, w_ada, b_ada, norm_mix_g, norm_ffn_g, w_in, cm_ln_g, cm_ln_b, cm_w_s, cm_b_s,
           na_rpb, ret_decay_logit, w_branch, w_out, router_w, router_bias, exp_w_gate, exp_w_up,
           exp_w_down, sh_w_gate, sh_w_up, sh_w_down, final_g):
    nb, l, d = x.shape
    lc = ctx.shape[1]
    depth = w_ada.shape[0]
    assert d == D_MODEL and nb < MOD_ROWS and l % GRID_W == 0 and lc % CHUNK == 0 and l % CHUNK == 0
    tm = 256
    assert lc % tm == 0 and l % tm == 0

    cc = jnp.zeros((MOD_ROWS, d), F32).at[:nb].set(c).at[nb].set(c_ctx)
    mod_all = _adaln(cc, w_ada, b_ada).reshape(depth, MOD_ROWS, 6, d)
    cos_t, sin_t = _rope_tables(lc, l)
    xs = jnp.concatenate([ctx, x], axis=1)

    for layer in range(depth):
        mod = mod_all[layer]
        z = _inproj(xs, mod, norm_mix_g[layer], w_in[layer].astype(BF16), tm)
        a_lat = _gmlp(z, cm_ln_g[layer], cm_ln_b[layer], cm_w_s[layer].astype(BF16),
                      jnp.transpose(cm_b_s[layer]))
        b_lat = _nattn(z, _na_bias_table(na_rpb[layer], l // GRID_W), lc)
        ret = _retention(z, ret_decay_logit[layer], cos_t, sin_t, lc)
        xs = _merge(a_lat, b_lat, ret, z, xs, mod, w_branch[layer].astype(BF16),
                    w_out[layer].astype(BF16), tm)
        h2, w_t = _router(xs, mod, norm_ffn_g[layer], jnp.transpose(router_w[layer]),
                          router_bias[layer], tm)
        idx, sw, cnt = _dispatch(w_t)
        routed = _experts(h2, idx, sw, cnt, exp_w_gate[layer].astype(BF16), exp_w_up[layer].astype(BF16),
                          exp_w_down[layer].astype(BF16))
        xs = _moe_out(h2, routed, xs, mod, sh_w_gate[layer].astype(BF16), sh_w_up[layer].astype(BF16),
                      sh_w_down[layer].astype(BF16), tm)
    return _final_norm(xs, final_g, lc, tm)
```

```python
import functools

import jax
import jax.numpy as jnp
import numpy as np
from jax import lax
from jax.experimental import pallas as pl
from jax.experimental.pallas import tpu as pltpu

F32 = jnp.float32
BF16 = jnp.bfloat16

D_MODEL = 1024
GRID_W = 64
CHUNK = 128
CM_GROUPS = 4
CM_WIDTH = 512
NA_HEADS = 8
NA_HEAD_DIM = 64
NA_WIDTH = NA_HEADS * NA_HEAD_DIM
WIN_R = 8
WIN_C = 16
RET_HEADS = 4
RET_DK = 64
RET_DV = 128
RET_QK_W = RET_HEADS * RET_DK
RET_V_W = RET_HEADS * RET_DV
ROPE_BASE = 10000.0
IN_W = 2 * CM_WIDTH + 3 * NA_WIDTH + 2 * RET_QK_W + 2 * RET_V_W + 3 * D_MODEL
N_EXPERTS = 64
TOP_K = 8
N_GROUPS = 8
TOPK_GROUPS = 4
EXPERT_W = 256
ROUTED_SCALE = 2.5
EPS = 1e-6
NEG = -1e30

OFF_AU, OFF_AV = 0, 512
OFF_NQ, OFF_NK, OFF_NV = 1024, 1536, 2048
OFF_RQ, OFF_RK, OFF_RV, OFF_RG = 2560, 2816, 3072, 3584
OFF_GA, OFF_GB, OFF_GC = 4096, 5120, 6144

MOD_ROWS = 16
VMEM_LIMIT = 56 * 1024 * 1024


def _params(sem, vmem=None):
    return pltpu.CompilerParams(dimension_semantics=sem, vmem_limit_bytes=vmem)


def _silu(v):
    return v * jax.nn.sigmoid(v)


def _mm(a, b):
    return jnp.dot(a, b, preferred_element_type=F32)


def _mm_nt(a, b):
    return lax.dot_general(a, b, (((1,), (1,)), ((), ())), preferred_element_type=F32)


def _mm_tn(a, b):
    return lax.dot_general(a, b, (((0,), (0,)), ((), ())), preferred_element_type=F32)


def _mod_row(nb):
    return lambda b, j, *_: (jnp.where(j == 0, nb, b), 0, 0)


def _adaln_kernel(c_ref, w_ref, b_ref, o_ref):
    s = _silu(c_ref[...])
    o_ref[0] = _mm(s, w_ref[0]) + b_ref[0]


def _adaln(cc, w_ada, b_ada):
    depth, d, n6 = w_ada.shape
    tn = 1536
    return pl.pallas_call(
        _adaln_kernel,
        grid=(depth, n6 // tn),
        in_specs=[
            pl.BlockSpec((MOD_ROWS, d), lambda l, n: (0, 0)),
            pl.BlockSpec((1, d, tn), lambda l, n: (l, 0, n)),
            pl.BlockSpec((1, 1, tn), lambda l, n: (l, 0, n)),
        ],
        out_specs=pl.BlockSpec((1, MOD_ROWS, tn), lambda l, n: (l, 0, n)),
        out_shape=jax.ShapeDtypeStruct((depth, MOD_ROWS, n6), F32),
        compiler_params=_params(("parallel", "parallel"), VMEM_LIMIT),
        name="adaln",
    )(cc, w_ada, b_ada.reshape(depth, 1, n6))


def _modulated_norm(x, g, shift, scale):
    y = x * lax.rsqrt(jnp.mean(x * x, axis=-1, keepdims=True) + EPS) * g
    return y * (1.0 + scale) + shift


def _inproj_kernel(x_ref, mod_ref, g_ref, w_ref, z_ref):
    h = _modulated_norm(x_ref[0], g_ref[...], mod_ref[0, 0:1, :], mod_ref[0, 1:2, :])
    hb = h.astype(BF16)
    nw = 512
    for n in range(IN_W // nw):
        z_ref[0, :, n * nw:(n + 1) * nw] = _mm(hb, w_ref[:, n * nw:(n + 1) * nw]).astype(BF16)


def _inproj(xs, mod, g, w_in_b, tm):
    nb, t, d = xs.shape
    return pl.pallas_call(
        _inproj_kernel,
        grid=(nb, t // tm),
        in_specs=[
            pl.BlockSpec((1, tm, d), lambda b, j: (b, j, 0)),
            pl.BlockSpec((1, 6, d), _mod_row(nb)),
            pl.BlockSpec((1, d), lambda b, j: (0, 0)),
            pl.BlockSpec((d, IN_W), lambda b, j: (0, 0)),
        ],
        out_specs=pl.BlockSpec((1, tm, IN_W), lambda b, j: (b, j, 0)),
        out_shape=jax.ShapeDtypeStruct((nb, t, IN_W), BF16),
        compiler_params=_params(("parallel", "parallel"), VMEM_LIMIT),
        name="inproj",
    )(xs, mod, g.reshape(1, d), w_in_b)


def _gmlp_kernel(u_ref, v_ref, lng_ref, lnb_ref, ws_ref, bs_ref, o_ref):
    u = jax.nn.gelu(u_ref[0].astype(F32))
    v = jax.nn.gelu(v_ref[0].astype(F32))
    vc = v - jnp.mean(v, axis=-1, keepdims=True)
    vn = vc * lax.rsqrt(jnp.mean(vc * vc, axis=-1, keepdims=True) + EPS) * lng_ref[...] + lnb_ref[...]
    vb = vn.astype(BF16)
    gw = CM_WIDTH // CM_GROUPS
    for g in range(CM_GROUPS):
        s = _mm(ws_ref[g], vb[:, g * gw:(g + 1) * gw]) + bs_ref[:, g:g + 1]
        o_ref[0, :, g * gw:(g + 1) * gw] = (u[:, g * gw:(g + 1) * gw] * s).astype(BF16)


def _gmlp(z, ln_g, ln_b, ws_b, bs_t):
    nb, t, _ = z.shape
    return pl.pallas_call(
        _gmlp_kernel,
        grid=(nb, t // CHUNK),
        in_specs=[
            pl.BlockSpec((1, CHUNK, CM_WIDTH), lambda b, j: (b, j, OFF_AU // CM_WIDTH)),
            pl.BlockSpec((1, CHUNK, CM_WIDTH), lambda b, j: (b, j, OFF_AV // CM_WIDTH)),
            pl.BlockSpec((1, CM_WIDTH), lambda b, j: (0, 0)),
            pl.BlockSpec((1, CM_WIDTH), lambda b, j: (0, 0)),
            pl.BlockSpec((CM_GROUPS, CHUNK, CHUNK), lambda b, j: (0, 0, 0)),
            pl.BlockSpec((CHUNK, CM_GROUPS), lambda b, j: (0, 0)),
        ],
        out_specs=pl.BlockSpec((1, CHUNK, CM_WIDTH), lambda b, j: (b, j, 0)),
        out_shape=jax.ShapeDtypeStruct((nb, t, CM_WIDTH), BF16),
        compiler_params=_params(("parallel", "parallel")),
        name="gmlp",
    )(z, z, ln_g.reshape(1, -1), ln_b.reshape(1, -1), ws_b, bs_t)


def _na_row_classes(rows):
    half = WIN_R // 2
    wr = min(WIN_R, rows)
    reps = list(range(half)) + [half] + list(range(rows - (wr - half) + 1, rows))
    cls = np.zeros((rows,), np.int32)
    for r in range(rows):
        if r < half:
            cls[r] = r
        elif r <= rows - (wr - half):
            cls[r] = half
        else:
            cls[r] = half + 1 + (r - (rows - (wr - half) + 1))
    return reps, cls


def _na_bias_table(rpb, rows):
    wr = min(WIN_R, rows)
    reps, _ = _na_row_classes(rows)
    nh = rpb.shape[0]
    qc = np.arange(GRID_W)[:, None]
    kc = np.arange(GRID_W)[None, :]
    col_start = np.clip(qc - WIN_C // 2, 0, GRID_W - WIN_C)
    valid = (kc >= col_start) & (kc < col_start + WIN_C)
    pad = GRID_W - WIN_C
    edge = jnp.concatenate([jnp.repeat(rpb[..., :1], pad, axis=-1), rpb.astype(F32),
                            jnp.repeat(rpb[..., -1:], pad, axis=-1)], axis=-1)
    toep = jnp.stack([edge[..., GRID_W - 1 - q:2 * GRID_W - 1 - q] for q in range(GRID_W)], axis=1)
    toep = jnp.where(valid[None, :, None, :], toep, NEG)
    tabs = []
    for r in reps:
        r0 = int(np.clip(r - WIN_R // 2, 0, rows - wr))
        lo = r0 - r + WIN_R - 1
        tabs.append(toep[:, :, lo:lo + wr, :].reshape(nh, GRID_W, wr * GRID_W))
    tab = jnp.stack(tabs)
    tab = tab.reshape(len(reps), nh // 2, 2, GRID_W, wr * GRID_W)
    return jnp.transpose(tab, (0, 1, 4, 2, 3)).reshape(len(reps), nh // 2, wr * GRID_W, 2 * GRID_W)


def _t_blocks(p):
    return jnp.concatenate([jnp.transpose(p[i:i + 128]) for i in range(0, p.shape[0], 128)],
                           axis=1).astype(BF16)


NA_ROWS = 2


def _na_kernel(cls_ref, q_ref, k_ref, v_ref, *refs, lc, rows):
    del cls_ref
    j = pl.program_id(1)
    n_ctx_blk = lc // GRID_W
    wr = min(WIN_R, rows)
    sub = lax.broadcasted_iota(jnp.int32, (128, 128), 0)
    lane = lax.broadcasted_iota(jnp.int32, (128, 128), 1)
    same_head = (sub // 64) == (lane // 64)
    first_head = lax.broadcasted_iota(jnp.int32, (GRID_W, 128), 1) < 64
    scale = NA_HEAD_DIM ** -0.5
    bias_refs, o_ref = refs[:-1], refs[-1]

    def attend(local, rr):
        qs = slice(rr * GRID_W, (rr + 1) * GRID_W)
        bias_ref = bias_refs[rr]
        q = q_ref[0, qs, :]
        if local:
            r0 = jnp.clip(j * NA_ROWS + rr - n_ctx_blk - WIN_R // 2, 0, rows - wr)
            start = pl.multiple_of(lc + r0 * GRID_W, GRID_W)
            k_loc = k_ref[0, pl.ds(start, wr * GRID_W), :]
            v_loc = v_ref[0, pl.ds(start, wr * GRID_W), :]
        k_ctx = k_ref[0, 0:lc, :]
        v_ctx = v_ref[0, 0:lc, :]
        for p in range(NA_HEADS // 2):
            cs = slice(p * 128, (p + 1) * 128)
            q2 = q[:, cs].astype(F32) * scale
            qbd = jnp.transpose(jnp.where(same_head, jnp.concatenate([q2, q2], axis=0), 0.0)).astype(BF16)
            s_ctx = _mm(k_ctx[:, cs], qbd)
            m = jnp.max(s_ctx, axis=0, keepdims=True)
            if local:
                s_loc = _mm(k_loc[:, cs], qbd) + bias_ref[0, p]
                m = jnp.maximum(m, jnp.max(s_loc, axis=0, keepdims=True))
            p_ctx = jnp.exp(s_ctx - m)
            den = jnp.sum(p_ctx, axis=0, keepdims=True)
            if local:
                p_loc = jnp.exp(s_loc - m)
                den = den + jnp.sum(p_loc, axis=0, keepdims=True)
            inv = 1.0 / den
            o = _mm(_t_blocks(p_ctx * inv), v_ctx[:, cs])
            if local:
                o = o + _mm(_t_blocks(p_loc * inv), v_loc[:, cs])
            o_ref[0, qs, cs] = jnp.where(first_head, o[:GRID_W], o[GRID_W:]).astype(BF16)

    @pl.when(j * NA_ROWS < n_ctx_blk)
    def _():
        for rr in range(NA_ROWS):
            attend(False, rr)

    @pl.when(j * NA_ROWS >= n_ctx_blk)
    def _():
        for rr in range(NA_ROWS):
            attend(True, rr)


def _nattn(z, bias_tab, lc):
    nb, t, _ = z.shape
    rows = (t - lc) // GRID_W
    n_ctx_blk = lc // GRID_W
    _, cls = _na_row_classes(rows)
    cls_arr = jnp.asarray(np.concatenate([np.zeros((n_ctx_blk,), np.int32), cls]))
    wr = min(WIN_R, rows)
    qb = NA_ROWS * GRID_W
    assert n_ctx_blk % NA_ROWS == 0 and rows % NA_ROWS == 0

    def bias_spec(rr):
        return pl.BlockSpec((1, NA_HEADS // 2, wr * GRID_W, 2 * GRID_W),
                            lambda b, j, c: (c[j * NA_ROWS + rr], 0, 0, 0))

    grid_spec = pltpu.PrefetchScalarGridSpec(
        num_scalar_prefetch=1,
        grid=(nb, t // qb),
        in_specs=[
            pl.BlockSpec((1, qb, NA_WIDTH), lambda b, j, c: (b, j, OFF_NQ // NA_WIDTH)),
            pl.BlockSpec((1, t, NA_WIDTH), lambda b, j, c: (b, 0, OFF_NK // NA_WIDTH)),
            pl.BlockSpec((1, t, NA_WIDTH), lambda b, j, c: (b, 0, OFF_NV // NA_WIDTH)),
        ] + [bias_spec(rr) for rr in range(NA_ROWS)],
        out_specs=pl.BlockSpec((1, qb, NA_WIDTH), lambda b, j, c: (b, j, 0)),
    )
    return pl.pallas_call(
        functools.partial(_na_kernel, lc=lc, rows=rows),
        grid_spec=grid_spec,
        out_shape=jax.ShapeDtypeStruct((nb, t, NA_WIDTH), BF16),
        compiler_params=_params(("parallel", "arbitrary"), VMEM_LIMIT),
        name="nattn",
    )(cls_arr, z, z, z, *([bias_tab] * NA_ROWS))


def _rope_tables(lc, l):
    half = RET_DK // 2
    nf = half // 2
    tpos = np.arange(l)
    inv = ROPE_BASE ** (-jnp.arange(nf, dtype=F32) / nf)

    def part(pos):
        ang = jnp.asarray(pos, F32)[:, None] * inv[None, :]
        c, s = jnp.cos(ang), jnp.sin(ang)
        return jnp.concatenate([c, c], -1), jnp.concatenate([-s, s], -1)

    c_r, s_r = part(tpos // GRID_W)
    c_c, s_c = part(tpos % GRID_W)
    cos = jnp.concatenate([c_r, c_c], -1)
    sin = jnp.concatenate([s_r, s_c], -1)
    cos = jnp.concatenate([jnp.ones((lc, RET_DK), F32), cos], 0)
    sin = jnp.concatenate([jnp.zeros((lc, RET_DK), F32), sin], 0)
    return jnp.tile(cos, (1, RET_HEADS)), jnp.tile(sin, (1, RET_HEADS))


def _ret_kernel(dl_ref, q_ref, k_ref, v_ref, cos_ref, sin_ref, o_ref, st_ref):
    d = pl.program_id(0)
    t = pl.program_id(2)

    @pl.when(t == 0)
    def _():
        st_ref[...] = jnp.zeros_like(st_ref)

    cos = cos_ref[...]
    sin = sin_ref[...]
    lane_qk = lax.broadcasted_iota(jnp.int32, (CHUNK, RET_QK_W), 1)
    first = (lane_qk % (RET_DK // 2)) < (RET_DK // 4)

    def rope(a):
        a = a.astype(F32)
        partner = jnp.where(first, pltpu.roll(a, RET_QK_W - RET_DK // 4, axis=1),
                            pltpu.roll(a, RET_DK // 4, axis=1))
        return a * cos + partner * sin

    q = rope(q_ref[0])
    k = rope(k_ref[0]) * (RET_DK ** -0.5)
    v = v_ref[0]

    ri = lax.broadcasted_iota(jnp.int32, (CHUNK, CHUNK), 0)
    ci = lax.broadcasted_iota(jnp.int32, (CHUNK, CHUNK), 1)
    fwd = d == 0
    dist = jnp.where(fwd, ri - ci, ci - ri)
    keep = dist >= d
    dist_f = jnp.maximum(dist, 0).astype(F32)
    q_exp = jnp.where(fwd, ri + 1, CHUNK - ri).astype(F32)
    k_exp = jnp.where(fwd, CHUNK - 1 - ri, ri).astype(F32)
    lane_half = ci >= RET_DK
    row_half = ri >= RET_DK

    def log_gamma(h):
        xv = jnp.full((CHUNK, CHUNK), dl_ref[d, h], F32)
        return jnp.minimum(xv, 0.0) - jnp.log1p(jnp.exp(-jnp.abs(xv)))

    for p in range(RET_HEADS // 2):
        cs = slice(p * 128, (p + 1) * 128)
        lg = [log_gamma(2 * p), log_gamma(2 * p + 1)]
        q2 = q[:, cs]
        k2 = k[:, cs]
        k2b = k2.astype(BF16)
        kd_t = jnp.transpose(k2 * jnp.exp(k_exp * jnp.where(lane_half, lg[1], lg[0])))
        s2 = st_ref[p]
        s2b = s2.astype(BF16)
        upd = s2 * jnp.exp(float(CHUNK) * jnp.where(row_half, lg[1], lg[0]))
        for hh in range(2):
            h = 2 * p + hh
            vh = v[:, h * RET_DV:(h + 1) * RET_DV]
            sel = lane_half if hh else jnp.logical_not(lane_half)
            qm = jnp.where(sel, q2, 0.0).astype(BF16)
            a = _mm_nt(qm, k2b)
            a = jnp.where(keep, a * jnp.exp(dist_f * lg[hh]), 0.0)
            inner = _mm(a.astype(BF16), vh)
            cross = _mm(qm, s2b) * jnp.exp(q_exp * lg[hh])
            o_ref[0, 0, :, h * RET_DV:(h + 1) * RET_DV] = (inner + cross).astype(o_ref.dtype)
            rsel = row_half if hh else jnp.logical_not(row_half)
            upd = upd + _mm(jnp.where(rsel, kd_t, 0.0).astype(BF16), vh)
        st_ref[p] = upd


def _retention(z, decay_logit, cos_t, sin_t, lc):
    nb, t, _ = z.shape
    nc = t // CHUNK
    ncc = lc // CHUNK

    def chunk(d, s):
        back = jnp.where(s < ncc, ncc - 1 - s, nc - 1 - (s - ncc))
        return jnp.where(d == 0, s, back)

    grid_spec = pltpu.PrefetchScalarGridSpec(
        num_scalar_prefetch=1,
        grid=(2, nb, nc),
        in_specs=[
            pl.BlockSpec((1, CHUNK, RET_QK_W), lambda d, b, s, dl: (b, chunk(d, s), OFF_RQ // RET_QK_W)),
            pl.BlockSpec((1, CHUNK, RET_QK_W), lambda d, b, s, dl: (b, chunk(d, s), OFF_RK // RET_QK_W)),
            pl.BlockSpec((1, CHUNK, RET_V_W), lambda d, b, s, dl: (b, chunk(d, s), OFF_RV // RET_V_W)),
            pl.BlockSpec((CHUNK, RET_QK_W), lambda d, b, s, dl: (chunk(d, s), 0)),
            pl.BlockSpec((CHUNK, RET_QK_W), lambda d, b, s, dl: (chunk(d, s), 0)),
        ],
        out_specs=pl.BlockSpec((1, 1, CHUNK, RET_V_W), lambda d, b, s, dl: (d, b, chunk(d, s), 0)),
        scratch_shapes=[pltpu.VMEM((RET_HEADS // 2, 2 * RET_DK, RET_DV), F32)],
    )
    return pl.pallas_call(
        _ret_kernel,
        grid_spec=grid_spec,
        out_shape=jax.ShapeDtypeStruct((2, nb, t, RET_V_W), BF16),
        compiler_params=_params(("parallel", "parallel", "arbitrary")),
        name="retention",
    )(decay_logit, z, z, z, cos_t, sin_t)


def _merge_kernel(a_ref, b_ref, rf_ref, rb_ref, rg_ref, ga_ref, gb_ref, gc_ref, x_ref, mod_ref,
                  wb_ref, wo_ref, o_ref):
    o = rf_ref[0, 0].astype(F32) + rb_ref[0, 0].astype(F32)
    parts = []
    for h in range(RET_HEADS):
        oh = o[:, h * RET_DV:(h + 1) * RET_DV]
        parts.append(oh * lax.rsqrt(jnp.mean(oh * oh, axis=-1, keepdims=True) + EPS))
    c = jnp.concatenate(parts, axis=-1) * _silu(rg_ref[0].astype(F32))
    m = (jax.nn.sigmoid(ga_ref[0].astype(F32)) * _mm(a_ref[0], wb_ref[0])
         + jax.nn.sigmoid(gb_ref[0].astype(F32)) * _mm(b_ref[0], wb_ref[1])
         + jax.nn.sigmoid(gc_ref[0].astype(F32)) * _mm(c.astype(BF16), wb_ref[2]))
    mix = _mm(m.astype(BF16), wo_ref[...])
    o_ref[0] = x_ref[0] + mod_ref[0, 2:3, :] * mix


def _merge(a_lat, b_lat, ret, z, xs, mod, wb_b, wo_b, tm):
    nb, t, d = xs.shape
    bw = CM_WIDTH
    return pl.pallas_call(
        _merge_kernel,
        grid=(nb, t // tm),
        in_specs=[
            pl.BlockSpec((1, tm, bw), lambda b, j: (b, j, 0)),
            pl.BlockSpec((1, tm, bw), lambda b, j: (b, j, 0)),
            pl.BlockSpec((1, 1, tm, bw), lambda b, j: (0, b, j, 0)),
            pl.BlockSpec((1, 1, tm, bw), lambda b, j: (1, b, j, 0)),
            pl.BlockSpec((1, tm, RET_V_W), lambda b, j: (b, j, OFF_RG // RET_V_W)),
            pl.BlockSpec((1, tm, d), lambda b, j: (b, j, OFF_GA // d)),
            pl.BlockSpec((1, tm, d), lambda b, j: (b, j, OFF_GB // d)),
            pl.BlockSpec((1, tm, d), lambda b, j: (b, j, OFF_GC // d)),
            pl.BlockSpec((1, tm, d), lambda b, j: (b, j, 0)),
            pl.BlockSpec((1, 6, d), _mod_row(nb)),
            pl.BlockSpec((3, bw, d), lambda b, j: (0, 0, 0)),
            pl.BlockSpec((d, d), lambda b, j: (0, 0)),
        ],
        out_specs=pl.BlockSpec((1, tm, d), lambda b, j: (b, j, 0)),
        out_shape=jax.ShapeDtypeStruct((nb, t, d), F32),
        compiler_params=_params(("parallel", "parallel"), VMEM_LIMIT),
        name="merge",
    )(a_lat, b_lat, ret, ret, z, z, z, z, xs, mod, wb_b, wo_b)


def _first_index(mask, idx, big):
    return jnp.min(jnp.where(mask, idx, big), axis=0, keepdims=True)


def _router_kernel(x_ref, mod_ref, g_ref, rwt_ref, rb_ref, h_ref, ei_ref, wk_ref):
    h = _modulated_norm(x_ref[0], g_ref[...], mod_ref[0, 3:4, :], mod_ref[0, 4:5, :])
    h_ref[0] = h
    tm = h.shape[0]
    logits = _mm_nt(rwt_ref[...], h)
    scores = jax.nn.sigmoid(logits)
    biased = scores + rb_ref[:, 0:1]
    gsz = N_EXPERTS // N_GROUPS
    sub = lax.broadcasted_iota(jnp.int32, (gsz, tm), 0)
    gscore = []
    for g in range(N_GROUPS):
        blk = biased[g * gsz:(g + 1) * gsz]
        m1 = jnp.max(blk, axis=0, keepdims=True)
        i1 = _first_index(blk == m1, sub, gsz)
        m2 = jnp.max(jnp.where(sub == i1, -jnp.inf, blk), axis=0, keepdims=True)
        gscore.append(m1 + m2)
    gs = jnp.concatenate(gscore, axis=0)
    gidx = lax.broadcasted_iota(jnp.int32, (N_GROUPS, tm), 0)
    gsel = jnp.zeros((N_GROUPS, tm), jnp.int32)
    for _ in range(TOPK_GROUPS):
        m = jnp.max(gs, axis=0, keepdims=True)
        pick = gidx == _first_index(gs == m, gidx, N_GROUPS)
        gsel = jnp.where(pick, 1, gsel)
        gs = jnp.where(pick, -jnp.inf, gs)
    cand = jnp.concatenate(
        [jnp.where(gsel[g:g + 1] > 0, biased[g * gsz:(g + 1) * gsz], -jnp.inf) for g in range(N_GROUPS)],
        axis=0)
    eidx = lax.broadcasted_iota(jnp.int32, (N_EXPERTS, tm), 0)
    picks, ws = [], []
    for _ in range(TOP_K):
        m = jnp.max(cand, axis=0, keepdims=True)
        first = _first_index(cand == m, eidx, N_EXPERTS)
        pick = eidx == first
        picks.append(first)
        ws.append(jnp.sum(jnp.where(pick, scores, 0.0), axis=0, keepdims=True))
        cand = jnp.where(pick, -jnp.inf, cand)
    w = jnp.concatenate(ws, axis=0)
    ei_ref[0] = jnp.concatenate(picks, axis=0)
    wk_ref[0] = w / jnp.sum(w, axis=0, keepdims=True) * ROUTED_SCALE


def _router(xs, mod, g, rw_t, rb, tm):
    nb, t, d = xs.shape
    return pl.pallas_call(
        _router_kernel,
        grid=(nb, t // tm),
        in_specs=[
            pl.BlockSpec((1, tm, d), lambda b, j: (b, j, 0)),
            pl.BlockSpec((1, 6, d), _mod_row(nb)),
            pl.BlockSpec((1, d), lambda b, j: (0, 0)),
            pl.BlockSpec((N_EXPERTS, d), lambda b, j: (0, 0)),
            pl.BlockSpec((N_EXPERTS, 1), lambda b, j: (0, 0)),
        ],
        out_specs=[
            pl.BlockSpec((1, tm, d), lambda b, j: (b, j, 0)),
            pl.BlockSpec((1, TOP_K, tm), lambda b, j: (b, 0, j)),
            pl.BlockSpec((1, TOP_K, tm), lambda b, j: (b, 0, j)),
        ],
        out_shape=[jax.ShapeDtypeStruct((nb, t, d), F32), jax.ShapeDtypeStruct((nb, TOP_K, t), jnp.int32),
                   jax.ShapeDtypeStruct((nb, TOP_K, t), F32)],
        compiler_params=_params(("parallel", "parallel")),
        name="router",
    )(xs, mod, g.reshape(1, d), rw_t, rb.reshape(N_EXPERTS, 1))


ROW_SLAB = D_MODEL // 128
MOE_TILE = 384
MOE_GROUP = 8


MOE_LIST_PAD = 1024


def _dispatch(eidx, wk):
    nb, k, t = eidx.shape
    pos = lax.broadcasted_iota(jnp.int32, eidx.shape, 2)
    skey, sw = lax.sort(((eidx * t + pos).reshape(nb, k * t), wk.reshape(nb, k * t)), dimension=1, num_keys=1)
    slab_row = (skey % t) * ROW_SLAB
    cnt = jnp.sum(eidx[:, None] == jnp.arange(N_EXPERTS, dtype=jnp.int32)[None, :, None, None],
                  axis=(2, 3), dtype=jnp.int32)
    off = jnp.cumsum(cnt, axis=1) - cnt
    pad = ((0, 0), (0, MOE_LIST_PAD))
    return jnp.pad(slab_row, pad), jnp.pad(sw, pad), cnt, off


def _experts_kernel(cnt_ref, off_ref, idx_ref, w_ref, src_ref, wg_ref, wu_ref, wd_ref, acc_ref,
                    stage_ref, ostage_ref):
    ne = pl.num_programs(1)
    e = pl.program_id(1)
    count = cnt_ref[pl.program_id(0) * ne + e]
    start = off_ref[pl.program_id(0) * ne + e]

    @pl.when(e == 0)
    def _():
        acc_ref[...] = jnp.zeros_like(acc_ref)
        stage_ref[...] = jnp.zeros_like(stage_ref)

    def slab(row):
        return pl.ds(pl.multiple_of(row, ROW_SLAB), ROW_SLAB)

    def tile(ti, carry):
        base = start + ti * MOE_TILE
        rows = jnp.minimum(count - ti * MOE_TILE, MOE_TILE)
        nfull = rows // MOE_GROUP

        def gather_row(r):
            stage_ref[slab(r * ROW_SLAB), :] = src_ref[0, slab(idx_ref[0, 0, base + r]), :]

        def gather(g, c):
            for u in range(MOE_GROUP):
                gather_row(g * MOE_GROUP + u)
            return c

        def gather_tail(r, c):
            gather_row(r)
            return c

        lax.fori_loop(0, nfull, gather, 0)
        lax.fori_loop(nfull * MOE_GROUP, rows, gather_tail, 0)
        x = jnp.concatenate([stage_ref[pl.ds(j, MOE_TILE, stride=ROW_SLAB), :] for j in range(ROW_SLAB)],
                            axis=-1).astype(BF16)
        act = _silu(_mm(x, wg_ref[0])) * _mm(x, wu_ref[0])
        win = MOE_TILE + 128
        aligned = pl.multiple_of((base // 128) * 128, 128)
        wrow = pltpu.roll(w_ref[0, :, pl.ds(aligned, win)], (win - (base - aligned)) % win, axis=1)
        wcol = jnp.concatenate(
            [jnp.transpose(jnp.broadcast_to(wrow[:, k * 128:(k + 1) * 128], (128, 128)))
             for k in range(MOE_TILE // 128)], axis=0)
        act = act * jnp.concatenate([wcol] * (EXPERT_W // 128), axis=-1)
        y = _mm(act.astype(BF16), wd_ref[0])
        for j in range(ROW_SLAB):
            ostage_ref[pl.ds(j, MOE_TILE, stride=ROW_SLAB), :] = y[:, j * 128:(j + 1) * 128]

        def scatter(g, c):
            rws = [g * MOE_GROUP + u for u in range(MOE_GROUP)]
            dst = [slab(idx_ref[0, 0, base + r]) for r in rws]
            vals = [acc_ref[0, d, :] + ostage_ref[slab(r * ROW_SLAB), :] for d, r in zip(dst, rws)]
            for d, v in zip(dst, vals):
                acc_ref[0, d, :] = v
            return c

        def scatter_tail(r, c):
            d = slab(idx_ref[0, 0, base + r])
            acc_ref[0, d, :] = acc_ref[0, d, :] + ostage_ref[slab(r * ROW_SLAB), :]
            return c

        lax.fori_loop(0, nfull, scatter, 0)
        lax.fori_loop(nfull * MOE_GROUP, rows, scatter_tail, 0)
        return carry

    lax.fori_loop(0, (count + MOE_TILE - 1) // MOE_TILE, tile, 0)


def _experts(h2, slab_row, sw, cnt, off, wg, wu, wd):
    nb, t, d = h2.shape
    ne, _, ew = wg.shape
    n_list = slab_row.shape[1]
    assert d == D_MODEL and ew == EXPERT_W and MOE_TILE % 128 == 0 and MOE_LIST_PAD >= MOE_TILE + 128
    grid_spec = pltpu.PrefetchScalarGridSpec(
        num_scalar_prefetch=2,
        grid=(nb, ne),
        in_specs=[
            pl.BlockSpec((1, 1, n_list), lambda b, e, c, o: (b, 0, 0), memory_space=pltpu.SMEM),
            pl.BlockSpec((1, 1, n_list), lambda b, e, c, o: (b, 0, 0)),
            pl.BlockSpec((1, t * ROW_SLAB, 128), lambda b, e, c, o: (b, 0, 0)),
            pl.BlockSpec((1, d, ew), lambda b, e, c, o: (e, 0, 0)),
            pl.BlockSpec((1, d, ew), lambda b, e, c, o: (e, 0, 0)),
            pl.BlockSpec((1, ew, d), lambda b, e, c, o: (e, 0, 0)),
        ],
        out_specs=pl.BlockSpec((1, t * ROW_SLAB, 128), lambda b, e, c, o: (b, 0, 0)),
        scratch_shapes=[pltpu.VMEM((MOE_TILE * ROW_SLAB, 128), F32),
                        pltpu.VMEM((MOE_TILE * ROW_SLAB, 128), F32)],
    )
    acc = pl.pallas_call(
        _experts_kernel,
        grid_spec=grid_spec,
        out_shape=jax.ShapeDtypeStruct((nb, t * ROW_SLAB, 128), F32),
        compiler_params=_params(("parallel", "arbitrary"), VMEM_LIMIT),
        name="experts",
    )(cnt.reshape(nb * ne), off.reshape(nb * ne), slab_row.reshape(nb, 1, n_list), sw.reshape(nb, 1, n_list),
      h2.reshape(nb, t * ROW_SLAB, 128), wg, wu, wd)
    return acc.reshape(nb, t, d)


def _moe_out_kernel(h_ref, r_ref, x_ref, mod_ref, sg_ref, su_ref, sd_ref, o_ref):
    h = h_ref[0].astype(BF16)
    shared = _mm((_silu(_mm(h, sg_ref[...])) * _mm(h, su_ref[...])).astype(BF16), sd_ref[...])
    o_ref[0] = x_ref[0] + mod_ref[0, 5:6, :] * (r_ref[0] + shared)


def _moe_out(h2, routed, xs, mod, sg, su, sd, tm):
    nb, t, d = xs.shape
    return pl.pallas_call(
        _moe_out_kernel,
        grid=(nb, t // tm),
        in_specs=[
            pl.BlockSpec((1, tm, d), lambda b, j: (b, j, 0)),
            pl.BlockSpec((1, tm, d), lambda b, j: (b, j, 0)),
            pl.BlockSpec((1, tm, d), lambda b, j: (b, j, 0)),
            pl.BlockSpec((1, 6, d), _mod_row(nb)),
            pl.BlockSpec(sg.shape, lambda b, j: (0, 0)),
            pl.BlockSpec(su.shape, lambda b, j: (0, 0)),
            pl.BlockSpec(sd.shape, lambda b, j: (0, 0)),
        ],
        out_specs=pl.BlockSpec((1, tm, d), lambda b, j: (b, j, 0)),
        out_shape=jax.ShapeDtypeStruct((nb, t, d), F32),
        compiler_params=_params(("parallel", "parallel")),
        name="moe_out",
    )(h2, routed, xs, mod, sg, su, sd)


def _final_kernel(x_ref, g_ref, o_ref):
    x = x_ref[0]
    o_ref[0] = x * lax.rsqrt(jnp.mean(x * x, axis=-1, keepdims=True) + EPS) * g_ref[...]


def _final_norm(xs, g, lc, tm):
    nb, t, d = xs.shape
    l = t - lc
    return pl.pallas_call(
        _final_kernel,
        grid=(nb, l // tm),
        in_specs=[
            pl.BlockSpec((1, tm, d), lambda b, j: (b, j + lc // tm, 0)),
            pl.BlockSpec((1, d), lambda b, j: (0, 0)),
        ],
        out_specs=pl.BlockSpec((1, tm, d), lambda b, j: (b, j, 0)),
        out_shape=jax.ShapeDtypeStruct((nb, l, d), F32),
        compiler_params=_params(("parallel", "parallel")),
        name="final_norm",
    )(xs, g.reshape(1, d))


def kernel(x, c, ctx, c_ctx, w_ada, b_ada, norm_mix_g, norm_ffn_g, w_in, cm_ln_g, cm_ln_b, cm_w_s, cm_b_s,
           na_rpb, ret_decay_logit, w_branch, w_out, router_w, router_bias, exp_w_gate, exp_w_up,
           exp_w_down, sh_w_gate, sh_w_up, sh_w_down, final_g):
    nb, l, d = x.shape
    lc = ctx.shape[1]
    depth = w_ada.shape[0]
    assert d == D_MODEL and nb < MOD_ROWS and l % GRID_W == 0 and lc % CHUNK == 0 and l % CHUNK == 0
    tm = 256
    assert lc % tm == 0 and l % tm == 0

    cc = jnp.zeros((MOD_ROWS, d), F32).at[:nb].set(c).at[nb].set(c_ctx)
    mod_all = _adaln(cc, w_ada, b_ada).reshape(depth, MOD_ROWS, 6, d)
    cos_t, sin_t = _rope_tables(lc, l)
    xs = jnp.concatenate([ctx, x], axis=1)

    for layer in range(depth):
        mod = mod_all[layer]
        z = _inproj(xs, mod, norm_mix_g[layer], w_in[layer].astype(BF16), tm)
        a_lat = _gmlp(z, cm_ln_g[layer], cm_ln_b[layer], cm_w_s[layer].astype(BF16),
                      jnp.transpose(cm_b_s[layer]))
        b_lat = _nattn(z, _na_bias_table(na_rpb[layer], l // GRID_W), lc)
        ret = _retention(z, ret_decay_logit[layer], cos_t, sin_t, lc)
        xs = _merge(a_lat, b_lat, ret, z, xs, mod, w_branch[layer].astype(BF16),
                    w_out[layer].astype(BF16), tm)
        h2, eidx, wk = _router(xs, mod, norm_ffn_g[layer], jnp.transpose(router_w[layer]),
                               router_bias[layer], tm)
        slab_row, sw, cnt, off = _dispatch(eidx, wk)
        routed = _experts(h2, slab_row, sw, cnt, off, exp_w_gate[layer].astype(BF16),
                          exp_w_up[layer].astype(BF16), exp_w_down[layer].astype(BF16))
        xs = _moe_out(h2, routed, xs, mod, sh_w_gate[layer].astype(BF16), sh_w_up[layer].astype(BF16),
                      sh_w_down[layer].astype(BF16), tm)
    return _final_norm(xs, final_g, lc, tm)
```

```python
import functools

import jax
import jax.numpy as jnp
import numpy as np
from jax import lax
from jax.experimental import pallas as pl
from jax.experimental.pallas import tpu as pltpu

F32 = jnp.float32
BF16 = jnp.bfloat16

D_MODEL = 1024
GRID_W = 64
CHUNK = 128
CM_GROUPS = 4
CM_WIDTH = 512
NA_HEADS = 8
NA_HEAD_DIM = 64
NA_WIDTH = NA_HEADS * NA_HEAD_DIM
WIN_R = 8
WIN_C = 16
RET_HEADS = 4
RET_DK = 64
RET_DV = 128
RET_QK_W = RET_HEADS * RET_DK
RET_V_W = RET_HEADS * RET_DV
ROPE_BASE = 10000.0
IN_W = 2 * CM_WIDTH + 3 * NA_WIDTH + 2 * RET_QK_W + 2 * RET_V_W + 3 * D_MODEL
N_EXPERTS = 64
TOP_K = 8
N_GROUPS = 8
TOPK_GROUPS = 4
EXPERT_W = 256
ROUTED_SCALE = 2.5
EPS = 1e-6
NEG = -1e30

OFF_AU, OFF_AV = 0, 512
OFF_NQ, OFF_NK, OFF_NV = 1024, 1536, 2048
OFF_RQ, OFF_RK, OFF_RV, OFF_RG = 2560, 2816, 3072, 3584
OFF_GA, OFF_GB, OFF_GC = 4096, 5120, 6144

MOD_ROWS = 16
VMEM_LIMIT = 56 * 1024 * 1024


def _params(sem, vmem=None):
    return pltpu.CompilerParams(dimension_semantics=sem, vmem_limit_bytes=vmem)


def _silu(v):
    return v * jax.nn.sigmoid(v)


def _mm(a, b):
    return jnp.dot(a, b, preferred_element_type=F32)


def _mm_nt(a, b):
    return lax.dot_general(a, b, (((1,), (1,)), ((), ())), preferred_element_type=F32)


def _mm_tn(a, b):
    return lax.dot_general(a, b, (((0,), (0,)), ((), ())), preferred_element_type=F32)


def _mod_row(nb):
    return lambda b, j, *_: (jnp.where(j == 0, nb, b), 0, 0)


def _adaln_kernel(c_ref, w_ref, b_ref, o_ref):
    s = _silu(c_ref[...])
    o_ref[0] = _mm(s, w_ref[0]) + b_ref[0]


def _adaln(cc, w_ada, b_ada):
    depth, d, n6 = w_ada.shape
    tn = 1536
    return pl.pallas_call(
        _adaln_kernel,
        grid=(depth, n6 // tn),
        in_specs=[
            pl.BlockSpec((MOD_ROWS, d), lambda l, n: (0, 0)),
            pl.BlockSpec((1, d, tn), lambda l, n: (l, 0, n)),
            pl.BlockSpec((1, 1, tn), lambda l, n: (l, 0, n)),
        ],
        out_specs=pl.BlockSpec((1, MOD_ROWS, tn), lambda l, n: (l, 0, n)),
        out_shape=jax.ShapeDtypeStruct((depth, MOD_ROWS, n6), F32),
        compiler_params=_params(("parallel", "parallel"), VMEM_LIMIT),
        name="adaln",
    )(cc, w_ada, b_ada.reshape(depth, 1, n6))


def _modulated_norm(x, g, shift, scale):
    y = x * lax.rsqrt(jnp.mean(x * x, axis=-1, keepdims=True) + EPS) * g
    return y * (1.0 + scale) + shift


def _inproj_kernel(x_ref, mod_ref, g_ref, w_ref, z_ref):
    h = _modulated_norm(x_ref[0], g_ref[...], mod_ref[0, 0:1, :], mod_ref[0, 1:2, :])
    hb = h.astype(BF16)
    nw = 512
    for n in range(IN_W // nw):
        z_ref[0, :, n * nw:(n + 1) * nw] = _mm(hb, w_ref[:, n * nw:(n + 1) * nw]).astype(BF16)


def _inproj(xs, mod, g, w_in_b, tm):
    nb, t, d = xs.shape
    return pl.pallas_call(
        _inproj_kernel,
        grid=(nb, t // tm),
        in_specs=[
            pl.BlockSpec((1, tm, d), lambda b, j: (b, j, 0)),
            pl.BlockSpec((1, 6, d), _mod_row(nb)),
            pl.BlockSpec((1, d), lambda b, j: (0, 0)),
            pl.BlockSpec((d, IN_W), lambda b, j: (0, 0)),
        ],
        out_specs=pl.BlockSpec((1, tm, IN_W), lambda b, j: (b, j, 0)),
        out_shape=jax.ShapeDtypeStruct((nb, t, IN_W), BF16),
        compiler_params=_params(("parallel", "parallel"), VMEM_LIMIT),
        name="inproj",
    )(xs, mod, g.reshape(1, d), w_in_b)


def _gmlp_kernel(u_ref, v_ref, lng_ref, lnb_ref, ws_ref, bs_ref, o_ref):
    u = jax.nn.gelu(u_ref[0].astype(F32))
    v = jax.nn.gelu(v_ref[0].astype(F32))
    vc = v - jnp.mean(v, axis=-1, keepdims=True)
    vn = vc * lax.rsqrt(jnp.mean(vc * vc, axis=-1, keepdims=True) + EPS) * lng_ref[...] + lnb_ref[...]
    vb = vn.astype(BF16)
    gw = CM_WIDTH // CM_GROUPS
    for g in range(CM_GROUPS):
        s = _mm(ws_ref[g], vb[:, g * gw:(g + 1) * gw]) + bs_ref[:, g:g + 1]
        o_ref[0, :, g * gw:(g + 1) * gw] = (u[:, g * gw:(g + 1) * gw] * s).astype(BF16)


def _gmlp(z, ln_g, ln_b, ws_b, bs_t):
    nb, t, _ = z.shape
    return pl.pallas_call(
        _gmlp_kernel,
        grid=(nb, t // CHUNK),
        in_specs=[
            pl.BlockSpec((1, CHUNK, CM_WIDTH), lambda b, j: (b, j, OFF_AU // CM_WIDTH)),
            pl.BlockSpec((1, CHUNK, CM_WIDTH), lambda b, j: (b, j, OFF_AV // CM_WIDTH)),
            pl.BlockSpec((1, CM_WIDTH), lambda b, j: (0, 0)),
            pl.BlockSpec((1, CM_WIDTH), lambda b, j: (0, 0)),
            pl.BlockSpec((CM_GROUPS, CHUNK, CHUNK), lambda b, j: (0, 0, 0)),
            pl.BlockSpec((CHUNK, CM_GROUPS), lambda b, j: (0, 0)),
        ],
        out_specs=pl.BlockSpec((1, CHUNK, CM_WIDTH), lambda b, j: (b, j, 0)),
        out_shape=jax.ShapeDtypeStruct((nb, t, CM_WIDTH), BF16),
        compiler_params=_params(("parallel", "parallel")),
        name="gmlp",
    )(z, z, ln_g.reshape(1, -1), ln_b.reshape(1, -1), ws_b, bs_t)


def _na_row_classes(rows):
    half = WIN_R // 2
    wr = min(WIN_R, rows)
    reps = list(range(half)) + [half] + list(range(rows - (wr - half) + 1, rows))
    cls = np.zeros((rows,), np.int32)
    for r in range(rows):
        if r < half:
            cls[r] = r
        elif r <= rows - (wr - half):
            cls[r] = half
        else:
            cls[r] = half + 1 + (r - (rows - (wr - half) + 1))
    return reps, cls


def _na_bias_table(rpb, rows):
    wr = min(WIN_R, rows)
    reps, _ = _na_row_classes(rows)
    nh = rpb.shape[0]
    qc = np.arange(GRID_W)[:, None]
    kc = np.arange(GRID_W)[None, :]
    col_start = np.clip(qc - WIN_C // 2, 0, GRID_W - WIN_C)
    valid = (kc >= col_start) & (kc < col_start + WIN_C)
    pad = GRID_W - WIN_C
    edge = jnp.concatenate([jnp.repeat(rpb[..., :1], pad, axis=-1), rpb.astype(F32),
                            jnp.repeat(rpb[..., -1:], pad, axis=-1)], axis=-1)
    toep = jnp.stack([edge[..., GRID_W - 1 - q:2 * GRID_W - 1 - q] for q in range(GRID_W)], axis=1)
    toep = jnp.where(valid[None, :, None, :], toep, NEG)
    tabs = []
    for r in reps:
        r0 = int(np.clip(r - WIN_R // 2, 0, rows - wr))
        lo = r0 - r + WIN_R - 1
        tabs.append(toep[:, :, lo:lo + wr, :].reshape(nh, GRID_W, wr * GRID_W))
    tab = jnp.stack(tabs)
    tab = tab.reshape(len(reps), nh // 2, 2, GRID_W, wr * GRID_W)
    return jnp.transpose(tab, (0, 1, 4, 2, 3)).reshape(len(reps), nh // 2, wr * GRID_W, 2 * GRID_W)


def _t_blocks(p):
    return jnp.concatenate([jnp.transpose(p[i:i + 128]) for i in range(0, p.shape[0], 128)],
                           axis=1).astype(BF16)


NA_ROWS = 2


def _na_kernel(cls_ref, q_ref, k_ref, v_ref, *refs, lc, rows):
    del cls_ref
    j = pl.program_id(1)
    n_ctx_blk = lc // GRID_W
    wr = min(WIN_R, rows)
    sub = lax.broadcasted_iota(jnp.int32, (128, 128), 0)
    lane = lax.broadcasted_iota(jnp.int32, (128, 128), 1)
    same_head = (sub // 64) == (lane // 64)
    first_head = lax.broadcasted_iota(jnp.int32, (GRID_W, 128), 1) < 64
    scale = NA_HEAD_DIM ** -0.5
    bias_refs, o_ref = refs[:-1], refs[-1]

    def attend(local, rr):
        qs = slice(rr * GRID_W, (rr + 1) * GRID_W)
        bias_ref = bias_refs[rr]
        q = q_ref[0, qs, :]
        if local:
            r0 = jnp.clip(j * NA_ROWS + rr - n_ctx_blk - WIN_R // 2, 0, rows - wr)
            start = pl.multiple_of(lc + r0 * GRID_W, GRID_W)
            k_loc = k_ref[0, pl.ds(start, wr * GRID_W), :]
            v_loc = v_ref[0, pl.ds(start, wr * GRID_W), :]
        k_ctx = k_ref[0, 0:lc, :]
        v_ctx = v_ref[0, 0:lc, :]
        for p in range(NA_HEADS // 2):
            cs = slice(p * 128, (p + 1) * 128)
            q2 = q[:, cs].astype(F32) * scale
            qbd = jnp.transpose(jnp.where(same_head, jnp.concatenate([q2, q2], axis=0), 0.0)).astype(BF16)
            s_ctx = _mm(k_ctx[:, cs], qbd)
            m = jnp.max(s_ctx, axis=0, keepdims=True)
            if local:
                s_loc = _mm(k_loc[:, cs], qbd) + bias_ref[0, p]
                m = jnp.maximum(m, jnp.max(s_loc, axis=0, keepdims=True))
            p_ctx = jnp.exp(s_ctx - m)
            den = jnp.sum(p_ctx, axis=0, keepdims=True)
            if local:
                p_loc = jnp.exp(s_loc - m)
                den = den + jnp.sum(p_loc, axis=0, keepdims=True)
            inv = 1.0 / den
            o = _mm(_t_blocks(p_ctx * inv), v_ctx[:, cs])
            if local:
                o = o + _mm(_t_blocks(p_loc * inv), v_loc[:, cs])
            o_ref[0, qs, cs] = jnp.where(first_head, o[:GRID_W], o[GRID_W:]).astype(BF16)

    @pl.when(j * NA_ROWS < n_ctx_blk)
    def _():
        for rr in range(NA_ROWS):
            attend(False, rr)

    @pl.when(j * NA_ROWS >= n_ctx_blk)
    def _():
        for rr in range(NA_ROWS):
            attend(True, rr)


def _nattn(z, bias_tab, lc):
    nb, t, _ = z.shape
    rows = (t - lc) // GRID_W
    n_ctx_blk = lc // GRID_W
    _, cls = _na_row_classes(rows)
    cls_arr = jnp.asarray(np.concatenate([np.zeros((n_ctx_blk,), np.int32), cls]))
    wr = min(WIN_R, rows)
    qb = NA_ROWS * GRID_W
    assert n_ctx_blk % NA_ROWS == 0 and rows % NA_ROWS == 0

    def bias_spec(rr):
        return pl.BlockSpec((1, NA_HEADS // 2, wr * GRID_W, 2 * GRID_W),
                            lambda b, j, c: (c[j * NA_ROWS + rr], 0, 0, 0))

    grid_spec = pltpu.PrefetchScalarGridSpec(
        num_scalar_prefetch=1,
        grid=(nb, t // qb),
        in_specs=[
            pl.BlockSpec((1, qb, NA_WIDTH), lambda b, j, c: (b, j, OFF_NQ // NA_WIDTH)),
            pl.BlockSpec((1, t, NA_WIDTH), lambda b, j, c: (b, 0, OFF_NK // NA_WIDTH)),
            pl.BlockSpec((1, t, NA_WIDTH), lambda b, j, c: (b, 0, OFF_NV // NA_WIDTH)),
        ] + [bias_spec(rr) for rr in range(NA_ROWS)],
        out_specs=pl.BlockSpec((1, qb, NA_WIDTH), lambda b, j, c: (b, j, 0)),
    )
    return pl.pallas_call(
        functools.partial(_na_kernel, lc=lc, rows=rows),
        grid_spec=grid_spec,
        out_shape=jax.ShapeDtypeStruct((nb, t, NA_WIDTH), BF16),
        compiler_params=_params(("parallel", "arbitrary"), VMEM_LIMIT),
        name="nattn",
    )(cls_arr, z, z, z, *([bias_tab] * NA_ROWS))


def _rope_tables(lc, l):
    half = RET_DK // 2
    nf = half // 2
    tpos = np.arange(l)
    inv = ROPE_BASE ** (-jnp.arange(nf, dtype=F32) / nf)

    def part(pos):
        ang = jnp.asarray(pos, F32)[:, None] * inv[None, :]
        c, s = jnp.cos(ang), jnp.sin(ang)
        return jnp.concatenate([c, c], -1), jnp.concatenate([-s, s], -1)

    c_r, s_r = part(tpos // GRID_W)
    c_c, s_c = part(tpos % GRID_W)
    cos = jnp.concatenate([c_r, c_c], -1)
    sin = jnp.concatenate([s_r, s_c], -1)
    cos = jnp.concatenate([jnp.ones((lc, RET_DK), F32), cos], 0)
    sin = jnp.concatenate([jnp.zeros((lc, RET_DK), F32), sin], 0)
    return jnp.tile(cos, (1, RET_HEADS)), jnp.tile(sin, (1, RET_HEADS))


def _ret_kernel(dl_ref, *refs):
    st_ref = refs[-1]

    @pl.when(pl.program_id(1) == 0)
    def _():
        st_ref[...] = jnp.zeros_like(st_ref)

    for d in range(2):
        _ret_direction(d, dl_ref, *refs[5 * d:5 * d + 5], refs[10 + d], st_ref)


def _ret_direction(d, dl_ref, q_ref, k_ref, v_ref, cos_ref, sin_ref, o_ref, st_ref):
    cos = cos_ref[...]
    sin = sin_ref[...]
    lane_qk = lax.broadcasted_iota(jnp.int32, (CHUNK, RET_QK_W), 1)
    first = (lane_qk % (RET_DK // 2)) < (RET_DK // 4)

    def rope(a):
        a = a.astype(F32)
        partner = jnp.where(first, pltpu.roll(a, RET_QK_W - RET_DK // 4, axis=1),
                            pltpu.roll(a, RET_DK // 4, axis=1))
        return a * cos + partner * sin

    q = rope(q_ref[0])
    k = rope(k_ref[0]) * (RET_DK ** -0.5)
    v = v_ref[0]

    ri = lax.broadcasted_iota(jnp.int32, (CHUNK, CHUNK), 0)
    ci = lax.broadcasted_iota(jnp.int32, (CHUNK, CHUNK), 1)
    dist = (ri - ci) if d == 0 else (ci - ri)
    keep = dist >= d
    dist_f = jnp.maximum(dist, 0).astype(F32)
    q_exp = ((ri + 1) if d == 0 else (CHUNK - ri)).astype(F32)
    k_exp = ((CHUNK - 1 - ri) if d == 0 else ri).astype(F32)
    lane_half = ci >= RET_DK
    row_half = ri >= RET_DK

    def log_gamma(h):
        xv = jnp.full((CHUNK, CHUNK), dl_ref[d, h], F32)
        return jnp.minimum(xv, 0.0) - jnp.log1p(jnp.exp(-jnp.abs(xv)))

    for p in range(RET_HEADS // 2):
        cs = slice(p * 128, (p + 1) * 128)
        lg = [log_gamma(2 * p), log_gamma(2 * p + 1)]
        q2 = q[:, cs]
        k2 = k[:, cs]
        k2b = k2.astype(BF16)
        kd_t = jnp.transpose(k2 * jnp.exp(k_exp * jnp.where(lane_half, lg[1], lg[0])))
        s2 = st_ref[d, p]
        s2b = s2.astype(BF16)
        upd = s2 * jnp.exp(float(CHUNK) * jnp.where(row_half, lg[1], lg[0]))
        for hh in range(2):
            h = 2 * p + hh
            vh = v[:, h * RET_DV:(h + 1) * RET_DV]
            sel = lane_half if hh else jnp.logical_not(lane_half)
            qm = jnp.where(sel, q2, 0.0).astype(BF16)
            a = _mm_nt(qm, k2b)
            a = jnp.where(keep, a * jnp.exp(dist_f * lg[hh]), 0.0)
            inner = _mm(a.astype(BF16), vh)
            cross = _mm(qm, s2b) * jnp.exp(q_exp * lg[hh])
            o_ref[0, :, h * RET_DV:(h + 1) * RET_DV] = (inner + cross).astype(o_ref.dtype)
            rsel = row_half if hh else jnp.logical_not(row_half)
            upd = upd + _mm(jnp.where(rsel, kd_t, 0.0).astype(BF16), vh)
        st_ref[d, p] = upd


def _retention(z, decay_logit, cos_t, sin_t, lc):
    nb, t, _ = z.shape
    nc = t // CHUNK
    ncc = lc // CHUNK

    def chunk(d, s):
        return s if d == 0 else jnp.where(s < ncc, ncc - 1 - s, nc - 1 - (s - ncc))

    def in_specs(d):
        return [
            pl.BlockSpec((1, CHUNK, RET_QK_W), lambda b, s, dl: (b, chunk(d, s), OFF_RQ // RET_QK_W)),
            pl.BlockSpec((1, CHUNK, RET_QK_W), lambda b, s, dl: (b, chunk(d, s), OFF_RK // RET_QK_W)),
            pl.BlockSpec((1, CHUNK, RET_V_W), lambda b, s, dl: (b, chunk(d, s), OFF_RV // RET_V_W)),
            pl.BlockSpec((CHUNK, RET_QK_W), lambda b, s, dl: (chunk(d, s), 0)),
            pl.BlockSpec((CHUNK, RET_QK_W), lambda b, s, dl: (chunk(d, s), 0)),
        ]

    def out_spec(d):
        return pl.BlockSpec((1, CHUNK, RET_V_W), lambda b, s, dl: (b, chunk(d, s), 0))

    grid_spec = pltpu.PrefetchScalarGridSpec(
        num_scalar_prefetch=1,
        grid=(nb, nc),
        in_specs=in_specs(0) + in_specs(1),
        out_specs=[out_spec(0), out_spec(1)],
        scratch_shapes=[pltpu.VMEM((2, RET_HEADS // 2, 2 * RET_DK, RET_DV), F32)],
    )
    return pl.pallas_call(
        _ret_kernel,
        grid_spec=grid_spec,
        out_shape=[jax.ShapeDtypeStruct((nb, t, RET_V_W), BF16)] * 2,
        compiler_params=_params(("parallel", "arbitrary")),
        name="retention",
    )(decay_logit, *([z, z, z, cos_t, sin_t] * 2))


def _merge_kernel(a_ref, b_ref, rf_ref, rb_ref, rg_ref, ga_ref, gb_ref, gc_ref, x_ref, mod_ref,
                  wb_ref, wo_ref, o_ref):
    o = rf_ref[0].astype(F32) + rb_ref[0].astype(F32)
    parts = []
    for h in range(RET_HEADS):
        oh = o[:, h * RET_DV:(h + 1) * RET_DV]
        parts.append(oh * lax.rsqrt(jnp.mean(oh * oh, axis=-1, keepdims=True) + EPS))
    c = jnp.concatenate(parts, axis=-1) * _silu(rg_ref[0].astype(F32))
    m = (jax.nn.sigmoid(ga_ref[0].astype(F32)) * _mm(a_ref[0], wb_ref[0])
         + jax.nn.sigmoid(gb_ref[0].astype(F32)) * _mm(b_ref[0], wb_ref[1])
         + jax.nn.sigmoid(gc_ref[0].astype(F32)) * _mm(c.astype(BF16), wb_ref[2]))
    mix = _mm(m.astype(BF16), wo_ref[...])
    o_ref[0] = x_ref[0] + mod_ref[0, 2:3, :] * mix


def _merge(a_lat, b_lat, ret, z, xs, mod, wb_b, wo_b, tm):
    nb, t, d = xs.shape
    bw = CM_WIDTH
    return pl.pallas_call(
        _merge_kernel,
        grid=(nb, t // tm),
        in_specs=[
            pl.BlockSpec((1, tm, bw), lambda b, j: (b, j, 0)),
            pl.BlockSpec((1, tm, bw), lambda b, j: (b, j, 0)),
            pl.BlockSpec((1, tm, bw), lambda b, j: (b, j, 0)),
            pl.BlockSpec((1, tm, bw), lambda b, j: (b, j, 0)),
            pl.BlockSpec((1, tm, RET_V_W), lambda b, j: (b, j, OFF_RG // RET_V_W)),
            pl.BlockSpec((1, tm, d), lambda b, j: (b, j, OFF_GA // d)),
            pl.BlockSpec((1, tm, d), lambda b, j: (b, j, OFF_GB // d)),
            pl.BlockSpec((1, tm, d), lambda b, j: (b, j, OFF_GC // d)),
            pl.BlockSpec((1, tm, d), lambda b, j: (b, j, 0)),
            pl.BlockSpec((1, 6, d), _mod_row(nb)),
            pl.BlockSpec((3, bw, d), lambda b, j: (0, 0, 0)),
            pl.BlockSpec((d, d), lambda b, j: (0, 0)),
        ],
        out_specs=pl.BlockSpec((1, tm, d), lambda b, j: (b, j, 0)),
        out_shape=jax.ShapeDtypeStruct((nb, t, d), F32),
        compiler_params=_params(("parallel", "parallel"), VMEM_LIMIT),
        name="merge",
    )(a_lat, b_lat, ret[0], ret[1], z, z, z, z, xs, mod, wb_b, wo_b)


ROW_SLAB = D_MODEL // 128


def _slab_load(ref, lead, n):
    return jnp.concatenate([ref[lead + (pl.ds(j, n, stride=ROW_SLAB), slice(None))] for j in range(ROW_SLAB)],
                           axis=-1)


def _slab_store(ref, lead, val):
    for j in range(ROW_SLAB):
        ref[lead + (pl.ds(j, val.shape[0], stride=ROW_SLAB), slice(None))] = val[:, j * 128:(j + 1) * 128]


def _first_index(mask, idx, big):
    return jnp.min(jnp.where(mask, idx, big), axis=0, keepdims=True)


def _router_kernel(x_ref, mod_ref, g_ref, rwt_ref, rb_ref, h_ref, ei_ref, wk_ref):
    h = _modulated_norm(x_ref[0], g_ref[...], mod_ref[0, 3:4, :], mod_ref[0, 4:5, :])
    _slab_store(h_ref, (0,), h)
    tm = h.shape[0]
    logits = _mm_nt(rwt_ref[...], h)
    scores = jax.nn.sigmoid(logits)
    biased = scores + rb_ref[:, 0:1]
    gsz = N_EXPERTS // N_GROUPS
    sub = lax.broadcasted_iota(jnp.int32, (gsz, tm), 0)
    gscore = []
    for g in range(N_GROUPS):
        blk = biased[g * gsz:(g + 1) * gsz]
        m1 = jnp.max(blk, axis=0, keepdims=True)
        i1 = _first_index(blk == m1, sub, gsz)
        m2 = jnp.max(jnp.where(sub == i1, -jnp.inf, blk), axis=0, keepdims=True)
        gscore.append(m1 + m2)
    gs = jnp.concatenate(gscore, axis=0)
    gidx = lax.broadcasted_iota(jnp.int32, (N_GROUPS, tm), 0)
    gsel = jnp.zeros((N_GROUPS, tm), jnp.int32)
    for _ in range(TOPK_GROUPS):
        m = jnp.max(gs, axis=0, keepdims=True)
        pick = gidx == _first_index(gs == m, gidx, N_GROUPS)
        gsel = jnp.where(pick, 1, gsel)
        gs = jnp.where(pick, -jnp.inf, gs)
    cand = jnp.concatenate(
        [jnp.where(gsel[g:g + 1] > 0, biased[g * gsz:(g + 1) * gsz], -jnp.inf) for g in range(N_GROUPS)],
        axis=0)
    eidx = lax.broadcasted_iota(jnp.int32, (N_EXPERTS, tm), 0)
    picks, ws = [], []
    for _ in range(TOP_K):
        m = jnp.max(cand, axis=0, keepdims=True)
        first = _first_index(cand == m, eidx, N_EXPERTS)
        pick = eidx == first
        picks.append(first)
        ws.append(jnp.sum(jnp.where(pick, scores, 0.0), axis=0, keepdims=True))
        cand = jnp.where(pick, -jnp.inf, cand)
    w = jnp.concatenate(ws, axis=0)
    ei_ref[0] = jnp.concatenate(picks, axis=0)
    wk_ref[0] = w / jnp.sum(w, axis=0, keepdims=True) * ROUTED_SCALE


def _router(xs, mod, g, rw_t, rb, tm):
    nb, t, d = xs.shape
    return pl.pallas_call(
        _router_kernel,
        grid=(nb, t // tm),
        in_specs=[
            pl.BlockSpec((1, tm, d), lambda b, j: (b, j, 0)),
            pl.BlockSpec((1, 6, d), _mod_row(nb)),
            pl.BlockSpec((1, d), lambda b, j: (0, 0)),
            pl.BlockSpec((N_EXPERTS, d), lambda b, j: (0, 0)),
            pl.BlockSpec((N_EXPERTS, 1), lambda b, j: (0, 0)),
        ],
        out_specs=[
            pl.BlockSpec((1, tm * ROW_SLAB, 128), lambda b, j: (b, j, 0)),
            pl.BlockSpec((1, TOP_K, tm), lambda b, j: (b, 0, j)),
            pl.BlockSpec((1, TOP_K, tm), lambda b, j: (b, 0, j)),
        ],
        out_shape=[jax.ShapeDtypeStruct((nb, t * ROW_SLAB, 128), F32),
                   jax.ShapeDtypeStruct((nb, TOP_K, t), jnp.int32), jax.ShapeDtypeStruct((nb, TOP_K, t), F32)],
        compiler_params=_params(("parallel", "parallel")),
        name="router",
    )(xs, mod, g.reshape(1, d), rw_t, rb.reshape(N_EXPERTS, 1))


MOE_TILE = 320
MOE_GROUP = 8
MOE_WIN = (MOE_TILE + 127) // 128 * 128 + 128
MOE_LIST_PAD = 1024


def _dispatch(eidx, wk):
    nb, k, t = eidx.shape
    pos = lax.broadcasted_iota(jnp.int32, eidx.shape, 2)
    skey, sw = lax.sort(((eidx * t + pos).reshape(nb, k * t), wk.reshape(nb, k * t)), dimension=1, num_keys=1)
    slab_row = (skey % t) * ROW_SLAB
    cnt = jnp.sum(eidx[:, None] == jnp.arange(N_EXPERTS, dtype=jnp.int32)[None, :, None, None],
                  axis=(2, 3), dtype=jnp.int32)
    off = jnp.cumsum(cnt, axis=1) - cnt
    pad = ((0, 0), (0, MOE_LIST_PAD))
    return jnp.pad(slab_row, pad), jnp.pad(sw, pad), cnt, off


def _experts_kernel(cnt_ref, off_ref, idx_ref, w_ref, src_ref, wg_ref, wu_ref, wd_ref, acc_ref,
                    stage_ref, ostage_ref):
    ne = pl.num_programs(1)
    e = pl.program_id(1)
    count = cnt_ref[pl.program_id(0) * ne + e]
    start = off_ref[pl.program_id(0) * ne + e]

    @pl.when(e == 0)
    def _():
        acc_ref[...] = jnp.zeros_like(acc_ref)
        stage_ref[...] = jnp.zeros_like(stage_ref)

    def slab(row):
        return pl.ds(pl.multiple_of(row, ROW_SLAB), ROW_SLAB)

    def tile(ti, carry):
        base = start + ti * MOE_TILE
        rows = jnp.minimum(count - ti * MOE_TILE, MOE_TILE)
        nfull = rows // MOE_GROUP

        def gather_row(r):
            stage_ref[slab(r * ROW_SLAB), :] = src_ref[0, slab(idx_ref[0, 0, base + r]), :]

        def gather(g, c):
            for u in range(MOE_GROUP):
                gather_row(g * MOE_GROUP + u)
            return c

        def gather_tail(r, c):
            gather_row(r)
            return c

        lax.fori_loop(0, nfull, gather, 0)
        lax.fori_loop(nfull * MOE_GROUP, rows, gather_tail, 0)
        x = _slab_load(stage_ref, (), MOE_TILE).astype(BF16)
        act = _silu(_mm(x, wg_ref[0])) * _mm(x, wu_ref[0])
        aligned = pl.multiple_of((base // 128) * 128, 128)
        wrow = pltpu.roll(w_ref[0, :, pl.ds(aligned, MOE_WIN)], (MOE_WIN - (base - aligned)) % MOE_WIN, axis=1)
        wcol = jnp.concatenate(
            [jnp.transpose(jnp.broadcast_to(wrow[:, k * 128:(k + 1) * 128], (128, 128)))
             for k in range(MOE_WIN // 128 - 1)], axis=0)[:MOE_TILE]
        act = act * jnp.concatenate([wcol] * (EXPERT_W // 128), axis=-1)
        _slab_store(ostage_ref, (), _mm(act.astype(BF16), wd_ref[0]))

        def scatter(g, c):
            rws = [g * MOE_GROUP + u for u in range(MOE_GROUP)]
            dst = [slab(idx_ref[0, 0, base + r]) for r in rws]
            vals = [acc_ref[0, d, :] + ostage_ref[slab(r * ROW_SLAB), :] for d, r in zip(dst, rws)]
            for d, v in zip(dst, vals):
                acc_ref[0, d, :] = v
            return c

        def scatter_tail(r, c):
            d = slab(idx_ref[0, 0, base + r])
            acc_ref[0, d, :] = acc_ref[0, d, :] + ostage_ref[slab(r * ROW_SLAB), :]
            return c

        lax.fori_loop(0, nfull, scatter, 0)
        lax.fori_loop(nfull * MOE_GROUP, rows, scatter_tail, 0)
        return carry

    lax.fori_loop(0, (count + MOE_TILE - 1) // MOE_TILE, tile, 0)


def _experts(h2, slab_row, sw, cnt, off, wg, wu, wd):
    nb = h2.shape[0]
    t = h2.shape[1] // ROW_SLAB
    ne, d, ew = wg.shape
    n_list = slab_row.shape[1]
    assert h2.shape == (nb, t * ROW_SLAB, 128) and d == D_MODEL and ew == EXPERT_W and MOE_TILE % 16 == 0
    assert MOE_LIST_PAD >= MOE_WIN
    grid_spec = pltpu.PrefetchScalarGridSpec(
        num_scalar_prefetch=2,
        grid=(nb, ne),
        in_specs=[
            pl.BlockSpec((1, 1, n_list), lambda b, e, c, o: (b, 0, 0), memory_space=pltpu.SMEM),
            pl.BlockSpec((1, 1, n_list), lambda b, e, c, o: (b, 0, 0)),
            pl.BlockSpec((1, t * ROW_SLAB, 128), lambda b, e, c, o: (b, 0, 0)),
            pl.BlockSpec((1, d, ew), lambda b, e, c, o: (e, 0, 0)),
            pl.BlockSpec((1, d, ew), lambda b, e, c, o: (e, 0, 0)),
            pl.BlockSpec((1, ew, d), lambda b, e, c, o: (e, 0, 0)),
        ],
        out_specs=pl.BlockSpec((1, t * ROW_SLAB, 128), lambda b, e, c, o: (b, 0, 0)),
        scratch_shapes=[pltpu.VMEM((MOE_TILE * ROW_SLAB, 128), F32),
                        pltpu.VMEM((MOE_TILE * ROW_SLAB, 128), F32)],
    )
    return pl.pallas_call(
        _experts_kernel,
        grid_spec=grid_spec,
        out_shape=jax.ShapeDtypeStruct((nb, t * ROW_SLAB, 128), F32),
        compiler_params=_params(("parallel", "arbitrary"), VMEM_LIMIT),
        name="experts",
    )(cnt.reshape(nb * ne), off.reshape(nb * ne), slab_row.reshape(nb, 1, n_list), sw.reshape(nb, 1, n_list),
      h2, wg, wu, wd)


def _moe_out_kernel(h_ref, r_ref, x_ref, mod_ref, sg_ref, su_ref, sd_ref, *rest):
    tm = x_ref.shape[1]
    h = _slab_load(h_ref, (0,), tm).astype(BF16)
    shared = _mm((_silu(_mm(h, sg_ref[...])) * _mm(h, su_ref[...])).astype(BF16), sd_ref[...])
    y = x_ref[0] + mod_ref[0, 5:6, :] * (_slab_load(r_ref, (0,), tm) + shared)
    if len(rest) == 2:
        y = y * lax.rsqrt(jnp.mean(y * y, axis=-1, keepdims=True) + EPS) * rest[0][...]
    rest[-1][0] = y


def _moe_out(h2, routed, xs, mod, sg, su, sd, tm, skip=0, final_g=None):
    nb, t, d = xs.shape
    tok = lambda b, j: (b, j + skip, 0)
    in_specs = [
        pl.BlockSpec((1, tm * ROW_SLAB, 128), tok),
        pl.BlockSpec((1, tm * ROW_SLAB, 128), tok),
        pl.BlockSpec((1, tm, d), tok),
        pl.BlockSpec((1, 6, d), (lambda b, j: (b, 0, 0)) if skip else _mod_row(nb)),
        pl.BlockSpec(sg.shape, lambda b, j: (0, 0)),
        pl.BlockSpec(su.shape, lambda b, j: (0, 0)),
        pl.BlockSpec(sd.shape, lambda b, j: (0, 0)),
    ]
    args = [h2, routed, xs, mod, sg, su, sd]
    if final_g is not None:
        in_specs.append(pl.BlockSpec((1, d), lambda b, j: (0, 0)))
        args.append(final_g.reshape(1, d))
    return pl.pallas_call(
        _moe_out_kernel,
        grid=(nb, t // tm - skip),
        in_specs=in_specs,
        out_specs=pl.BlockSpec((1, tm, d), lambda b, j: (b, j, 0)),
        out_shape=jax.ShapeDtypeStruct((nb, t - skip * tm, d), F32),
        compiler_params=_params(("parallel", "parallel")),
        name="moe_out",
    )(*args)


def kernel(x, c, ctx, c_ctx, w_ada, b_ada, norm_mix_g, norm_ffn_g, w_in, cm_ln_g, cm_ln_b, cm_w_s, cm_b_s,
           na_rpb, ret_decay_logit, w_branch, w_out, router_w, router_bias, exp_w_gate, exp_w_up,
           exp_w_down, sh_w_gate, sh_w_up, sh_w_down, final_g):
    nb, l, d = x.shape
    lc = ctx.shape[1]
    depth = w_ada.shape[0]
    assert d == D_MODEL and nb < MOD_ROWS and l % GRID_W == 0 and lc % CHUNK == 0 and l % CHUNK == 0
    tm = 256
    assert lc % tm == 0 and l % tm == 0

    cc = jnp.zeros((MOD_ROWS, d), F32).at[:nb].set(c).at[nb].set(c_ctx)
    mod_all = _adaln(cc, w_ada, b_ada).reshape(depth, MOD_ROWS, 6, d)
    cos_t, sin_t = _rope_tables(lc, l)
    xs = jnp.concatenate([ctx, x], axis=1)

    for layer in range(depth):
        mod = mod_all[layer]
        z = _inproj(xs, mod, norm_mix_g[layer], w_in[layer].astype(BF16), tm)
        a_lat = _gmlp(z, cm_ln_g[layer], cm_ln_b[layer], cm_w_s[layer].astype(BF16),
                      jnp.transpose(cm_b_s[layer]))
        b_lat = _nattn(z, _na_bias_table(na_rpb[layer], l // GRID_W), lc)
        ret = _retention(z, ret_decay_logit[layer], cos_t, sin_t, lc)
        xs = _merge(a_lat, b_lat, ret, z, xs, mod, w_branch[layer].astype(BF16),
                    w_out[layer].astype(BF16), tm)
        h2, eidx, wk = _router(xs, mod, norm_ffn_g[layer], jnp.transpose(router_w[layer]),
                               router_bias[layer], tm)
        slab_row, sw, cnt, off = _dispatch(eidx, wk)
        routed = _experts(h2, slab_row, sw, cnt, off, exp_w_gate[layer].astype(BF16),
                          exp_w_up[layer].astype(BF16), exp_w_down[layer].astype(BF16))
        last = layer == depth - 1
        xs = _moe_out(h2, routed, xs, mod, sh_w_gate[layer].astype(BF16), sh_w_up[layer].astype(BF16),
                      sh_w_down[layer].astype(BF16), tm, skip=lc // tm if last else 0,
                      final_g=final_g if last else None)
    return xs
```

```python
import functools

import jax
import jax.numpy as jnp
import numpy as np
from jax import lax
from jax.experimental import pallas as pl
from jax.experimental.pallas import tpu as pltpu

F32 = jnp.float32
BF16 = jnp.bfloat16

D_MODEL = 1024
GRID_W = 64
CHUNK = 128
CM_GROUPS = 4
CM_WIDTH = 512
NA_HEADS = 8
NA_HEAD_DIM = 64
NA_WIDTH = NA_HEADS * NA_HEAD_DIM
WIN_R = 8
WIN_C = 16
RET_HEADS = 4
RET_DK = 64
RET_DV = 128
RET_QK_W = RET_HEADS * RET_DK
RET_V_W = RET_HEADS * RET_DV
ROPE_BASE = 10000.0
IN_W = 2 * CM_WIDTH + 3 * NA_WIDTH + 2 * RET_QK_W + 2 * RET_V_W + 3 * D_MODEL
N_EXPERTS = 64
TOP_K = 8
N_GROUPS = 8
TOPK_GROUPS = 4
EXPERT_W = 256
ROUTED_SCALE = 2.5
EPS = 1e-6
NEG = -1e30

OFF_AU, OFF_AV = 0, 512
OFF_NQ, OFF_NK, OFF_NV = 1024, 1536, 2048
OFF_RQ, OFF_RK, OFF_RV, OFF_RG = 2560, 2816, 3072, 3584
OFF_GA, OFF_GB, OFF_GC = 4096, 5120, 6144

MOD_ROWS = 16
VMEM_LIMIT = 56 * 1024 * 1024


def _params(sem, vmem=None):
    return pltpu.CompilerParams(dimension_semantics=sem, vmem_limit_bytes=vmem)


def _silu(v):
    return v * jax.nn.sigmoid(v)


def _mm(a, b):
    return jnp.dot(a, b, preferred_element_type=F32)


def _mm_nt(a, b):
    return lax.dot_general(a, b, (((1,), (1,)), ((), ())), preferred_element_type=F32)


def _mm_tn(a, b):
    return lax.dot_general(a, b, (((0,), (0,)), ((), ())), preferred_element_type=F32)


def _mod_row(nb):
    return lambda b, j, *_: (jnp.where(j == 0, nb, b), 0, 0)


def _adaln_kernel(c_ref, w_ref, b_ref, o_ref):
    s = _silu(c_ref[...])
    o_ref[0] = _mm(s, w_ref[0]) + b_ref[0]


def _adaln(cc, w_ada, b_ada):
    depth, d, n6 = w_ada.shape
    tn = 1536
    return pl.pallas_call(
        _adaln_kernel,
        grid=(depth, n6 // tn),
        in_specs=[
            pl.BlockSpec((MOD_ROWS, d), lambda l, n: (0, 0)),
            pl.BlockSpec((1, d, tn), lambda l, n: (l, 0, n)),
            pl.BlockSpec((1, 1, tn), lambda l, n: (l, 0, n)),
        ],
        out_specs=pl.BlockSpec((1, MOD_ROWS, tn), lambda l, n: (l, 0, n)),
        out_shape=jax.ShapeDtypeStruct((depth, MOD_ROWS, n6), F32),
        compiler_params=_params(("parallel", "parallel"), VMEM_LIMIT),
        name="adaln",
    )(cc, w_ada, b_ada.reshape(depth, 1, n6))


def _modulated_norm(x, g, shift, scale):
    y = x * lax.rsqrt(jnp.mean(x * x, axis=-1, keepdims=True) + EPS) * g
    return y * (1.0 + scale) + shift


def _inproj_kernel(x_ref, mod_ref, g_ref, w_ref, z_ref):
    h = _modulated_norm(x_ref[0], g_ref[...], mod_ref[0, 0:1, :], mod_ref[0, 1:2, :])
    hb = h.astype(BF16)
    nw = 512
    for n in range(IN_W // nw):
        z_ref[0, :, n * nw:(n + 1) * nw] = _mm(hb, w_ref[:, n * nw:(n + 1) * nw]).astype(BF16)


def _inproj(xs, mod, g, w_in_b, tm):
    nb, t, d = xs.shape
    return pl.pallas_call(
        _inproj_kernel,
        grid=(nb, t // tm),
        in_specs=[
            pl.BlockSpec((1, tm, d), lambda b, j: (b, j, 0)),
            pl.BlockSpec((1, 6, d), _mod_row(nb)),
            pl.BlockSpec((1, d), lambda b, j: (0, 0)),
            pl.BlockSpec((d, IN_W), lambda b, j: (0, 0)),
        ],
        out_specs=pl.BlockSpec((1, tm, IN_W), lambda b, j: (b, j, 0)),
        out_shape=jax.ShapeDtypeStruct((nb, t, IN_W), BF16),
        compiler_params=_params(("parallel", "parallel"), VMEM_LIMIT),
        name="inproj",
    )(xs, mod, g.reshape(1, d), w_in_b)


def _gmlp_kernel(u_ref, v_ref, lng_ref, lnb_ref, ws_ref, bs_ref, o_ref):
    u = jax.nn.gelu(u_ref[0].astype(F32))
    v = jax.nn.gelu(v_ref[0].astype(F32))
    vc = v - jnp.mean(v, axis=-1, keepdims=True)
    vn = vc * lax.rsqrt(jnp.mean(vc * vc, axis=-1, keepdims=True) + EPS) * lng_ref[...] + lnb_ref[...]
    vb = vn.astype(BF16)
    gw = CM_WIDTH // CM_GROUPS
    for g in range(CM_GROUPS):
        s = _mm(ws_ref[g], vb[:, g * gw:(g + 1) * gw]) + bs_ref[:, g:g + 1]
        o_ref[0, :, g * gw:(g + 1) * gw] = (u[:, g * gw:(g + 1) * gw] * s).astype(BF16)


def _gmlp(z, ln_g, ln_b, ws_b, bs_t):
    nb, t, _ = z.shape
    return pl.pallas_call(
        _gmlp_kernel,
        grid=(nb, t // CHUNK),
        in_specs=[
            pl.BlockSpec((1, CHUNK, CM_WIDTH), lambda b, j: (b, j, OFF_AU // CM_WIDTH)),
            pl.BlockSpec((1, CHUNK, CM_WIDTH), lambda b, j: (b, j, OFF_AV // CM_WIDTH)),
            pl.BlockSpec((1, CM_WIDTH), lambda b, j: (0, 0)),
            pl.BlockSpec((1, CM_WIDTH), lambda b, j: (0, 0)),
            pl.BlockSpec((CM_GROUPS, CHUNK, CHUNK), lambda b, j: (0, 0, 0)),
            pl.BlockSpec((CHUNK, CM_GROUPS), lambda b, j: (0, 0)),
        ],
        out_specs=pl.BlockSpec((1, CHUNK, CM_WIDTH), lambda b, j: (b, j, 0)),
        out_shape=jax.ShapeDtypeStruct((nb, t, CM_WIDTH), BF16),
        compiler_params=_params(("parallel", "parallel")),
        name="gmlp",
    )(z, z, ln_g.reshape(1, -1), ln_b.reshape(1, -1), ws_b, bs_t)


def _na_row_classes(rows):
    half = WIN_R // 2
    wr = min(WIN_R, rows)
    reps = list(range(half)) + [half] + list(range(rows - (wr - half) + 1, rows))
    cls = np.zeros((rows,), np.int32)
    for r in range(rows):
        if r < half:
            cls[r] = r
        elif r <= rows - (wr - half):
            cls[r] = half
        else:
            cls[r] = half + 1 + (r - (rows - (wr - half) + 1))
    return reps, cls


def _na_bias_table(rpb, rows):
    wr = min(WIN_R, rows)
    reps, _ = _na_row_classes(rows)
    nh = rpb.shape[0]
    qc = np.arange(GRID_W)[:, None]
    kc = np.arange(GRID_W)[None, :]
    col_start = np.clip(qc - WIN_C // 2, 0, GRID_W - WIN_C)
    valid = (kc >= col_start) & (kc < col_start + WIN_C)
    pad = GRID_W - WIN_C
    edge = jnp.concatenate([jnp.repeat(rpb[..., :1], pad, axis=-1), rpb.astype(F32),
                            jnp.repeat(rpb[..., -1:], pad, axis=-1)], axis=-1)
    toep = jnp.stack([edge[..., GRID_W - 1 - q:2 * GRID_W - 1 - q] for q in range(GRID_W)], axis=1)
    toep = jnp.where(valid[None, :, None, :], toep, NEG)
    tabs = []
    for r in reps:
        r0 = int(np.clip(r - WIN_R // 2, 0, rows - wr))
        lo = r0 - r + WIN_R - 1
        tabs.append(toep[:, :, lo:lo + wr, :].reshape(nh, GRID_W, wr * GRID_W))
    tab = jnp.stack(tabs)
    tab = tab.reshape(len(reps), nh // 2, 2, GRID_W, wr * GRID_W)
    return jnp.transpose(tab, (0, 1, 4, 2, 3)).reshape(len(reps), nh // 2, wr * GRID_W, 2 * GRID_W)


def _t_blocks(p):
    return jnp.concatenate([jnp.transpose(p[i:i + 128]) for i in range(0, p.shape[0], 128)],
                           axis=1).astype(BF16)


NA_ROWS = 2


def _na_kernel(cls_ref, q_ref, k_ref, v_ref, *refs, lc, rows):
    del cls_ref
    j = pl.program_id(1)
    n_ctx_blk = lc // GRID_W
    wr = min(WIN_R, rows)
    sub = lax.broadcasted_iota(jnp.int32, (128, 128), 0)
    lane = lax.broadcasted_iota(jnp.int32, (128, 128), 1)
    same_head = (sub // 64) == (lane // 64)
    first_head = lax.broadcasted_iota(jnp.int32, (GRID_W, 128), 1) < 64
    scale = NA_HEAD_DIM ** -0.5
    bias_refs, o_ref = refs[:-1], refs[-1]

    def attend(local, rr):
        qs = slice(rr * GRID_W, (rr + 1) * GRID_W)
        bias_ref = bias_refs[rr]
        q = q_ref[0, qs, :]
        if local:
            r0 = jnp.clip(j * NA_ROWS + rr - n_ctx_blk - WIN_R // 2, 0, rows - wr)
            start = pl.multiple_of(lc + r0 * GRID_W, GRID_W)
            k_loc = k_ref[0, pl.ds(start, wr * GRID_W), :]
            v_loc = v_ref[0, pl.ds(start, wr * GRID_W), :]
        k_ctx = k_ref[0, 0:lc, :]
        v_ctx = v_ref[0, 0:lc, :]
        for p in range(NA_HEADS // 2):
            cs = slice(p * 128, (p + 1) * 128)
            q2 = q[:, cs].astype(F32) * scale
            qbd = jnp.transpose(jnp.where(same_head, jnp.concatenate([q2, q2], axis=0), 0.0)).astype(BF16)
            s_ctx = _mm(k_ctx[:, cs], qbd)
            m = jnp.max(s_ctx, axis=0, keepdims=True)
            if local:
                s_loc = _mm(k_loc[:, cs], qbd) + bias_ref[0, p]
                m = jnp.maximum(m, jnp.max(s_loc, axis=0, keepdims=True))
            p_ctx = jnp.exp(s_ctx - m)
            den = jnp.sum(p_ctx, axis=0, keepdims=True)
            if local:
                p_loc = jnp.exp(s_loc - m)
                den = den + jnp.sum(p_loc, axis=0, keepdims=True)
            inv = 1.0 / den
            o = _mm(_t_blocks(p_ctx * inv), v_ctx[:, cs])
            if local:
                o = o + _mm(_t_blocks(p_loc * inv), v_loc[:, cs])
            o_ref[0, qs, cs] = jnp.where(first_head, o[:GRID_W], o[GRID_W:]).astype(BF16)

    @pl.when(j * NA_ROWS < n_ctx_blk)
    def _():
        for rr in range(NA_ROWS):
            attend(False, rr)

    @pl.when(j * NA_ROWS >= n_ctx_blk)
    def _():
        for rr in range(NA_ROWS):
            attend(True, rr)


def _nattn(z, bias_tab, lc):
    nb, t, _ = z.shape
    rows = (t - lc) // GRID_W
    n_ctx_blk = lc // GRID_W
    _, cls = _na_row_classes(rows)
    cls_arr = jnp.asarray(np.concatenate([np.zeros((n_ctx_blk,), np.int32), cls]))
    wr = min(WIN_R, rows)
    qb = NA_ROWS * GRID_W
    assert n_ctx_blk % NA_ROWS == 0 and rows % NA_ROWS == 0

    def bias_spec(rr):
        return pl.BlockSpec((1, NA_HEADS // 2, wr * GRID_W, 2 * GRID_W),
                            lambda b, j, c: (c[j * NA_ROWS + rr], 0, 0, 0))

    grid_spec = pltpu.PrefetchScalarGridSpec(
        num_scalar_prefetch=1,
        grid=(nb, t // qb),
        in_specs=[
            pl.BlockSpec((1, qb, NA_WIDTH), lambda b, j, c: (b, j, OFF_NQ // NA_WIDTH)),
            pl.BlockSpec((1, t, NA_WIDTH), lambda b, j, c: (b, 0, OFF_NK // NA_WIDTH)),
            pl.BlockSpec((1, t, NA_WIDTH), lambda b, j, c: (b, 0, OFF_NV // NA_WIDTH)),
        ] + [bias_spec(rr) for rr in range(NA_ROWS)],
        out_specs=pl.BlockSpec((1, qb, NA_WIDTH), lambda b, j, c: (b, j, 0)),
    )
    return pl.pallas_call(
        functools.partial(_na_kernel, lc=lc, rows=rows),
        grid_spec=grid_spec,
        out_shape=jax.ShapeDtypeStruct((nb, t, NA_WIDTH), BF16),
        compiler_params=_params(("parallel", "arbitrary"), VMEM_LIMIT),
        name="nattn",
    )(cls_arr, z, z, z, *([bias_tab] * NA_ROWS))


def _rope_tables(lc, l):
    half = RET_DK // 2
    nf = half // 2
    tpos = np.arange(l)
    inv = ROPE_BASE ** (-jnp.arange(nf, dtype=F32) / nf)

    def part(pos):
        ang = jnp.asarray(pos, F32)[:, None] * inv[None, :]
        c, s = jnp.cos(ang), jnp.sin(ang)
        return jnp.concatenate([c, c], -1), jnp.concatenate([-s, s], -1)

    c_r, s_r = part(tpos // GRID_W)
    c_c, s_c = part(tpos % GRID_W)
    cos = jnp.concatenate([c_r, c_c], -1)
    sin = jnp.concatenate([s_r, s_c], -1)
    cos = jnp.concatenate([jnp.ones((lc, RET_DK), F32), cos], 0)
    sin = jnp.concatenate([jnp.zeros((lc, RET_DK), F32), sin], 0)
    return jnp.tile(cos, (1, RET_HEADS)), jnp.tile(sin, (1, RET_HEADS))


def _ret_kernel(dl_ref, *refs):
    st_ref = refs[-1]

    @pl.when(pl.program_id(1) == 0)
    def _():
        st_ref[...] = jnp.zeros_like(st_ref)

    for d in range(2):
        _ret_direction(d, dl_ref, *refs[5 * d:5 * d + 5], refs[10 + d], st_ref)


def _ret_direction(d, dl_ref, q_ref, k_ref, v_ref, cos_ref, sin_ref, o_ref, st_ref):
    cos = cos_ref[...]
    sin = sin_ref[...]
    lane_qk = lax.broadcasted_iota(jnp.int32, (CHUNK, RET_QK_W), 1)
    first = (lane_qk % (RET_DK // 2)) < (RET_DK // 4)

    def rope(a):
        a = a.astype(F32)
        partner = jnp.where(first, pltpu.roll(a, RET_QK_W - RET_DK // 4, axis=1),
                            pltpu.roll(a, RET_DK // 4, axis=1))
        return a * cos + partner * sin

    q = rope(q_ref[0])
    k = rope(k_ref[0]) * (RET_DK ** -0.5)
    v = v_ref[0]

    ri = lax.broadcasted_iota(jnp.int32, (CHUNK, CHUNK), 0)
    ci = lax.broadcasted_iota(jnp.int32, (CHUNK, CHUNK), 1)
    dist = (ri - ci) if d == 0 else (ci - ri)
    keep = dist >= d
    dist_f = jnp.maximum(dist, 0).astype(F32)
    q_exp = ((ri + 1) if d == 0 else (CHUNK - ri)).astype(F32)
    k_exp = ((CHUNK - 1 - ri) if d == 0 else ri).astype(F32)
    lane_half = ci >= RET_DK
    row_half = ri >= RET_DK

    def log_gamma(h):
        xv = jnp.full((CHUNK, CHUNK), dl_ref[d, h], F32)
        return jnp.minimum(xv, 0.0) - jnp.log1p(jnp.exp(-jnp.abs(xv)))

    for p in range(RET_HEADS // 2):
        cs = slice(p * 128, (p + 1) * 128)
        lg = [log_gamma(2 * p), log_gamma(2 * p + 1)]
        q2 = q[:, cs]
        k2 = k[:, cs]
        k2b = k2.astype(BF16)
        kd_t = jnp.transpose(k2 * jnp.exp(k_exp * jnp.where(lane_half, lg[1], lg[0])))
        s2 = st_ref[d, p]
        s2b = s2.astype(BF16)
        upd = s2 * jnp.exp(float(CHUNK) * jnp.where(row_half, lg[1], lg[0]))
        for hh in range(2):
            h = 2 * p + hh
            vh = v[:, h * RET_DV:(h + 1) * RET_DV]
            sel = lane_half if hh else jnp.logical_not(lane_half)
            qm = jnp.where(sel, q2, 0.0).astype(BF16)
            a = _mm_nt(qm, k2b)
            a = jnp.where(keep, a * jnp.exp(dist_f * lg[hh]), 0.0)
            inner = _mm(a.astype(BF16), vh)
            cross = _mm(qm, s2b) * jnp.exp(q_exp * lg[hh])
            o_ref[0, :, h * RET_DV:(h + 1) * RET_DV] = (inner + cross).astype(o_ref.dtype)
            rsel = row_half if hh else jnp.logical_not(row_half)
            upd = upd + _mm(jnp.where(rsel, kd_t, 0.0).astype(BF16), vh)
        st_ref[d, p] = upd


def _retention(z, decay_logit, cos_t, sin_t, lc):
    nb, t, _ = z.shape
    nc = t // CHUNK
    ncc = lc // CHUNK

    def chunk(d, s):
        return s if d == 0 else jnp.where(s < ncc, ncc - 1 - s, nc - 1 - (s - ncc))

    def in_specs(d):
        return [
            pl.BlockSpec((1, CHUNK, RET_QK_W), lambda b, s, dl: (b, chunk(d, s), OFF_RQ // RET_QK_W)),
            pl.BlockSpec((1, CHUNK, RET_QK_W), lambda b, s, dl: (b, chunk(d, s), OFF_RK // RET_QK_W)),
            pl.BlockSpec((1, CHUNK, RET_V_W), lambda b, s, dl: (b, chunk(d, s), OFF_RV // RET_V_W)),
            pl.BlockSpec((CHUNK, RET_QK_W), lambda b, s, dl: (chunk(d, s), 0)),
            pl.BlockSpec((CHUNK, RET_QK_W), lambda b, s, dl: (chunk(d, s), 0)),
        ]

    def out_spec(d):
        return pl.BlockSpec((1, CHUNK, RET_V_W), lambda b, s, dl: (b, chunk(d, s), 0))

    grid_spec = pltpu.PrefetchScalarGridSpec(
        num_scalar_prefetch=1,
        grid=(nb, nc),
        in_specs=in_specs(0) + in_specs(1),
        out_specs=[out_spec(0), out_spec(1)],
        scratch_shapes=[pltpu.VMEM((2, RET_HEADS // 2, 2 * RET_DK, RET_DV), F32)],
    )
    return pl.pallas_call(
        _ret_kernel,
        grid_spec=grid_spec,
        out_shape=[jax.ShapeDtypeStruct((nb, t, RET_V_W), BF16)] * 2,
        compiler_params=_params(("parallel", "arbitrary")),
        name="retention",
    )(decay_logit, *([z, z, z, cos_t, sin_t] * 2))


def _merge_kernel(a_ref, b_ref, rf_ref, rb_ref, rg_ref, ga_ref, gb_ref, gc_ref, x_ref, mod_ref,
                  wb_ref, wo_ref, o_ref):
    o = rf_ref[0].astype(F32) + rb_ref[0].astype(F32)
    parts = []
    for h in range(RET_HEADS):
        oh = o[:, h * RET_DV:(h + 1) * RET_DV]
        parts.append(oh * lax.rsqrt(jnp.mean(oh * oh, axis=-1, keepdims=True) + EPS))
    c = jnp.concatenate(parts, axis=-1) * _silu(rg_ref[0].astype(F32))
    m = (jax.nn.sigmoid(ga_ref[0].astype(F32)) * _mm(a_ref[0], wb_ref[0])
         + jax.nn.sigmoid(gb_ref[0].astype(F32)) * _mm(b_ref[0], wb_ref[1])
         + jax.nn.sigmoid(gc_ref[0].astype(F32)) * _mm(c.astype(BF16), wb_ref[2]))
    mix = _mm(m.astype(BF16), wo_ref[...])
    o_ref[0] = x_ref[0] + mod_ref[0, 2:3, :] * mix


def _merge(a_lat, b_lat, ret, z, xs, mod, wb_b, wo_b, tm):
    nb, t, d = xs.shape
    bw = CM_WIDTH
    return pl.pallas_call(
        _merge_kernel,
        grid=(nb, t // tm),
        in_specs=[
            pl.BlockSpec((1, tm, bw), lambda b, j: (b, j, 0)),
            pl.BlockSpec((1, tm, bw), lambda b, j: (b, j, 0)),
            pl.BlockSpec((1, tm, bw), lambda b, j: (b, j, 0)),
            pl.BlockSpec((1, tm, bw), lambda b, j: (b, j, 0)),
            pl.BlockSpec((1, tm, RET_V_W), lambda b, j: (b, j, OFF_RG // RET_V_W)),
            pl.BlockSpec((1, tm, d), lambda b, j: (b, j, OFF_GA // d)),
            pl.BlockSpec((1, tm, d), lambda b, j: (b, j, OFF_GB // d)),
            pl.BlockSpec((1, tm, d), lambda b, j: (b, j, OFF_GC // d)),
            pl.BlockSpec((1, tm, d), lambda b, j: (b, j, 0)),
            pl.BlockSpec((1, 6, d), _mod_row(nb)),
            pl.BlockSpec((3, bw, d), lambda b, j: (0, 0, 0)),
            pl.BlockSpec((d, d), lambda b, j: (0, 0)),
        ],
        out_specs=pl.BlockSpec((1, tm, d), lambda b, j: (b, j, 0)),
        out_shape=jax.ShapeDtypeStruct((nb, t, d), F32),
        compiler_params=_params(("parallel", "parallel"), VMEM_LIMIT),
        name="merge",
    )(a_lat, b_lat, ret[0], ret[1], z, z, z, z, xs, mod, wb_b, wo_b)


ROW_SLAB = D_MODEL // 128


def _slab_load(ref, lead, n):
    return jnp.concatenate([ref[lead + (pl.ds(j, n, stride=ROW_SLAB), slice(None))] for j in range(ROW_SLAB)],
                           axis=-1)


def _slab_store(ref, lead, val):
    for j in range(ROW_SLAB):
        ref[lead + (pl.ds(j, val.shape[0], stride=ROW_SLAB), slice(None))] = val[:, j * 128:(j + 1) * 128]


def _first_index(mask, idx, big):
    return jnp.min(jnp.where(mask, idx, big), axis=0, keepdims=True)


def _router_kernel(x_ref, mod_ref, g_ref, rwt_ref, rb_ref, h_ref, ei_ref, wk_ref):
    h = _modulated_norm(x_ref[0], g_ref[...], mod_ref[0, 3:4, :], mod_ref[0, 4:5, :])
    _slab_store(h_ref, (0,), h)
    tm = h.shape[0]
    logits = _mm_nt(rwt_ref[...], h)
    scores = jax.nn.sigmoid(logits)
    biased = scores + rb_ref[:, 0:1]
    gsz = N_EXPERTS // N_GROUPS
    sub = lax.broadcasted_iota(jnp.int32, (gsz, tm), 0)
    gscore = []
    for g in range(N_GROUPS):
        blk = biased[g * gsz:(g + 1) * gsz]
        m1 = jnp.max(blk, axis=0, keepdims=True)
        i1 = _first_index(blk == m1, sub, gsz)
        m2 = jnp.max(jnp.where(sub == i1, -jnp.inf, blk), axis=0, keepdims=True)
        gscore.append(m1 + m2)
    gs = jnp.concatenate(gscore, axis=0)
    gidx = lax.broadcasted_iota(jnp.int32, (N_GROUPS, tm), 0)
    gsel = jnp.zeros((N_GROUPS, tm), jnp.int32)
    for _ in range(TOPK_GROUPS):
        m = jnp.max(gs, axis=0, keepdims=True)
        pick = gidx == _first_index(gs == m, gidx, N_GROUPS)
        gsel = jnp.where(pick, 1, gsel)
        gs = jnp.where(pick, -jnp.inf, gs)
    cand = jnp.concatenate(
        [jnp.where(gsel[g:g + 1] > 0, biased[g * gsz:(g + 1) * gsz], -jnp.inf) for g in range(N_GROUPS)],
        axis=0)
    eidx = lax.broadcasted_iota(jnp.int32, (N_EXPERTS, tm), 0)
    picks, ws = [], []
    for _ in range(TOP_K):
        m = jnp.max(cand, axis=0, keepdims=True)
        first = _first_index(cand == m, eidx, N_EXPERTS)
        pick = eidx == first
        picks.append(first)
        ws.append(jnp.sum(jnp.where(pick, scores, 0.0), axis=0, keepdims=True))
        cand = jnp.where(pick, -jnp.inf, cand)
    w = jnp.concatenate(ws, axis=0)
    ei_ref[0] = jnp.concatenate(picks, axis=0)
    wk_ref[0] = w / jnp.sum(w, axis=0, keepdims=True) * ROUTED_SCALE


def _router(xs, mod, g, rw_t, rb, tm):
    nb, t, d = xs.shape
    return pl.pallas_call(
        _router_kernel,
        grid=(nb, t // tm),
        in_specs=[
            pl.BlockSpec((1, tm, d), lambda b, j: (b, j, 0)),
            pl.BlockSpec((1, 6, d), _mod_row(nb)),
            pl.BlockSpec((1, d), lambda b, j: (0, 0)),
            pl.BlockSpec((N_EXPERTS, d), lambda b, j: (0, 0)),
            pl.BlockSpec((N_EXPERTS, 1), lambda b, j: (0, 0)),
        ],
        out_specs=[
            pl.BlockSpec((1, tm * ROW_SLAB, 128), lambda b, j: (b, j, 0)),
            pl.BlockSpec((1, TOP_K, tm), lambda b, j: (b, 0, j)),
            pl.BlockSpec((1, TOP_K, tm), lambda b, j: (b, 0, j)),
        ],
        out_shape=[jax.ShapeDtypeStruct((nb, t * ROW_SLAB, 128), F32),
                   jax.ShapeDtypeStruct((nb, TOP_K, t), jnp.int32), jax.ShapeDtypeStruct((nb, TOP_K, t), F32)],
        compiler_params=_params(("parallel", "parallel")),
        name="router",
    )(xs, mod, g.reshape(1, d), rw_t, rb.reshape(N_EXPERTS, 1))


MOE_TILE = 320
MOE_GROUP = 8
MOE_WIN = (MOE_TILE + 127) // 128 * 128 + 128
MOE_LIST_PAD = 1024


def _dispatch(eidx, wk):
    nb, k, t = eidx.shape
    pos = lax.broadcasted_iota(jnp.int32, eidx.shape, 2)
    skey, sw = lax.sort(((eidx * t + pos).reshape(nb, k * t), wk.reshape(nb, k * t)), dimension=1, num_keys=1)
    slab_row = (skey % t) * ROW_SLAB
    cnt = jnp.sum(eidx[:, None] == jnp.arange(N_EXPERTS, dtype=jnp.int32)[None, :, None, None],
                  axis=(2, 3), dtype=jnp.int32)
    off = jnp.cumsum(cnt, axis=1) - cnt
    pad = ((0, 0), (0, MOE_LIST_PAD))
    return jnp.pad(slab_row, pad), jnp.pad(sw, pad), cnt, off


def _slab(row):
    return pl.ds(pl.multiple_of(row, ROW_SLAB), ROW_SLAB)


def _expert_ffn(base, w_ref, stage_ref, ostage_ref, wg_ref, wu_ref, wd_ref):
    x = _slab_load(stage_ref, (), MOE_TILE).astype(BF16)
    act = _silu(_mm(x, wg_ref[0])) * _mm(x, wu_ref[0])
    aligned = pl.multiple_of((base // 128) * 128, 128)
    wrow = pltpu.roll(w_ref[0, :, pl.ds(aligned, MOE_WIN)], (MOE_WIN - (base - aligned)) % MOE_WIN, axis=1)
    wcol = jnp.concatenate(
        [jnp.transpose(jnp.broadcast_to(wrow[:, k * 128:(k + 1) * 128], (128, 128)))
         for k in range(MOE_WIN // 128 - 1)], axis=0)[:MOE_TILE]
    act = act * jnp.concatenate([wcol] * (EXPERT_W // 128), axis=-1)
    _slab_store(ostage_ref, (), _mm(act.astype(BF16), wd_ref[0]))


def _experts_kernel(cnt_ref, off_ref, idx_ref, w_ref, src_ref, wg_ref, wu_ref, wd_ref, acc_ref,
                    stage0, stage1, ostage0, ostage1, *, ne, spare):
    b = pl.program_id(0)
    s = pl.program_id(1)
    e_g = jnp.minimum(s, ne - 1)
    e_m = jnp.clip(s - 1, 0, ne - 1)
    e_s = jnp.maximum(s - 2, 0)
    start_g = off_ref[b * ne + e_g]
    start_m = off_ref[b * ne + e_m]
    count_m = jnp.where((s >= 1) & (s <= ne), cnt_ref[b * ne + e_m], 0)
    start_s = off_ref[b * ne + e_s]
    count_s = jnp.where(s >= 2, jnp.minimum(cnt_ref[b * ne + e_s], MOE_TILE), 0)

    @pl.when(s == 0)
    def _():
        acc_ref[...] = jnp.zeros_like(acc_ref)
        stage1[...] = jnp.zeros_like(stage1)
        ostage0[...] = jnp.zeros_like(ostage0)

    def step(stage_w, stage_r, ostage_w, ostage_r):
        for r in range(MOE_TILE):
            stage_w[_slab(r * ROW_SLAB), :] = src_ref[0, _slab(idx_ref[0, 0, start_g + r]), :]
        _expert_ffn(start_m, w_ref, stage_r, ostage_w, wg_ref, wu_ref, wd_ref)
        for g in range(MOE_TILE // MOE_GROUP):
            rws = range(g * MOE_GROUP, (g + 1) * MOE_GROUP)
            dst = [_slab(jnp.where(r < count_s, idx_ref[0, 0, start_s + r], spare)) for r in rws]
            vals = [acc_ref[0, d, :] + ostage_r[_slab(r * ROW_SLAB), :] for d, r in zip(dst, rws)]
            for d, v in zip(dst, vals):
                acc_ref[0, d, :] = v

    @pl.when(s % 2 == 0)
    def _():
        step(stage0, stage1, ostage1, ostage0)

    @pl.when(s % 2 == 1)
    def _():
        step(stage1, stage0, ostage0, ostage1)

    def overflow(stage_x, ostage_x):
        def tile(ti, carry):
            base = start_m + ti * MOE_TILE
            rows = jnp.minimum(count_m - ti * MOE_TILE, MOE_TILE)

            def gather(r, c):
                stage_x[_slab(r * ROW_SLAB), :] = src_ref[0, _slab(idx_ref[0, 0, base + r]), :]
                return c

            def scatter(r, c):
                d = _slab(idx_ref[0, 0, base + r])
                acc_ref[0, d, :] = acc_ref[0, d, :] + ostage_x[_slab(r * ROW_SLAB), :]
                return c

            lax.fori_loop(0, rows, gather, 0)
            _expert_ffn(base, w_ref, stage_x, ostage_x, wg_ref, wu_ref, wd_ref)
            lax.fori_loop(0, rows, scatter, 0)
            return carry

        lax.fori_loop(1, (count_m + MOE_TILE - 1) // MOE_TILE, tile, 0)

    @pl.when((count_m > MOE_TILE) & (s % 2 == 0))
    def _():
        overflow(stage1, ostage0)

    @pl.when((count_m > MOE_TILE) & (s % 2 == 1))
    def _():
        overflow(stage0, ostage1)


def _experts(h2, slab_row, sw, cnt, off, wg, wu, wd):
    nb = h2.shape[0]
    t = h2.shape[1] // ROW_SLAB
    ne, d, ew = wg.shape
    n_list = slab_row.shape[1]
    assert h2.shape == (nb, t * ROW_SLAB, 128) and d == D_MODEL and ew == EXPERT_W and MOE_TILE % 16 == 0
    assert MOE_LIST_PAD >= MOE_WIN
    ffn = lambda b, s, c, o: (jnp.clip(s - 1, 0, ne - 1), 0, 0)
    acc_rows = (t + 1) * ROW_SLAB
    grid_spec = pltpu.PrefetchScalarGridSpec(
        num_scalar_prefetch=2,
        grid=(nb, ne + 2),
        in_specs=[
            pl.BlockSpec((1, 1, n_list), lambda b, s, c, o: (b, 0, 0), memory_space=pltpu.SMEM),
            pl.BlockSpec((1, 1, n_list), lambda b, s, c, o: (b, 0, 0)),
            pl.BlockSpec((1, t * ROW_SLAB, 128), lambda b, s, c, o: (b, 0, 0)),
            pl.BlockSpec((1, d, ew), ffn),
            pl.BlockSpec((1, d, ew), ffn),
            pl.BlockSpec((1, ew, d), ffn),
        ],
        out_specs=pl.BlockSpec((1, acc_rows, 128), lambda b, s, c, o: (b, 0, 0)),
        scratch_shapes=[pltpu.VMEM((MOE_TILE * ROW_SLAB, 128), F32)] * 4,
    )
    return pl.pallas_call(
        functools.partial(_experts_kernel, ne=ne, spare=t * ROW_SLAB),
        grid_spec=grid_spec,
        out_shape=jax.ShapeDtypeStruct((nb, acc_rows, 128), F32),
        compiler_params=_params(("parallel", "arbitrary"), VMEM_LIMIT),
        name="experts",
    )(cnt.reshape(nb * ne), off.reshape(nb * ne), slab_row.reshape(nb, 1, n_list), sw.reshape(nb, 1, n_list),
      h2, wg, wu, wd)


def _moe_out_kernel(h_ref, r_ref, x_ref, mod_ref, sg_ref, su_ref, sd_ref, *rest):
    tm = x_ref.shape[1]
    h = _slab_load(h_ref, (0,), tm).astype(BF16)
    shared = _mm((_silu(_mm(h, sg_ref[...])) * _mm(h, su_ref[...])).astype(BF16), sd_ref[...])
    y = x_ref[0] + mod_ref[0, 5:6, :] * (_slab_load(r_ref, (0,), tm) + shared)
    if len(rest) == 2:
        y = y * lax.rsqrt(jnp.mean(y * y, axis=-1, keepdims=True) + EPS) * rest[0][...]
    rest[-1][0] = y


def _moe_out(h2, routed, xs, mod, sg, su, sd, tm, skip=0, final_g=None):
    nb, t, d = xs.shape
    tok = lambda b, j: (b, j + skip, 0)
    in_specs = [
        pl.BlockSpec((1, tm * ROW_SLAB, 128), tok),
        pl.BlockSpec((1, tm * ROW_SLAB, 128), tok),
        pl.BlockSpec((1, tm, d), tok),
        pl.BlockSpec((1, 6, d), (lambda b, j: (b, 0, 0)) if skip else _mod_row(nb)),
        pl.BlockSpec(sg.shape, lambda b, j: (0, 0)),
        pl.BlockSpec(su.shape, lambda b, j: (0, 0)),
        pl.BlockSpec(sd.shape, lambda b, j: (0, 0)),
    ]
    args = [h2, routed, xs, mod, sg, su, sd]
    if final_g is not None:
        in_specs.append(pl.BlockSpec((1, d), lambda b, j: (0, 0)))
        args.append(final_g.reshape(1, d))
    return pl.pallas_call(
        _moe_out_kernel,
        grid=(nb, t // tm - skip),
        in_specs=in_specs,
        out_specs=pl.BlockSpec((1, tm, d), lambda b, j: (b, j, 0)),
        out_shape=jax.ShapeDtypeStruct((nb, t - skip * tm, d), F32),
        compiler_params=_params(("parallel", "parallel")),
        name="moe_out",
    )(*args)


def kernel(x, c, ctx, c_ctx, w_ada, b_ada, norm_mix_g, norm_ffn_g, w_in, cm_ln_g, cm_ln_b, cm_w_s, cm_b_s,
           na_rpb, ret_decay_logit, w_branch, w_out, router_w, router_bias, exp_w_gate, exp_w_up,
           exp_w_down, sh_w_gate, sh_w_up, sh_w_down, final_g):
    nb, l, d = x.shape
    lc = ctx.shape[1]
    depth = w_ada.shape[0]
    assert d == D_MODEL and nb < MOD_ROWS and l % GRID_W == 0 and lc % CHUNK == 0 and l % CHUNK == 0
    tm = 256
    assert lc % tm == 0 and l % tm == 0

    cc = jnp.zeros((MOD_ROWS, d), F32).at[:nb].set(c).at[nb].set(c_ctx)
    mod_all = _adaln(cc, w_ada, b_ada).reshape(depth, MOD_ROWS, 6, d)
    cos_t, sin_t = _rope_tables(lc, l)
    xs = jnp.concatenate([ctx, x], axis=1)

    for layer in range(depth):
        mod = mod_all[layer]
        z = _inproj(xs, mod, norm_mix_g[layer], w_in[layer].astype(BF16), tm)
        a_lat = _gmlp(z, cm_ln_g[layer], cm_ln_b[layer], cm_w_s[layer].astype(BF16),
                      jnp.transpose(cm_b_s[layer]))
        b_lat = _nattn(z, _na_bias_table(na_rpb[layer], l // GRID_W), lc)
        ret = _retention(z, ret_decay_logit[layer], cos_t, sin_t, lc)
        xs = _merge(a_lat, b_lat, ret, z, xs, mod, w_branch[layer].astype(BF16),
                    w_out[layer].astype(BF16), tm)
        h2, eidx, wk = _router(xs, mod, norm_ffn_g[layer], jnp.transpose(router_w[layer]),
                               router_bias[layer], tm)
        slab_row, sw, cnt, off = _dispatch(eidx, wk)
        routed = _experts(h2, slab_row, sw, cnt, off, exp_w_gate[layer].astype(BF16),
                          exp_w_up[layer].astype(BF16), exp_w_down[layer].astype(BF16))
        last = layer == depth - 1
        xs = _moe_out(h2, routed, xs, mod, sh_w_gate[layer].astype(BF16), sh_w_up[layer].astype(BF16),
                      sh_w_down[layer].astype(BF16), tm, skip=lc // tm if last else 0,
                      final_g=final_g if last else None)
    return xs
```

```python
import functools

import jax
import jax.numpy as jnp
import numpy as np
from jax import lax
from jax.experimental import pallas as pl
from jax.experimental.pallas import tpu as pltpu

F32 = jnp.float32
BF16 = jnp.bfloat16

D_MODEL = 1024
GRID_W = 64
CHUNK = 128
CM_GROUPS = 4
CM_WIDTH = 512
NA_HEADS = 8
NA_HEAD_DIM = 64
NA_WIDTH = NA_HEADS * NA_HEAD_DIM
WIN_R = 8
WIN_C = 16
RET_HEADS = 4
RET_DK = 64
RET_DV = 128
RET_QK_W = RET_HEADS * RET_DK
RET_V_W = RET_HEADS * RET_DV
ROPE_BASE = 10000.0
IN_W = 2 * CM_WIDTH + 3 * NA_WIDTH + 2 * RET_QK_W + 2 * RET_V_W + 3 * D_MODEL
N_EXPERTS = 64
TOP_K = 8
N_GROUPS = 8
TOPK_GROUPS = 4
EXPERT_W = 256
ROUTED_SCALE = 2.5
EPS = 1e-6
NEG = -1e30

OFF_AU, OFF_AV = 0, 512
OFF_NQ, OFF_NK, OFF_NV = 1024, 1536, 2048
OFF_RQ, OFF_RK, OFF_RV, OFF_RG = 2560, 2816, 3072, 3584
OFF_GA, OFF_GB, OFF_GC = 4096, 5120, 6144

MOD_ROWS = 16
VMEM_LIMIT = 56 * 1024 * 1024


def _params(sem, vmem=None):
    return pltpu.CompilerParams(dimension_semantics=sem, vmem_limit_bytes=vmem)


def _silu(v):
    return v * jax.nn.sigmoid(v)


def _mm(a, b):
    return jnp.dot(a, b, preferred_element_type=F32)


def _mm_nt(a, b):
    return lax.dot_general(a, b, (((1,), (1,)), ((), ())), preferred_element_type=F32)


def _mm_tn(a, b):
    return lax.dot_general(a, b, (((0,), (0,)), ((), ())), preferred_element_type=F32)


def _mod_row(nb):
    return lambda b, j, *_: (jnp.where(j == 0, nb, b), 0, 0)


def _adaln_kernel(c_ref, w_ref, b_ref, o_ref):
    s = _silu(c_ref[...])
    o_ref[0] = _mm(s, w_ref[0]) + b_ref[0]


def _adaln(cc, w_ada, b_ada):
    depth, d, n6 = w_ada.shape
    tn = 1536
    return pl.pallas_call(
        _adaln_kernel,
        grid=(depth, n6 // tn),
        in_specs=[
            pl.BlockSpec((MOD_ROWS, d), lambda l, n: (0, 0)),
            pl.BlockSpec((1, d, tn), lambda l, n: (l, 0, n)),
            pl.BlockSpec((1, 1, tn), lambda l, n: (l, 0, n)),
        ],
        out_specs=pl.BlockSpec((1, MOD_ROWS, tn), lambda l, n: (l, 0, n)),
        out_shape=jax.ShapeDtypeStruct((depth, MOD_ROWS, n6), F32),
        compiler_params=_params(("parallel", "parallel"), VMEM_LIMIT),
        name="adaln",
    )(cc, w_ada, b_ada.reshape(depth, 1, n6))


def _modulated_norm(x, g, shift, scale):
    y = x * lax.rsqrt(jnp.mean(x * x, axis=-1, keepdims=True) + EPS) * g
    return y * (1.0 + scale) + shift


def _inproj_kernel(x_ref, mod_ref, g_ref, w_ref, z_ref):
    h = _modulated_norm(x_ref[0], g_ref[...], mod_ref[0, 0:1, :], mod_ref[0, 1:2, :])
    hb = h.astype(BF16)
    nw = 512
    for n in range(IN_W // nw):
        z_ref[0, :, n * nw:(n + 1) * nw] = _mm(hb, w_ref[:, n * nw:(n + 1) * nw]).astype(BF16)


def _inproj(xs, mod, g, w_in_b, tm):
    nb, t, d = xs.shape
    return pl.pallas_call(
        _inproj_kernel,
        grid=(nb, t // tm),
        in_specs=[
            pl.BlockSpec((1, tm, d), lambda b, j: (b, j, 0)),
            pl.BlockSpec((1, 6, d), _mod_row(nb)),
            pl.BlockSpec((1, d), lambda b, j: (0, 0)),
            pl.BlockSpec((d, IN_W), lambda b, j: (0, 0)),
        ],
        out_specs=pl.BlockSpec((1, tm, IN_W), lambda b, j: (b, j, 0)),
        out_shape=jax.ShapeDtypeStruct((nb, t, IN_W), BF16),
        compiler_params=_params(("parallel", "parallel"), VMEM_LIMIT),
        name="inproj",
    )(xs, mod, g.reshape(1, d), w_in_b)


def _gmlp_kernel(u_ref, v_ref, lng_ref, lnb_ref, ws_ref, bs_ref, o_ref):
    u = jax.nn.gelu(u_ref[0].astype(F32))
    v = jax.nn.gelu(v_ref[0].astype(F32))
    vc = v - jnp.mean(v, axis=-1, keepdims=True)
    vn = vc * lax.rsqrt(jnp.mean(vc * vc, axis=-1, keepdims=True) + EPS) * lng_ref[...] + lnb_ref[...]
    vb = vn.astype(BF16)
    gw = CM_WIDTH // CM_GROUPS
    for c in range(u.shape[0] // CHUNK):
        rs = slice(c * CHUNK, (c + 1) * CHUNK)
        for g in range(CM_GROUPS):
            gs = slice(g * gw, (g + 1) * gw)
            s = _mm(ws_ref[g], vb[rs, gs]) + bs_ref[:, g:g + 1]
            o_ref[0, rs, gs] = (u[rs, gs] * s).astype(BF16)


GMLP_ROWS = 2 * CHUNK


def _gmlp(z, ln_g, ln_b, ws_b, bs_t):
    nb, t, _ = z.shape
    assert t % GMLP_ROWS == 0
    return pl.pallas_call(
        _gmlp_kernel,
        grid=(nb, t // GMLP_ROWS),
        in_specs=[
            pl.BlockSpec((1, GMLP_ROWS, CM_WIDTH), lambda b, j: (b, j, OFF_AU // CM_WIDTH)),
            pl.BlockSpec((1, GMLP_ROWS, CM_WIDTH), lambda b, j: (b, j, OFF_AV // CM_WIDTH)),
            pl.BlockSpec((1, CM_WIDTH), lambda b, j: (0, 0)),
            pl.BlockSpec((1, CM_WIDTH), lambda b, j: (0, 0)),
            pl.BlockSpec((CM_GROUPS, CHUNK, CHUNK), lambda b, j: (0, 0, 0)),
            pl.BlockSpec((CHUNK, CM_GROUPS), lambda b, j: (0, 0)),
        ],
        out_specs=pl.BlockSpec((1, GMLP_ROWS, CM_WIDTH), lambda b, j: (b, j, 0)),
        out_shape=jax.ShapeDtypeStruct((nb, t, CM_WIDTH), BF16),
        compiler_params=_params(("parallel", "parallel")),
        name="gmlp",
    )(z, z, ln_g.reshape(1, -1), ln_b.reshape(1, -1), ws_b, bs_t)


def _na_row_classes(rows):
    half = WIN_R // 2
    wr = min(WIN_R, rows)
    reps = list(range(half)) + [half] + list(range(rows - (wr - half) + 1, rows))
    cls = np.zeros((rows,), np.int32)
    for r in range(rows):
        if r < half:
            cls[r] = r
        elif r <= rows - (wr - half):
            cls[r] = half
        else:
            cls[r] = half + 1 + (r - (rows - (wr - half) + 1))
    return reps, cls


def _na_bias_table(rpb, rows):
    wr = min(WIN_R, rows)
    reps, _ = _na_row_classes(rows)
    nh = rpb.shape[0]
    qc = np.arange(GRID_W)[:, None]
    kc = np.arange(GRID_W)[None, :]
    col_start = np.clip(qc - WIN_C // 2, 0, GRID_W - WIN_C)
    valid = (kc >= col_start) & (kc < col_start + WIN_C)
    pad = GRID_W - WIN_C
    edge = jnp.concatenate([jnp.repeat(rpb[..., :1], pad, axis=-1), rpb.astype(F32),
                            jnp.repeat(rpb[..., -1:], pad, axis=-1)], axis=-1)
    toep = jnp.stack([edge[..., GRID_W - 1 - q:2 * GRID_W - 1 - q] for q in range(GRID_W)], axis=1)
    toep = jnp.where(valid[None, :, None, :], toep, NEG)
    tabs = []
    for r in reps:
        r0 = int(np.clip(r - WIN_R // 2, 0, rows - wr))
        lo = r0 - r + WIN_R - 1
        tabs.append(toep[:, :, lo:lo + wr, :].reshape(nh, GRID_W, wr * GRID_W))
    tab = jnp.stack(tabs)
    tab = tab.reshape(len(reps), nh // 2, 2, GRID_W, wr * GRID_W)
    return jnp.transpose(tab, (0, 1, 4, 2, 3)).reshape(len(reps), nh // 2, wr * GRID_W, 2 * GRID_W)


def _t_blocks(p):
    return jnp.concatenate([jnp.transpose(p[i:i + 128]) for i in range(0, p.shape[0], 128)],
                           axis=1).astype(BF16)


NA_ROWS = 2


def _na_kernel(cls_ref, q_ref, k_ref, v_ref, *refs, lc, rows):
    del cls_ref
    j = pl.program_id(1)
    n_ctx_blk = lc // GRID_W
    wr = min(WIN_R, rows)
    sub = lax.broadcasted_iota(jnp.int32, (128, 128), 0)
    lane = lax.broadcasted_iota(jnp.int32, (128, 128), 1)
    same_head = (sub // 64) == (lane // 64)
    first_head = lax.broadcasted_iota(jnp.int32, (GRID_W, 128), 1) < 64
    scale = NA_HEAD_DIM ** -0.5
    bias_refs, o_ref = refs[:-1], refs[-1]

    def attend(local, rr):
        qs = slice(rr * GRID_W, (rr + 1) * GRID_W)
        bias_ref = bias_refs[rr]
        q = q_ref[0, qs, :]
        if local:
            r0 = jnp.clip(j * NA_ROWS + rr - n_ctx_blk - WIN_R // 2, 0, rows - wr)
            start = pl.multiple_of(lc + r0 * GRID_W, GRID_W)
            k_loc = k_ref[0, pl.ds(start, wr * GRID_W), :]
            v_loc = v_ref[0, pl.ds(start, wr * GRID_W), :]
        k_ctx = k_ref[0, 0:lc, :]
        v_ctx = v_ref[0, 0:lc, :]
        for p in range(NA_HEADS // 2):
            cs = slice(p * 128, (p + 1) * 128)
            q2 = q[:, cs].astype(F32) * scale
            qbd = jnp.transpose(jnp.where(same_head, jnp.concatenate([q2, q2], axis=0), 0.0)).astype(BF16)
            s_ctx = _mm(k_ctx[:, cs], qbd)
            m = jnp.max(s_ctx, axis=0, keepdims=True)
            if local:
                s_loc = _mm(k_loc[:, cs], qbd) + bias_ref[0, p]
                m = jnp.maximum(m, jnp.max(s_loc, axis=0, keepdims=True))
            p_ctx = jnp.exp(s_ctx - m)
            den = jnp.sum(p_ctx, axis=0, keepdims=True)
            if local:
                p_loc = jnp.exp(s_loc - m)
                den = den + jnp.sum(p_loc, axis=0, keepdims=True)
            inv = 1.0 / den
            o = _mm(_t_blocks(p_ctx * inv), v_ctx[:, cs])
            if local:
                o = o + _mm(_t_blocks(p_loc * inv), v_loc[:, cs])
            o_ref[0, qs, cs] = jnp.where(first_head, o[:GRID_W], o[GRID_W:]).astype(BF16)

    @pl.when(j * NA_ROWS < n_ctx_blk)
    def _():
        for rr in range(NA_ROWS):
            attend(False, rr)

    @pl.when(j * NA_ROWS >= n_ctx_blk)
    def _():
        for rr in range(NA_ROWS):
            attend(True, rr)


def _nattn(z, bias_tab, lc):
    nb, t, _ = z.shape
    rows = (t - lc) // GRID_W
    n_ctx_blk = lc // GRID_W
    _, cls = _na_row_classes(rows)
    cls_arr = jnp.asarray(np.concatenate([np.zeros((n_ctx_blk,), np.int32), cls]))
    wr = min(WIN_R, rows)
    qb = NA_ROWS * GRID_W
    assert n_ctx_blk % NA_ROWS == 0 and rows % NA_ROWS == 0

    def bias_spec(rr):
        return pl.BlockSpec((1, NA_HEADS // 2, wr * GRID_W, 2 * GRID_W),
                            lambda b, j, c: (c[j * NA_ROWS + rr], 0, 0, 0))

    grid_spec = pltpu.PrefetchScalarGridSpec(
        num_scalar_prefetch=1,
        grid=(nb, t // qb),
        in_specs=[
            pl.BlockSpec((1, qb, NA_WIDTH), lambda b, j, c: (b, j, OFF_NQ // NA_WIDTH)),
            pl.BlockSpec((1, t, NA_WIDTH), lambda b, j, c: (b, 0, OFF_NK // NA_WIDTH)),
            pl.BlockSpec((1, t, NA_WIDTH), lambda b, j, c: (b, 0, OFF_NV // NA_WIDTH)),
        ] + [bias_spec(rr) for rr in range(NA_ROWS)],
        out_specs=pl.BlockSpec((1, qb, NA_WIDTH), lambda b, j, c: (b, j, 0)),
    )
    return pl.pallas_call(
        functools.partial(_na_kernel, lc=lc, rows=rows),
        grid_spec=grid_spec,
        out_shape=jax.ShapeDtypeStruct((nb, t, NA_WIDTH), BF16),
        compiler_params=_params(("parallel", "arbitrary"), VMEM_LIMIT),
        name="nattn",
    )(cls_arr, z, z, z, *([bias_tab] * NA_ROWS))


def _rope_tables(lc, l):
    half = RET_DK // 2
    nf = half // 2
    tpos = np.arange(l)
    inv = ROPE_BASE ** (-jnp.arange(nf, dtype=F32) / nf)

    def part(pos):
        ang = jnp.asarray(pos, F32)[:, None] * inv[None, :]
        c, s = jnp.cos(ang), jnp.sin(ang)
        return jnp.concatenate([c, c], -1), jnp.concatenate([-s, s], -1)

    c_r, s_r = part(tpos // GRID_W)
    c_c, s_c = part(tpos % GRID_W)
    cos = jnp.concatenate([c_r, c_c], -1)
    sin = jnp.concatenate([s_r, s_c], -1)
    cos = jnp.concatenate([jnp.ones((lc, RET_DK), F32), cos], 0)
    sin = jnp.concatenate([jnp.zeros((lc, RET_DK), F32), sin], 0)
    return jnp.tile(cos, (1, RET_HEADS)), jnp.tile(sin, (1, RET_HEADS))


def _ret_kernel(dl_ref, *refs):
    st_ref = refs[-1]

    @pl.when(pl.program_id(1) == 0)
    def _():
        st_ref[...] = jnp.zeros_like(st_ref)

    for d in range(2):
        _ret_direction(d, dl_ref, *refs[5 * d:5 * d + 5], refs[10 + d], st_ref)


def _ret_direction(d, dl_ref, q_ref, k_ref, v_ref, cos_ref, sin_ref, o_ref, st_ref):
    cos = cos_ref[...]
    sin = sin_ref[...]
    lane_qk = lax.broadcasted_iota(jnp.int32, (CHUNK, RET_QK_W), 1)
    first = (lane_qk % (RET_DK // 2)) < (RET_DK // 4)

    def rope(a):
        a = a.astype(F32)
        partner = jnp.where(first, pltpu.roll(a, RET_QK_W - RET_DK // 4, axis=1),
                            pltpu.roll(a, RET_DK // 4, axis=1))
        return a * cos + partner * sin

    q = rope(q_ref[0])
    k = rope(k_ref[0]) * (RET_DK ** -0.5)
    v = v_ref[0]

    ri = lax.broadcasted_iota(jnp.int32, (CHUNK, CHUNK), 0)
    ci = lax.broadcasted_iota(jnp.int32, (CHUNK, CHUNK), 1)
    dist = (ri - ci) if d == 0 else (ci - ri)
    keep = dist >= d
    dist_f = jnp.maximum(dist, 0).astype(F32)
    q_exp = ((ri + 1) if d == 0 else (CHUNK - ri)).astype(F32)
    k_exp = ((CHUNK - 1 - ri) if d == 0 else ri).astype(F32)
    lane_half = ci >= RET_DK
    row_half = ri >= RET_DK

    def log_gamma(h):
        xv = jnp.full((CHUNK, CHUNK), dl_ref[d, h], F32)
        return jnp.minimum(xv, 0.0) - jnp.log1p(jnp.exp(-jnp.abs(xv)))

    for p in range(RET_HEADS // 2):
        cs = slice(p * 128, (p + 1) * 128)
        lg = [log_gamma(2 * p), log_gamma(2 * p + 1)]
        q2 = q[:, cs]
        k2 = k[:, cs]
        k2b = k2.astype(BF16)
        kd_t = jnp.transpose(k2 * jnp.exp(k_exp * jnp.where(lane_half, lg[1], lg[0])))
        s2 = st_ref[d, p]
        s2b = s2.astype(BF16)
        upd = s2 * jnp.exp(float(CHUNK) * jnp.where(row_half, lg[1], lg[0]))
        for hh in range(2):
            h = 2 * p + hh
            vh = v[:, h * RET_DV:(h + 1) * RET_DV]
            sel = lane_half if hh else jnp.logical_not(lane_half)
            qm = jnp.where(sel, q2, 0.0).astype(BF16)
            a = _mm_nt(qm, k2b)
            a = jnp.where(keep, a * jnp.exp(dist_f * lg[hh]), 0.0)
            inner = _mm(a.astype(BF16), vh)
            cross = _mm(qm, s2b) * jnp.exp(q_exp * lg[hh])
            o_ref[0, :, h * RET_DV:(h + 1) * RET_DV] = (inner + cross).astype(o_ref.dtype)
            rsel = row_half if hh else jnp.logical_not(row_half)
            upd = upd + _mm(jnp.where(rsel, kd_t, 0.0).astype(BF16), vh)
        st_ref[d, p] = upd


def _retention(z, decay_logit, cos_t, sin_t, lc):
    nb, t, _ = z.shape
    nc = t // CHUNK
    ncc = lc // CHUNK

    def chunk(d, s):
        return s if d == 0 else jnp.where(s < ncc, ncc - 1 - s, nc - 1 - (s - ncc))

    def in_specs(d):
        return [
            pl.BlockSpec((1, CHUNK, RET_QK_W), lambda b, s, dl: (b, chunk(d, s), OFF_RQ // RET_QK_W)),
            pl.BlockSpec((1, CHUNK, RET_QK_W), lambda b, s, dl: (b, chunk(d, s), OFF_RK // RET_QK_W)),
            pl.BlockSpec((1, CHUNK, RET_V_W), lambda b, s, dl: (b, chunk(d, s), OFF_RV // RET_V_W)),
            pl.BlockSpec((CHUNK, RET_QK_W), lambda b, s, dl: (chunk(d, s), 0)),
            pl.BlockSpec((CHUNK, RET_QK_W), lambda b, s, dl: (chunk(d, s), 0)),
        ]

    def out_spec(d):
        return pl.BlockSpec((1, CHUNK, RET_V_W), lambda b, s, dl: (b, chunk(d, s), 0))

    grid_spec = pltpu.PrefetchScalarGridSpec(
        num_scalar_prefetch=1,
        grid=(nb, nc),
        in_specs=in_specs(0) + in_specs(1),
        out_specs=[out_spec(0), out_spec(1)],
        scratch_shapes=[pltpu.VMEM((2, RET_HEADS // 2, 2 * RET_DK, RET_DV), F32)],
    )
    return pl.pallas_call(
        _ret_kernel,
        grid_spec=grid_spec,
        out_shape=[jax.ShapeDtypeStruct((nb, t, RET_V_W), BF16)] * 2,
        compiler_params=_params(("parallel", "arbitrary")),
        name="retention",
    )(decay_logit, *([z, z, z, cos_t, sin_t] * 2))


def _merge_kernel(a_ref, b_ref, rf_ref, rb_ref, rg_ref, ga_ref, gb_ref, gc_ref, x_ref, mod_ref,
                  wb_ref, wo_ref, o_ref):
    o = rf_ref[0].astype(F32) + rb_ref[0].astype(F32)
    parts = []
    for h in range(RET_HEADS):
        oh = o[:, h * RET_DV:(h + 1) * RET_DV]
        parts.append(oh * lax.rsqrt(jnp.mean(oh * oh, axis=-1, keepdims=True) + EPS))
    c = jnp.concatenate(parts, axis=-1) * _silu(rg_ref[0].astype(F32))
    m = (jax.nn.sigmoid(ga_ref[0].astype(F32)) * _mm(a_ref[0], wb_ref[0])
         + jax.nn.sigmoid(gb_ref[0].astype(F32)) * _mm(b_ref[0], wb_ref[1])
         + jax.nn.sigmoid(gc_ref[0].astype(F32)) * _mm(c.astype(BF16), wb_ref[2]))
    mix = _mm(m.astype(BF16), wo_ref[...])
    o_ref[0] = x_ref[0] + mod_ref[0, 2:3, :] * mix


def _merge(a_lat, b_lat, ret, z, xs, mod, wb_b, wo_b, tm):
    nb, t, d = xs.shape
    bw = CM_WIDTH
    return pl.pallas_call(
        _merge_kernel,
        grid=(nb, t // tm),
        in_specs=[
            pl.BlockSpec((1, tm, bw), lambda b, j: (b, j, 0)),
            pl.BlockSpec((1, tm, bw), lambda b, j: (b, j, 0)),
            pl.BlockSpec((1, tm, bw), lambda b, j: (b, j, 0)),
            pl.BlockSpec((1, tm, bw), lambda b, j: (b, j, 0)),
            pl.BlockSpec((1, tm, RET_V_W), lambda b, j: (b, j, OFF_RG // RET_V_W)),
            pl.BlockSpec((1, tm, d), lambda b, j: (b, j, OFF_GA // d)),
            pl.BlockSpec((1, tm, d), lambda b, j: (b, j, OFF_GB // d)),
            pl.BlockSpec((1, tm, d), lambda b, j: (b, j, OFF_GC // d)),
            pl.BlockSpec((1, tm, d), lambda b, j: (b, j, 0)),
            pl.BlockSpec((1, 6, d), _mod_row(nb)),
            pl.BlockSpec((3, bw, d), lambda b, j: (0, 0, 0)),
            pl.BlockSpec((d, d), lambda b, j: (0, 0)),
        ],
        out_specs=pl.BlockSpec((1, tm, d), lambda b, j: (b, j, 0)),
        out_shape=jax.ShapeDtypeStruct((nb, t, d), F32),
        compiler_params=_params(("parallel", "parallel"), VMEM_LIMIT),
        name="merge",
    )(a_lat, b_lat, ret[0], ret[1], z, z, z, z, xs, mod, wb_b, wo_b)


ROW_SLAB = D_MODEL // 128


def _slab_load(ref, lead, n, row0=0):
    return jnp.concatenate(
        [ref[lead + (pl.ds(row0 * ROW_SLAB + j, n, stride=ROW_SLAB), slice(None))] for j in range(ROW_SLAB)],
        axis=-1)


def _slab_store(ref, lead, val, row0=0):
    for j in range(ROW_SLAB):
        ref[lead + (pl.ds(row0 * ROW_SLAB + j, val.shape[0], stride=ROW_SLAB), slice(None))] = (
            val[:, j * 128:(j + 1) * 128])


def _first_index(mask, idx, big):
    return jnp.min(jnp.where(mask, idx, big), axis=0, keepdims=True)


def _router_kernel(x_ref, mod_ref, g_ref, rwt_ref, rb_ref, h_ref, ei_ref, wk_ref):
    h = _modulated_norm(x_ref[0], g_ref[...], mod_ref[0, 3:4, :], mod_ref[0, 4:5, :])
    _slab_store(h_ref, (0,), h)
    tm = h.shape[0]
    logits = _mm_nt(rwt_ref[...], h)
    scores = jax.nn.sigmoid(logits)
    biased = scores + rb_ref[:, 0:1]
    gsz = N_EXPERTS // N_GROUPS
    sub = lax.broadcasted_iota(jnp.int32, (gsz, tm), 0)
    gscore = []
    for g in range(N_GROUPS):
        blk = biased[g * gsz:(g + 1) * gsz]
        m1 = jnp.max(blk, axis=0, keepdims=True)
        i1 = _first_index(blk == m1, sub, gsz)
        m2 = jnp.max(jnp.where(sub == i1, -jnp.inf, blk), axis=0, keepdims=True)
        gscore.append(m1 + m2)
    gs = jnp.concatenate(gscore, axis=0)
    gidx = lax.broadcasted_iota(jnp.int32, (N_GROUPS, tm), 0)
    gsel = jnp.zeros((N_GROUPS, tm), jnp.int32)
    for _ in range(TOPK_GROUPS):
        m = jnp.max(gs, axis=0, keepdims=True)
        pick = gidx == _first_index(gs == m, gidx, N_GROUPS)
        gsel = jnp.where(pick, 1, gsel)
        gs = jnp.where(pick, -jnp.inf, gs)
    cand = jnp.concatenate(
        [jnp.where(gsel[g:g + 1] > 0, biased[g * gsz:(g + 1) * gsz], -jnp.inf) for g in range(N_GROUPS)],
        axis=0)
    eidx = lax.broadcasted_iota(jnp.int32, (N_EXPERTS, tm), 0)
    picks, ws = [], []
    for _ in range(TOP_K):
        m = jnp.max(cand, axis=0, keepdims=True)
        first = _first_index(cand == m, eidx, N_EXPERTS)
        pick = eidx == first
        picks.append(first)
        ws.append(jnp.sum(jnp.where(pick, scores, 0.0), axis=0, keepdims=True))
        cand = jnp.where(pick, -jnp.inf, cand)
    w = jnp.concatenate(ws, axis=0)
    ei_ref[0] = jnp.concatenate(picks, axis=0)
    wk_ref[0] = w / jnp.sum(w, axis=0, keepdims=True) * ROUTED_SCALE


def _router(xs, mod, g, rw_t, rb, tm):
    nb, t, d = xs.shape
    return pl.pallas_call(
        _router_kernel,
        grid=(nb, t // tm),
        in_specs=[
            pl.BlockSpec((1, tm, d), lambda b, j: (b, j, 0)),
            pl.BlockSpec((1, 6, d), _mod_row(nb)),
            pl.BlockSpec((1, d), lambda b, j: (0, 0)),
            pl.BlockSpec((N_EXPERTS, d), lambda b, j: (0, 0)),
            pl.BlockSpec((N_EXPERTS, 1), lambda b, j: (0, 0)),
        ],
        out_specs=[
            pl.BlockSpec((1, tm * ROW_SLAB, 128), lambda b, j: (b, j, 0)),
            pl.BlockSpec((1, TOP_K, tm), lambda b, j: (b, 0, j)),
            pl.BlockSpec((1, TOP_K, tm), lambda b, j: (b, 0, j)),
        ],
        out_shape=[jax.ShapeDtypeStruct((nb, t * ROW_SLAB, 128), F32),
                   jax.ShapeDtypeStruct((nb, TOP_K, t), jnp.int32), jax.ShapeDtypeStruct((nb, TOP_K, t), F32)],
        compiler_params=_params(("parallel", "parallel")),
        name="router",
    )(xs, mod, g.reshape(1, d), rw_t, rb.reshape(N_EXPERTS, 1))


MOE_TILE = 320
MOE_GROUP = 8
MOE_EPS = 2
MOE_WIN = (MOE_TILE + 127) // 128 * 128 + 128
MOE_LIST_PAD = 1024


def _dispatch(eidx, wk):
    nb, k, t = eidx.shape
    pos = lax.broadcasted_iota(jnp.int32, eidx.shape, 2)
    skey, sw = lax.sort(((eidx * t + pos).reshape(nb, k * t), wk.reshape(nb, k * t)), dimension=1, num_keys=1)
    slab_row = (skey % t) * ROW_SLAB
    cnt = jnp.sum(eidx[:, None] == jnp.arange(N_EXPERTS, dtype=jnp.int32)[None, :, None, None],
                  axis=(2, 3), dtype=jnp.int32)
    off = jnp.cumsum(cnt, axis=1) - cnt
    pad = ((0, 0), (0, MOE_LIST_PAD))
    return jnp.pad(slab_row, pad), jnp.pad(sw, pad), cnt, off


def _slab(row):
    return pl.ds(pl.multiple_of(row, ROW_SLAB), ROW_SLAB)


def _expert_ffn(base, w_ref, stage_ref, ostage_ref, wg_ref, wu_ref, wd_ref, k):
    x = _slab_load(stage_ref, (), MOE_TILE, k * MOE_TILE).astype(BF16)
    act = _silu(_mm(x, wg_ref[k])) * _mm(x, wu_ref[k])
    aligned = pl.multiple_of((base // 128) * 128, 128)
    wrow = pltpu.roll(w_ref[0, :, pl.ds(aligned, MOE_WIN)], (MOE_WIN - (base - aligned)) % MOE_WIN, axis=1)
    wcol = jnp.concatenate(
        [jnp.transpose(jnp.broadcast_to(wrow[:, k * 128:(k + 1) * 128], (128, 128)))
         for k in range(MOE_WIN // 128 - 1)], axis=0)[:MOE_TILE]
    act = act * jnp.concatenate([wcol] * (EXPERT_W // 128), axis=-1)
    _slab_store(ostage_ref, (), _mm(act.astype(BF16), wd_ref[k]), k * MOE_TILE)


def _experts_kernel(cnt_ref, off_ref, idx_ref, w_ref, src_ref, wg_ref, wu_ref, wd_ref, acc_ref,
                    stage0, stage1, ostage0, ostage1, *, ne, spare):
    b = pl.program_id(0)
    s = pl.program_id(1)
    ns = ne // MOE_EPS

    def expert(stage, k):
        return b * ne + stage * MOE_EPS + k

    st_g = jnp.minimum(s, ns - 1)
    st_m = jnp.clip(s - 1, 0, ns - 1)
    st_s = jnp.maximum(s - 2, 0)
    ffn_live = (s >= 1) & (s <= ns)
    start_g = [off_ref[expert(st_g, k)] for k in range(MOE_EPS)]
    start_m = [off_ref[expert(st_m, k)] for k in range(MOE_EPS)]
    count_m = [jnp.where(ffn_live, cnt_ref[expert(st_m, k)], 0) for k in range(MOE_EPS)]
    start_s = [off_ref[expert(st_s, k)] for k in range(MOE_EPS)]
    count_s = [jnp.where(s >= 2, jnp.minimum(cnt_ref[expert(st_s, k)], MOE_TILE), 0) for k in range(MOE_EPS)]

    @pl.when(s == 0)
    def _():
        acc_ref[...] = jnp.zeros_like(acc_ref)
        stage1[...] = jnp.zeros_like(stage1)
        ostage0[...] = jnp.zeros_like(ostage0)

    def step(stage_w, stage_r, ostage_w, ostage_r):
        for k in range(MOE_EPS):
            for r in range(MOE_TILE):
                stage_w[_slab((k * MOE_TILE + r) * ROW_SLAB), :] = (
                    src_ref[0, _slab(idx_ref[0, 0, start_g[k] + r]), :])
        for k in range(MOE_EPS):
            _expert_ffn(start_m[k], w_ref, stage_r, ostage_w, wg_ref, wu_ref, wd_ref, k)
        for k in range(MOE_EPS):
            for g in range(MOE_TILE // MOE_GROUP):
                rws = range(g * MOE_GROUP, (g + 1) * MOE_GROUP)
                dst = [_slab(jnp.where(r < count_s[k], idx_ref[0, 0, start_s[k] + r], spare)) for r in rws]
                vals = [acc_ref[0, d, :] + ostage_r[_slab((k * MOE_TILE + r) * ROW_SLAB), :]
                        for d, r in zip(dst, rws)]
                for d, v in zip(dst, vals):
                    acc_ref[0, d, :] = v

    @pl.when(s % 2 == 0)
    def _():
        step(stage0, stage1, ostage1, ostage0)

    @pl.when(s % 2 == 1)
    def _():
        step(stage1, stage0, ostage0, ostage1)

    def overflow(stage_x, ostage_x, k):
        def tile(ti, carry):
            base = start_m[k] + ti * MOE_TILE
            rows = jnp.minimum(count_m[k] - ti * MOE_TILE, MOE_TILE)

            def gather(r, c):
                stage_x[_slab((k * MOE_TILE + r) * ROW_SLAB), :] = src_ref[0, _slab(idx_ref[0, 0, base + r]), :]
                return c

            def scatter(r, c):
                d = _slab(idx_ref[0, 0, base + r])
                acc_ref[0, d, :] = acc_ref[0, d, :] + ostage_x[_slab((k * MOE_TILE + r) * ROW_SLAB), :]
                return c

            lax.fori_loop(0, rows, gather, 0)
            _expert_ffn(base, w_ref, stage_x, ostage_x, wg_ref, wu_ref, wd_ref, k)
            lax.fori_loop(0, rows, scatter, 0)
            return carry

        lax.fori_loop(1, (count_m[k] + MOE_TILE - 1) // MOE_TILE, tile, 0)

    for k in range(MOE_EPS):
        @pl.when((count_m[k] > MOE_TILE) & (s % 2 == 0))
        def _(k=k):
            overflow(stage1, ostage0, k)

        @pl.when((count_m[k] > MOE_TILE) & (s % 2 == 1))
        def _(k=k):
            overflow(stage0, ostage1, k)


def _experts(h2, slab_row, sw, cnt, off, wg, wu, wd):
    nb = h2.shape[0]
    t = h2.shape[1] // ROW_SLAB
    ne, d, ew = wg.shape
    n_list = slab_row.shape[1]
    assert h2.shape == (nb, t * ROW_SLAB, 128) and d == D_MODEL and ew == EXPERT_W and MOE_TILE % 16 == 0
    assert MOE_LIST_PAD >= MOE_WIN and ne % MOE_EPS == 0
    ns = ne // MOE_EPS
    ffn = lambda b, s, c, o: (jnp.clip(s - 1, 0, ns - 1), 0, 0)
    acc_rows = (t + 1) * ROW_SLAB
    grid_spec = pltpu.PrefetchScalarGridSpec(
        num_scalar_prefetch=2,
        grid=(nb, ns + 2),
        in_specs=[
            pl.BlockSpec((1, 1, n_list), lambda b, s, c, o: (b, 0, 0), memory_space=pltpu.SMEM),
            pl.BlockSpec((1, 1, n_list), lambda b, s, c, o: (b, 0, 0)),
            pl.BlockSpec((1, t * ROW_SLAB, 128), lambda b, s, c, o: (b, 0, 0), pipeline_mode=pl.Buffered(1)),
            pl.BlockSpec((MOE_EPS, d, ew), ffn),
            pl.BlockSpec((MOE_EPS, d, ew), ffn),
            pl.BlockSpec((MOE_EPS, ew, d), ffn),
        ],
        out_specs=pl.BlockSpec((1, acc_rows, 128), lambda b, s, c, o: (b, 0, 0)),
        scratch_shapes=[pltpu.VMEM((MOE_EPS * MOE_TILE * ROW_SLAB, 128), F32)] * 4,
    )
    return pl.pallas_call(
        functools.partial(_experts_kernel, ne=ne, spare=t * ROW_SLAB),
        grid_spec=grid_spec,
        out_shape=jax.ShapeDtypeStruct((nb, acc_rows, 128), F32),
        compiler_params=_params(("parallel", "arbitrary"), VMEM_LIMIT),
        name="experts",
    )(cnt.reshape(nb * ne), off.reshape(nb * ne), slab_row.reshape(nb, 1, n_list), sw.reshape(nb, 1, n_list),
      h2, wg, wu, wd)


def _moe_out_kernel(h_ref, r_ref, x_ref, mod_ref, sg_ref, su_ref, sd_ref, *rest):
    tm = x_ref.shape[1]
    h = _slab_load(h_ref, (0,), tm).astype(BF16)
    shared = _mm((_silu(_mm(h, sg_ref[...])) * _mm(h, su_ref[...])).astype(BF16), sd_ref[...])
    y = x_ref[0] + mod_ref[0, 5:6, :] * (_slab_load(r_ref, (0,), tm) + shared)
    if len(rest) == 2:
        y = y * lax.rsqrt(jnp.mean(y * y, axis=-1, keepdims=True) + EPS) * rest[0][...]
    rest[-1][0] = y


def _moe_out(h2, routed, xs, mod, sg, su, sd, tm, skip=0, final_g=None):
    nb, t, d = xs.shape
    tok = lambda b, j: (b, j + skip, 0)
    in_specs = [
        pl.BlockSpec((1, tm * ROW_SLAB, 128), tok),
        pl.BlockSpec((1, tm * ROW_SLAB, 128), tok),
        pl.BlockSpec((1, tm, d), tok),
        pl.BlockSpec((1, 6, d), (lambda b, j: (b, 0, 0)) if skip else _mod_row(nb)),
        pl.BlockSpec(sg.shape, lambda b, j: (0, 0)),
        pl.BlockSpec(su.shape, lambda b, j: (0, 0)),
        pl.BlockSpec(sd.shape, lambda b, j: (0, 0)),
    ]
    args = [h2, routed, xs, mod, sg, su, sd]
    if final_g is not None:
        in_specs.append(pl.BlockSpec((1, d), lambda b, j: (0, 0)))
        args.append(final_g.reshape(1, d))
    return pl.pallas_call(
        _moe_out_kernel,
        grid=(nb, t // tm - skip),
        in_specs=in_specs,
        out_specs=pl.BlockSpec((1, tm, d), lambda b, j: (b, j, 0)),
        out_shape=jax.ShapeDtypeStruct((nb, t - skip * tm, d), F32),
        compiler_params=_params(("parallel", "parallel")),
        name="moe_out",
    )(*args)


def kernel(x, c, ctx, c_ctx, w_ada, b_ada, norm_mix_g, norm_ffn_g, w_in, cm_ln_g, cm_ln_b, cm_w_s, cm_b_s,
           na_rpb, ret_decay_logit, w_branch, w_out, router_w, router_bias, exp_w_gate, exp_w_up,
           exp_w_down, sh_w_gate, sh_w_up, sh_w_down, final_g):
    nb, l, d = x.shape
    lc = ctx.shape[1]
    depth = w_ada.shape[0]
    assert d == D_MODEL and nb < MOD_ROWS and l % GRID_W == 0 and lc % CHUNK == 0 and l % CHUNK == 0
    tm = 256
    assert lc % tm == 0 and l % tm == 0

    cc = jnp.zeros((MOD_ROWS, d), F32).at[:nb].set(c).at[nb].set(c_ctx)
    mod_all = _adaln(cc, w_ada, b_ada).reshape(depth, MOD_ROWS, 6, d)
    cos_t, sin_t = _rope_tables(lc, l)
    xs = jnp.concatenate([ctx, x], axis=1)

    for layer in range(depth):
        mod = mod_all[layer]
        z = _inproj(xs, mod, norm_mix_g[layer], w_in[layer].astype(BF16), tm)
        a_lat = _gmlp(z, cm_ln_g[layer], cm_ln_b[layer], cm_w_s[layer].astype(BF16),
                      jnp.transpose(cm_b_s[layer]))
        b_lat = _nattn(z, _na_bias_table(na_rpb[layer], l // GRID_W), lc)
        ret = _retention(z, ret_decay_logit[layer], cos_t, sin_t, lc)
        xs = _merge(a_lat, b_lat, ret, z, xs, mod, w_branch[layer].astype(BF16),
                    w_out[layer].astype(BF16), tm)
        h2, eidx, wk = _router(xs, mod, norm_ffn_g[layer], jnp.transpose(router_w[layer]),
                               router_bias[layer], tm)
        slab_row, sw, cnt, off = _dispatch(eidx, wk)
        routed = _experts(h2, slab_row, sw, cnt, off, exp_w_gate[layer].astype(BF16),
                          exp_w_up[layer].astype(BF16), exp_w_down[layer].astype(BF16))
        last = layer == depth - 1
        xs = _moe_out(h2, routed, xs, mod, sh_w_gate[layer].astype(BF16), sh_w_up[layer].astype(BF16),
                      sh_w_down[layer].astype(BF16), tm, skip=lc // tm if last else 0,
                      final_g=final_g if last else None)
    return xs
```

```python
import functools

import jax
import jax.numpy as jnp
import numpy as np
from jax import lax
from jax.experimental import pallas as pl
from jax.experimental.pallas import tpu as pltpu

F32 = jnp.float32
BF16 = jnp.bfloat16

D_MODEL = 1024
GRID_W = 64
CHUNK = 128
CM_GROUPS = 4
CM_WIDTH = 512
NA_HEADS = 8
NA_HEAD_DIM = 64
NA_WIDTH = NA_HEADS * NA_HEAD_DIM
WIN_R = 8
WIN_C = 16
RET_HEADS = 4
RET_DK = 64
RET_DV = 128
RET_QK_W = RET_HEADS * RET_DK
RET_V_W = RET_HEADS * RET_DV
ROPE_BASE = 10000.0
IN_W = 2 * CM_WIDTH + 3 * NA_WIDTH + 2 * RET_QK_W + 2 * RET_V_W + 3 * D_MODEL
N_EXPERTS = 64
TOP_K = 8
N_GROUPS = 8
TOPK_GROUPS = 4
EXPERT_W = 256
ROUTED_SCALE = 2.5
EPS = 1e-6
NEG = -1e30

OFF_AU, OFF_AV = 0, 512
OFF_NQ, OFF_NK, OFF_NV = 1024, 1536, 2048
OFF_RQ, OFF_RK, OFF_RV, OFF_RG = 2560, 2816, 3072, 3584
OFF_GA, OFF_GB, OFF_GC = 4096, 5120, 6144

MOD_ROWS = 16
VMEM_LIMIT = 56 * 1024 * 1024

TOKEN_TILE = 256
ADALN_COLS = 1536
INPROJ_COLS = 512
GMLP_ROWS = 2 * CHUNK
NA_ROWS = 4
MOE_TILE = 320
MOE_GROUP = 8
MOE_EPS = 1
MOE_WIN = (MOE_TILE + 127) // 128 * 128 + 128
MOE_LIST_PAD = 1024
CAST_BLOCK_BYTES = 8 * 1024 * 1024


def _params(sem, vmem=None):
    return pltpu.CompilerParams(dimension_semantics=sem, vmem_limit_bytes=vmem)


def _silu(v):
    return v * jax.nn.sigmoid(v)


def _mm(a, b):
    return jnp.dot(a, b, preferred_element_type=F32)


def _mm_nt(a, b):
    return lax.dot_general(a, b, (((1,), (1,)), ((), ())), preferred_element_type=F32)


def _mm_tn(a, b):
    return lax.dot_general(a, b, (((0,), (0,)), ((), ())), preferred_element_type=F32)


def _cast_kernel(x_ref, o_ref):
    o_ref[...] = x_ref[...].astype(BF16)


def _to_bf16(w):
    cols = w.shape[-1]
    flat = w.reshape(-1, cols)
    rows = CAST_BLOCK_BYTES // (4 * cols)
    assert flat.shape[0] % rows == 0
    out = pl.pallas_call(
        _cast_kernel,
        grid=(flat.shape[0] // rows,),
        in_specs=[pl.BlockSpec((rows, cols), lambda i: (i, 0))],
        out_specs=pl.BlockSpec((rows, cols), lambda i: (i, 0)),
        out_shape=jax.ShapeDtypeStruct(flat.shape, BF16),
        compiler_params=_params(("parallel",), VMEM_LIMIT),
        name="to_bf16",
    )(flat)
    return out.reshape(w.shape)


def _mod_row(nb):
    return lambda b, j, *_: (jnp.where(j == 0, nb, b), 0, 0)


def _adaln_kernel(c_ref, w_ref, b_ref, o_ref):
    s = _silu(c_ref[...])
    o_ref[0] = _mm(s, w_ref[0]) + b_ref[0]


def _adaln(cc, w_ada, b_ada):
    depth, d, n6 = w_ada.shape
    tn = ADALN_COLS
    return pl.pallas_call(
        _adaln_kernel,
        grid=(depth, n6 // tn),
        in_specs=[
            pl.BlockSpec((MOD_ROWS, d), lambda l, n: (0, 0)),
            pl.BlockSpec((1, d, tn), lambda l, n: (l, 0, n)),
            pl.BlockSpec((1, 1, tn), lambda l, n: (l, 0, n)),
        ],
        out_specs=pl.BlockSpec((1, MOD_ROWS, tn), lambda l, n: (l, 0, n)),
        out_shape=jax.ShapeDtypeStruct((depth, MOD_ROWS, n6), F32),
        compiler_params=_params(("parallel", "parallel"), VMEM_LIMIT),
        name="adaln",
    )(cc, w_ada, b_ada.reshape(depth, 1, n6))


def _modulated_norm(x, g, shift, scale):
    y = x * lax.rsqrt(jnp.mean(x * x, axis=-1, keepdims=True) + EPS) * g
    return y * (1.0 + scale) + shift


def _inproj_kernel(x_ref, mod_ref, g_ref, w_ref, z_ref):
    h = _modulated_norm(x_ref[0], g_ref[...], mod_ref[0, 0:1, :], mod_ref[0, 1:2, :])
    hb = h.astype(BF16)
    nw = INPROJ_COLS
    for n in range(IN_W // nw):
        z_ref[0, :, n * nw:(n + 1) * nw] = _mm(hb, w_ref[:, n * nw:(n + 1) * nw]).astype(BF16)


def _inproj(xs, mod, g, w_in_b, tm):
    nb, t, d = xs.shape
    return pl.pallas_call(
        _inproj_kernel,
        grid=(nb, t // tm),
        in_specs=[
            pl.BlockSpec((1, tm, d), lambda b, j: (b, j, 0)),
            pl.BlockSpec((1, 6, d), _mod_row(nb)),
            pl.BlockSpec((1, d), lambda b, j: (0, 0)),
            pl.BlockSpec((d, IN_W), lambda b, j: (0, 0)),
        ],
        out_specs=pl.BlockSpec((1, tm, IN_W), lambda b, j: (b, j, 0)),
        out_shape=jax.ShapeDtypeStruct((nb, t, IN_W), BF16),
        compiler_params=_params(("parallel", "parallel"), VMEM_LIMIT),
        name="inproj",
    )(xs, mod, g.reshape(1, d), w_in_b)


def _gmlp_kernel(u_ref, v_ref, lng_ref, lnb_ref, ws_ref, bs_ref, o_ref):
    u = jax.nn.gelu(u_ref[0].astype(F32))
    v = jax.nn.gelu(v_ref[0].astype(F32))
    vc = v - jnp.mean(v, axis=-1, keepdims=True)
    vn = vc * lax.rsqrt(jnp.mean(vc * vc, axis=-1, keepdims=True) + EPS) * lng_ref[...] + lnb_ref[...]
    vb = vn.astype(BF16)
    gw = CM_WIDTH // CM_GROUPS
    for c in range(u.shape[0] // CHUNK):
        rs = slice(c * CHUNK, (c + 1) * CHUNK)
        for g in range(CM_GROUPS):
            gs = slice(g * gw, (g + 1) * gw)
            s = _mm(ws_ref[g], vb[rs, gs]) + bs_ref[:, g:g + 1]
            o_ref[0, rs, gs] = (u[rs, gs] * s).astype(BF16)


def _gmlp(z, ln_g, ln_b, ws_b, bs_t):
    nb, t, _ = z.shape
    assert t % GMLP_ROWS == 0
    return pl.pallas_call(
        _gmlp_kernel,
        grid=(nb, t // GMLP_ROWS),
        in_specs=[
            pl.BlockSpec((1, GMLP_ROWS, CM_WIDTH), lambda b, j: (b, j, OFF_AU // CM_WIDTH)),
            pl.BlockSpec((1, GMLP_ROWS, CM_WIDTH), lambda b, j: (b, j, OFF_AV // CM_WIDTH)),
            pl.BlockSpec((1, CM_WIDTH), lambda b, j: (0, 0)),
            pl.BlockSpec((1, CM_WIDTH), lambda b, j: (0, 0)),
            pl.BlockSpec((CM_GROUPS, CHUNK, CHUNK), lambda b, j: (0, 0, 0)),
            pl.BlockSpec((CHUNK, CM_GROUPS), lambda b, j: (0, 0)),
        ],
        out_specs=pl.BlockSpec((1, GMLP_ROWS, CM_WIDTH), lambda b, j: (b, j, 0)),
        out_shape=jax.ShapeDtypeStruct((nb, t, CM_WIDTH), BF16),
        compiler_params=_params(("parallel", "parallel")),
        name="gmlp",
    )(z, z, ln_g.reshape(1, -1), ln_b.reshape(1, -1), ws_b, bs_t)


def _na_row_classes(rows):
    half = WIN_R // 2
    wr = min(WIN_R, rows)
    reps = list(range(half)) + [half] + list(range(rows - (wr - half) + 1, rows))
    cls = np.zeros((rows,), np.int32)
    for r in range(rows):
        if r < half:
            cls[r] = r
        elif r <= rows - (wr - half):
            cls[r] = half
        else:
            cls[r] = half + 1 + (r - (rows - (wr - half) + 1))
    return reps, cls


def _na_bias_table(rpb, rows):
    wr = min(WIN_R, rows)
    reps, _ = _na_row_classes(rows)
    nh = rpb.shape[0]
    qc = np.arange(GRID_W)[:, None]
    kc = np.arange(GRID_W)[None, :]
    col_start = np.clip(qc - WIN_C // 2, 0, GRID_W - WIN_C)
    valid = (kc >= col_start) & (kc < col_start + WIN_C)
    pad = GRID_W - WIN_C
    edge = jnp.concatenate([jnp.repeat(rpb[..., :1], pad, axis=-1), rpb.astype(F32),
                            jnp.repeat(rpb[..., -1:], pad, axis=-1)], axis=-1)
    toep = jnp.stack([edge[..., GRID_W - 1 - q:2 * GRID_W - 1 - q] for q in range(GRID_W)], axis=1)
    toep = jnp.where(valid[None, :, None, :], toep, NEG)
    tabs = []
    for r in reps:
        r0 = int(np.clip(r - WIN_R // 2, 0, rows - wr))
        lo = r0 - r + WIN_R - 1
        tabs.append(toep[:, :, lo:lo + wr, :].reshape(nh, GRID_W, wr * GRID_W))
    tab = jnp.stack(tabs)
    tab = tab.reshape(len(reps), nh // 2, 2, GRID_W, wr * GRID_W)
    return jnp.transpose(tab, (0, 1, 4, 2, 3)).reshape(len(reps), nh // 2, wr * GRID_W, 2 * GRID_W)


def _t_blocks(p):
    return jnp.concatenate([jnp.transpose(p[i:i + 128]) for i in range(0, p.shape[0], 128)],
                           axis=1).astype(BF16)


def _na_kernel(cls_ref, q_ref, k_ref, v_ref, *refs, lc, rows):
    del cls_ref
    j = pl.program_id(1)
    n_ctx_blk = lc // GRID_W
    wr = min(WIN_R, rows)
    sub = lax.broadcasted_iota(jnp.int32, (128, 128), 0)
    lane = lax.broadcasted_iota(jnp.int32, (128, 128), 1)
    same_head = (sub // 64) == (lane // 64)
    first_head = lax.broadcasted_iota(jnp.int32, (GRID_W, 128), 1) < 64
    scale = NA_HEAD_DIM ** -0.5
    bias_refs, o_ref = refs[:-1], refs[-1]

    def attend(local, rr):
        qs = slice(rr * GRID_W, (rr + 1) * GRID_W)
        bias_ref = bias_refs[rr]
        q = q_ref[0, qs, :]
        if local:
            r0 = jnp.clip(j * NA_ROWS + rr - n_ctx_blk - WIN_R // 2, 0, rows - wr)
            start = pl.multiple_of(lc + r0 * GRID_W, GRID_W)
            k_loc = k_ref[0, pl.ds(start, wr * GRID_W), :]
            v_loc = v_ref[0, pl.ds(start, wr * GRID_W), :]
        k_ctx = k_ref[0, 0:lc, :]
        v_ctx = v_ref[0, 0:lc, :]
        for p in range(NA_HEADS // 2):
            cs = slice(p * 128, (p + 1) * 128)
            q2 = q[:, cs].astype(F32) * scale
            qbd = jnp.transpose(jnp.where(same_head, jnp.concatenate([q2, q2], axis=0), 0.0)).astype(BF16)
            s_ctx = _mm(k_ctx[:, cs], qbd)
            m = jnp.max(s_ctx, axis=0, keepdims=True)
            if local:
                s_loc = _mm(k_loc[:, cs], qbd) + bias_ref[0, p]
                m = jnp.maximum(m, jnp.max(s_loc, axis=0, keepdims=True))
            p_ctx = jnp.exp(s_ctx - m)
            den = jnp.sum(p_ctx, axis=0, keepdims=True)
            if local:
                p_loc = jnp.exp(s_loc - m)
                den = den + jnp.sum(p_loc, axis=0, keepdims=True)
            inv = 1.0 / den
            o = _mm(_t_blocks(p_ctx * inv), v_ctx[:, cs])
            if local:
                o = o + _mm(_t_blocks(p_loc * inv), v_loc[:, cs])
            o_ref[0, qs, cs] = jnp.where(first_head, o[:GRID_W], o[GRID_W:]).astype(BF16)

    @pl.when(j * NA_ROWS < n_ctx_blk)
    def _():
        for rr in range(NA_ROWS):
            attend(False, rr)

    @pl.when(j * NA_ROWS >= n_ctx_blk)
    def _():
        for rr in range(NA_ROWS):
            attend(True, rr)


def _nattn(z, bias_tab, lc):
    nb, t, _ = z.shape
    rows = (t - lc) // GRID_W
    n_ctx_blk = lc // GRID_W
    _, cls = _na_row_classes(rows)
    cls_arr = jnp.asarray(np.concatenate([np.zeros((n_ctx_blk,), np.int32), cls]))
    wr = min(WIN_R, rows)
    qb = NA_ROWS * GRID_W
    assert n_ctx_blk % NA_ROWS == 0 and rows % NA_ROWS == 0

    def bias_spec(rr):
        return pl.BlockSpec((1, NA_HEADS // 2, wr * GRID_W, 2 * GRID_W),
                            lambda b, j, c: (c[j * NA_ROWS + rr], 0, 0, 0))

    grid_spec = pltpu.PrefetchScalarGridSpec(
        num_scalar_prefetch=1,
        grid=(nb, t // qb),
        in_specs=[
            pl.BlockSpec((1, qb, NA_WIDTH), lambda b, j, c: (b, j, OFF_NQ // NA_WIDTH)),
            pl.BlockSpec((1, t, NA_WIDTH), lambda b, j, c: (b, 0, OFF_NK // NA_WIDTH)),
            pl.BlockSpec((1, t, NA_WIDTH), lambda b, j, c: (b, 0, OFF_NV // NA_WIDTH)),
        ] + [bias_spec(rr) for rr in range(NA_ROWS)],
        out_specs=pl.BlockSpec((1, qb, NA_WIDTH), lambda b, j, c: (b, j, 0)),
    )
    return pl.pallas_call(
        functools.partial(_na_kernel, lc=lc, rows=rows),
        grid_spec=grid_spec,
        out_shape=jax.ShapeDtypeStruct((nb, t, NA_WIDTH), BF16),
        compiler_params=_params(("parallel", "arbitrary"), VMEM_LIMIT),
        name="nattn",
    )(cls_arr, z, z, z, *([bias_tab] * NA_ROWS))


def _rope_tables(lc, l):
    half = RET_DK // 2
    nf = half // 2
    tpos = np.arange(l)
    inv = ROPE_BASE ** (-jnp.arange(nf, dtype=F32) / nf)

    def part(pos):
        ang = jnp.asarray(pos, F32)[:, None] * inv[None, :]
        c, s = jnp.cos(ang), jnp.sin(ang)
        return jnp.concatenate([c, c], -1), jnp.concatenate([-s, s], -1)

    c_r, s_r = part(tpos // GRID_W)
    c_c, s_c = part(tpos % GRID_W)
    cos = jnp.concatenate([c_r, c_c], -1)
    sin = jnp.concatenate([s_r, s_c], -1)
    cos = jnp.concatenate([jnp.ones((lc, RET_DK), F32), cos], 0)
    sin = jnp.concatenate([jnp.zeros((lc, RET_DK), F32), sin], 0)
    return jnp.tile(cos, (1, RET_HEADS)), jnp.tile(sin, (1, RET_HEADS))


def _ret_kernel(dl_ref, *refs):
    st_ref = refs[-1]

    @pl.when(pl.program_id(1) == 0)
    def _():
        st_ref[...] = jnp.zeros_like(st_ref)

    for d in range(2):
        _ret_direction(d, dl_ref, *refs[5 * d:5 * d + 5], refs[10 + d], st_ref)


def _ret_direction(d, dl_ref, q_ref, k_ref, v_ref, cos_ref, sin_ref, o_ref, st_ref):
    cos = cos_ref[...]
    sin = sin_ref[...]
    lane_qk = lax.broadcasted_iota(jnp.int32, (CHUNK, RET_QK_W), 1)
    first = (lane_qk % (RET_DK // 2)) < (RET_DK // 4)

    def rope(a):
        a = a.astype(F32)
        partner = jnp.where(first, pltpu.roll(a, RET_QK_W - RET_DK // 4, axis=1),
                            pltpu.roll(a, RET_DK // 4, axis=1))
        return a * cos + partner * sin

    q = rope(q_ref[0])
    k = rope(k_ref[0]) * (RET_DK ** -0.5)
    v = v_ref[0]

    ri = lax.broadcasted_iota(jnp.int32, (CHUNK, CHUNK), 0)
    ci = lax.broadcasted_iota(jnp.int32, (CHUNK, CHUNK), 1)
    dist = (ri - ci) if d == 0 else (ci - ri)
    keep = dist >= d
    dist_f = jnp.maximum(dist, 0).astype(F32)
    q_exp = ((ri + 1) if d == 0 else (CHUNK - ri)).astype(F32)
    k_exp = ((CHUNK - 1 - ri) if d == 0 else ri).astype(F32)
    lane_half = ci >= RET_DK
    row_half = ri >= RET_DK

    def log_gamma(h):
        xv = jnp.full((CHUNK, CHUNK), dl_ref[d, h], F32)
        return jnp.minimum(xv, 0.0) - jnp.log1p(jnp.exp(-jnp.abs(xv)))

    for p in range(RET_HEADS // 2):
        cs = slice(p * 128, (p + 1) * 128)
        lg = [log_gamma(2 * p), log_gamma(2 * p + 1)]
        q2 = q[:, cs]
        k2 = k[:, cs]
        k2b = k2.astype(BF16)
        kd_t = jnp.transpose(k2 * jnp.exp(k_exp * jnp.where(lane_half, lg[1], lg[0])))
        s2 = st_ref[d, p]
        s2b = s2.astype(BF16)
        upd = s2 * jnp.exp(float(CHUNK) * jnp.where(row_half, lg[1], lg[0]))
        for hh in range(2):
            h = 2 * p + hh
            vh = v[:, h * RET_DV:(h + 1) * RET_DV]
            sel = lane_half if hh else jnp.logical_not(lane_half)
            qm = jnp.where(sel, q2, 0.0).astype(BF16)
            a = _mm_nt(qm, k2b)
            a = jnp.where(keep, a * jnp.exp(dist_f * lg[hh]), 0.0)
            inner = _mm(a.astype(BF16), vh)
            cross = _mm(qm, s2b) * jnp.exp(q_exp * lg[hh])
            o_ref[0, :, h * RET_DV:(h + 1) * RET_DV] = (inner + cross).astype(o_ref.dtype)
            rsel = row_half if hh else jnp.logical_not(row_half)
            upd = upd + _mm(jnp.where(rsel, kd_t, 0.0).astype(BF16), vh)
        st_ref[d, p] = upd


def _retention(z, decay_logit, cos_t, sin_t, lc):
    nb, t, _ = z.shape
    nc = t // CHUNK
    ncc = lc // CHUNK

    def chunk(d, s):
        return s if d == 0 else jnp.where(s < ncc, ncc - 1 - s, nc - 1 - (s - ncc))

    def in_specs(d):
        return [
            pl.BlockSpec((1, CHUNK, RET_QK_W), lambda b, s, dl: (b, chunk(d, s), OFF_RQ // RET_QK_W)),
            pl.BlockSpec((1, CHUNK, RET_QK_W), lambda b, s, dl: (b, chunk(d, s), OFF_RK // RET_QK_W)),
            pl.BlockSpec((1, CHUNK, RET_V_W), lambda b, s, dl: (b, chunk(d, s), OFF_RV // RET_V_W)),
            pl.BlockSpec((CHUNK, RET_QK_W), lambda b, s, dl: (chunk(d, s), 0)),
            pl.BlockSpec((CHUNK, RET_QK_W), lambda b, s, dl: (chunk(d, s), 0)),
        ]

    def out_spec(d):
        return pl.BlockSpec((1, CHUNK, RET_V_W), lambda b, s, dl: (b, chunk(d, s), 0))

    grid_spec = pltpu.PrefetchScalarGridSpec(
        num_scalar_prefetch=1,
        grid=(nb, nc),
        in_specs=in_specs(0) + in_specs(1),
        out_specs=[out_spec(0), out_spec(1)],
        scratch_shapes=[pltpu.VMEM((2, RET_HEADS // 2, 2 * RET_DK, RET_DV), F32)],
    )
    return pl.pallas_call(
        _ret_kernel,
        grid_spec=grid_spec,
        out_shape=[jax.ShapeDtypeStruct((nb, t, RET_V_W), BF16)] * 2,
        compiler_params=_params(("parallel", "arbitrary")),
        name="retention",
    )(decay_logit, *([z, z, z, cos_t, sin_t] * 2))


def _merge_kernel(a_ref, b_ref, rf_ref, rb_ref, rg_ref, ga_ref, gb_ref, gc_ref, x_ref, mod_ref,
                  wb_ref, wo_ref, o_ref):
    o = rf_ref[0].astype(F32) + rb_ref[0].astype(F32)
    parts = []
    for h in range(RET_HEADS):
        oh = o[:, h * RET_DV:(h + 1) * RET_DV]
        parts.append(oh * lax.rsqrt(jnp.mean(oh * oh, axis=-1, keepdims=True) + EPS))
    c = jnp.concatenate(parts, axis=-1) * _silu(rg_ref[0].astype(F32))
    m = (jax.nn.sigmoid(ga_ref[0].astype(F32)) * _mm(a_ref[0], wb_ref[0])
         + jax.nn.sigmoid(gb_ref[0].astype(F32)) * _mm(b_ref[0], wb_ref[1])
         + jax.nn.sigmoid(gc_ref[0].astype(F32)) * _mm(c.astype(BF16), wb_ref[2]))
    mix = _mm(m.astype(BF16), wo_ref[...])
    o_ref[0] = x_ref[0] + mod_ref[0, 2:3, :] * mix


def _merge(a_lat, b_lat, ret, z, xs, mod, wb_b, wo_b, tm):
    nb, t, d = xs.shape
    bw = CM_WIDTH
    return pl.pallas_call(
        _merge_kernel,
        grid=(nb, t // tm),
        in_specs=[
            pl.BlockSpec((1, tm, bw), lambda b, j: (b, j, 0)),
            pl.BlockSpec((1, tm, bw), lambda b, j: (b, j, 0)),
            pl.BlockSpec((1, tm, bw), lambda b, j: (b, j, 0)),
            pl.BlockSpec((1, tm, bw), lambda b, j: (b, j, 0)),
            pl.BlockSpec((1, tm, RET_V_W), lambda b, j: (b, j, OFF_RG // RET_V_W)),
            pl.BlockSpec((1, tm, d), lambda b, j: (b, j, OFF_GA // d)),
            pl.BlockSpec((1, tm, d), lambda b, j: (b, j, OFF_GB // d)),
            pl.BlockSpec((1, tm, d), lambda b, j: (b, j, OFF_GC // d)),
            pl.BlockSpec((1, tm, d), lambda b, j: (b, j, 0)),
            pl.BlockSpec((1, 6, d), _mod_row(nb)),
            pl.BlockSpec((3, bw, d), lambda b, j: (0, 0, 0)),
            pl.BlockSpec((d, d), lambda b, j: (0, 0)),
        ],
        out_specs=pl.BlockSpec((1, tm, d), lambda b, j: (b, j, 0)),
        out_shape=jax.ShapeDtypeStruct((nb, t, d), F32),
        compiler_params=_params(("parallel", "parallel"), VMEM_LIMIT),
        name="merge",
    )(a_lat, b_lat, ret[0], ret[1], z, z, z, z, xs, mod, wb_b, wo_b)


ROW_SLAB = D_MODEL // 128


def _slab_load(ref, lead, n, row0=0):
    return jnp.concatenate(
        [ref[lead + (pl.ds(row0 * ROW_SLAB + j, n, stride=ROW_SLAB), slice(None))] for j in range(ROW_SLAB)],
        axis=-1)


def _slab_store(ref, lead, val, row0=0):
    for j in range(ROW_SLAB):
        ref[lead + (pl.ds(row0 * ROW_SLAB + j, val.shape[0], stride=ROW_SLAB), slice(None))] = (
            val[:, j * 128:(j + 1) * 128])


def _first_index(mask, idx, big):
    return jnp.min(jnp.where(mask, idx, big), axis=0, keepdims=True)


def _router_kernel(x_ref, mod_ref, g_ref, rwt_ref, rb_ref, h_ref, ei_ref, wk_ref):
    h = _modulated_norm(x_ref[0], g_ref[...], mod_ref[0, 3:4, :], mod_ref[0, 4:5, :])
    _slab_store(h_ref, (0,), h)
    tm = h.shape[0]
    logits = _mm_nt(rwt_ref[...], h)
    scores = jax.nn.sigmoid(logits)
    biased = scores + rb_ref[:, 0:1]
    gsz = N_EXPERTS // N_GROUPS
    sub = lax.broadcasted_iota(jnp.int32, (gsz, tm), 0)
    gscore = []
    for g in range(N_GROUPS):
        blk = biased[g * gsz:(g + 1) * gsz]
        m1 = jnp.max(blk, axis=0, keepdims=True)
        i1 = _first_index(blk == m1, sub, gsz)
        m2 = jnp.max(jnp.where(sub == i1, -jnp.inf, blk), axis=0, keepdims=True)
        gscore.append(m1 + m2)
    gs = jnp.concatenate(gscore, axis=0)
    gidx = lax.broadcasted_iota(jnp.int32, (N_GROUPS, tm), 0)
    gsel = jnp.zeros((N_GROUPS, tm), jnp.int32)
    for _ in range(TOPK_GROUPS):
        m = jnp.max(gs, axis=0, keepdims=True)
        pick = gidx == _first_index(gs == m, gidx, N_GROUPS)
        gsel = jnp.where(pick, 1, gsel)
        gs = jnp.where(pick, -jnp.inf, gs)
    cand = jnp.concatenate(
        [jnp.where(gsel[g:g + 1] > 0, biased[g * gsz:(g + 1) * gsz], -jnp.inf) for g in range(N_GROUPS)],
        axis=0)
    eidx = lax.broadcasted_iota(jnp.int32, (N_EXPERTS, tm), 0)
    picks, ws = [], []
    for _ in range(TOP_K):
        m = jnp.max(cand, axis=0, keepdims=True)
        first = _first_index(cand == m, eidx, N_EXPERTS)
        pick = eidx == first
        picks.append(first)
        ws.append(jnp.sum(jnp.where(pick, scores, 0.0), axis=0, keepdims=True))
        cand = jnp.where(pick, -jnp.inf, cand)
    w = jnp.concatenate(ws, axis=0)
    ei_ref[0] = jnp.concatenate(picks, axis=0)
    wk_ref[0] = w / jnp.sum(w, axis=0, keepdims=True) * ROUTED_SCALE


def _router(xs, mod, g, rw_t, rb, tm):
    nb, t, d = xs.shape
    return pl.pallas_call(
        _router_kernel,
        grid=(nb, t // tm),
        in_specs=[
            pl.BlockSpec((1, tm, d), lambda b, j: (b, j, 0)),
            pl.BlockSpec((1, 6, d), _mod_row(nb)),
            pl.BlockSpec((1, d), lambda b, j: (0, 0)),
            pl.BlockSpec((N_EXPERTS, d), lambda b, j: (0, 0)),
            pl.BlockSpec((N_EXPERTS, 1), lambda b, j: (0, 0)),
        ],
        out_specs=[
            pl.BlockSpec((1, tm * ROW_SLAB, 128), lambda b, j: (b, j, 0)),
            pl.BlockSpec((1, TOP_K, tm), lambda b, j: (b, 0, j)),
            pl.BlockSpec((1, TOP_K, tm), lambda b, j: (b, 0, j)),
        ],
        out_shape=[jax.ShapeDtypeStruct((nb, t * ROW_SLAB, 128), F32),
                   jax.ShapeDtypeStruct((nb, TOP_K, t), jnp.int32), jax.ShapeDtypeStruct((nb, TOP_K, t), F32)],
        compiler_params=_params(("parallel", "parallel")),
        name="router",
    )(xs, mod, g.reshape(1, d), rw_t, rb.reshape(N_EXPERTS, 1))


def _dispatch(eidx, wk):
    nb, k, t = eidx.shape
    pos = lax.broadcasted_iota(jnp.int32, eidx.shape, 2)
    skey, sw = lax.sort(((eidx * t + pos).reshape(nb, k * t), wk.reshape(nb, k * t)), dimension=1, num_keys=1)
    slab_row = (skey % t) * ROW_SLAB
    cnt = jnp.sum(eidx[:, None] == jnp.arange(N_EXPERTS, dtype=jnp.int32)[None, :, None, None],
                  axis=(2, 3), dtype=jnp.int32)
    off = jnp.cumsum(cnt, axis=1) - cnt
    pad = ((0, 0), (0, MOE_LIST_PAD))
    return jnp.pad(slab_row, pad), jnp.pad(sw, pad), cnt, off


def _slab(row):
    return pl.ds(pl.multiple_of(row, ROW_SLAB), ROW_SLAB)


def _expert_ffn(base, w_ref, stage_ref, ostage_ref, wg_ref, wu_ref, wd_ref, k):
    x = _slab_load(stage_ref, (), MOE_TILE, k * MOE_TILE).astype(BF16)
    act = _silu(_mm(x, wg_ref[k])) * _mm(x, wu_ref[k])
    aligned = pl.multiple_of((base // 128) * 128, 128)
    wrow = pltpu.roll(w_ref[0, :, pl.ds(aligned, MOE_WIN)], (MOE_WIN - (base - aligned)) % MOE_WIN, axis=1)
    wcol = jnp.concatenate(
        [jnp.transpose(jnp.broadcast_to(wrow[:, k * 128:(k + 1) * 128], (128, 128)))
         for k in range(MOE_WIN // 128 - 1)], axis=0)[:MOE_TILE]
    act = act * jnp.concatenate([wcol] * (EXPERT_W // 128), axis=-1)
    _slab_store(ostage_ref, (), _mm(act.astype(BF16), wd_ref[k]), k * MOE_TILE)


def _experts_kernel(cnt_ref, off_ref, idx_ref, w_ref, src_ref, wg_ref, wu_ref, wd_ref, acc_ref,
                    stage0, stage1, ostage0, ostage1, *, ne, spare):
    b = pl.program_id(0)
    s = pl.program_id(1)
    ns = ne // MOE_EPS

    def expert(stage, k):
        return b * ne + stage * MOE_EPS + k

    st_g = jnp.minimum(s, ns - 1)
    st_m = jnp.clip(s - 1, 0, ns - 1)
    st_s = jnp.maximum(s - 2, 0)
    ffn_live = (s >= 1) & (s <= ns)
    start_g = [off_ref[expert(st_g, k)] for k in range(MOE_EPS)]
    start_m = [off_ref[expert(st_m, k)] for k in range(MOE_EPS)]
    count_m = [jnp.where(ffn_live, cnt_ref[expert(st_m, k)], 0) for k in range(MOE_EPS)]
    start_s = [off_ref[expert(st_s, k)] for k in range(MOE_EPS)]
    count_s = [jnp.where(s >= 2, jnp.minimum(cnt_ref[expert(st_s, k)], MOE_TILE), 0) for k in range(MOE_EPS)]

    @pl.when(s == 0)
    def _():
        acc_ref[...] = jnp.zeros_like(acc_ref)
        stage1[...] = jnp.zeros_like(stage1)
        ostage0[...] = jnp.zeros_like(ostage0)

    def step(stage_w, stage_r, ostage_w, ostage_r):
        for k in range(MOE_EPS):
            for r in range(MOE_TILE):
                stage_w[_slab((k * MOE_TILE + r) * ROW_SLAB), :] = (
                    src_ref[0, _slab(idx_ref[0, 0, start_g[k] + r]), :])
        for k in range(MOE_EPS):
            _expert_ffn(start_m[k], w_ref, stage_r, ostage_w, wg_ref, wu_ref, wd_ref, k)
        for k in range(MOE_EPS):
            for g in range(MOE_TILE // MOE_GROUP):
                rws = range(g * MOE_GROUP, (g + 1) * MOE_GROUP)
                dst = [_slab(jnp.where(r < count_s[k], idx_ref[0, 0, start_s[k] + r], spare)) for r in rws]
                vals = [acc_ref[0, d, :] + ostage_r[_slab((k * MOE_TILE + r) * ROW_SLAB), :]
                        for d, r in zip(dst, rws)]
                for d, v in zip(dst, vals):
                    acc_ref[0, d, :] = v

    @pl.when(s % 2 == 0)
    def _():
        step(stage0, stage1, ostage1, ostage0)

    @pl.when(s % 2 == 1)
    def _():
        step(stage1, stage0, ostage0, ostage1)

    def overflow(stage_x, ostage_x, k):
        def tile(ti, carry):
            base = start_m[k] + ti * MOE_TILE
            rows = jnp.minimum(count_m[k] - ti * MOE_TILE, MOE_TILE)

            def gather(r, c):
                stage_x[_slab((k * MOE_TILE + r) * ROW_SLAB), :] = src_ref[0, _slab(idx_ref[0, 0, base + r]), :]
                return c

            def scatter(r, c):
                d = _slab(idx_ref[0, 0, base + r])
                acc_ref[0, d, :] = acc_ref[0, d, :] + ostage_x[_slab((k * MOE_TILE + r) * ROW_SLAB), :]
                return c

            lax.fori_loop(0, rows, gather, 0)
            _expert_ffn(base, w_ref, stage_x, ostage_x, wg_ref, wu_ref, wd_ref, k)
            lax.fori_loop(0, rows, scatter, 0)
            return carry

        lax.fori_loop(1, (count_m[k] + MOE_TILE - 1) // MOE_TILE, tile, 0)

    for k in range(MOE_EPS):
        @pl.when((count_m[k] > MOE_TILE) & (s % 2 == 0))
        def _(k=k):
            overflow(stage1, ostage0, k)

        @pl.when((count_m[k] > MOE_TILE) & (s % 2 == 1))
        def _(k=k):
            overflow(stage0, ostage1, k)


def _experts(h2, slab_row, sw, cnt, off, wg, wu, wd, layer):
    nb = h2.shape[0]
    t = h2.shape[1] // ROW_SLAB
    ne = cnt.shape[1]
    _, d, ew = wg.shape
    n_list = slab_row.shape[1]
    assert h2.shape == (nb, t * ROW_SLAB, 128) and d == D_MODEL and ew == EXPERT_W and MOE_TILE % 16 == 0
    assert MOE_LIST_PAD >= MOE_WIN and ne % MOE_EPS == 0
    ns = ne // MOE_EPS
    ffn = lambda b, s, c, o: (layer * ns + jnp.clip(s - 1, 0, ns - 1), 0, 0)
    acc_rows = (t + 1) * ROW_SLAB
    grid_spec = pltpu.PrefetchScalarGridSpec(
        num_scalar_prefetch=2,
        grid=(nb, ns + 2),
        in_specs=[
            pl.BlockSpec((1, 1, n_list), lambda b, s, c, o: (b, 0, 0), memory_space=pltpu.SMEM),
            pl.BlockSpec((1, 1, n_list), lambda b, s, c, o: (b, 0, 0)),
            pl.BlockSpec((1, t * ROW_SLAB, 128), lambda b, s, c, o: (b, 0, 0)),
            pl.BlockSpec((MOE_EPS, d, ew), ffn),
            pl.BlockSpec((MOE_EPS, d, ew), ffn),
            pl.BlockSpec((MOE_EPS, ew, d), ffn),
        ],
        out_specs=pl.BlockSpec((1, acc_rows, 128), lambda b, s, c, o: (b, 0, 0)),
        scratch_shapes=[pltpu.VMEM((MOE_EPS * MOE_TILE * ROW_SLAB, 128), F32)] * 4,
    )
    return pl.pallas_call(
        functools.partial(_experts_kernel, ne=ne, spare=t * ROW_SLAB),
        grid_spec=grid_spec,
        out_shape=jax.ShapeDtypeStruct((nb, acc_rows, 128), F32),
        compiler_params=_params(("parallel", "arbitrary"), VMEM_LIMIT),
        name="experts",
    )(cnt.reshape(nb * ne), off.reshape(nb * ne), slab_row.reshape(nb, 1, n_list), sw.reshape(nb, 1, n_list),
      h2, wg, wu, wd)


def _moe_out_kernel(h_ref, r_ref, x_ref, mod_ref, sg_ref, su_ref, sd_ref, *rest):
    tm = x_ref.shape[1]
    h = _slab_load(h_ref, (0,), tm).astype(BF16)
    shared = _mm((_silu(_mm(h, sg_ref[...])) * _mm(h, su_ref[...])).astype(BF16), sd_ref[...])
    y = x_ref[0] + mod_ref[0, 5:6, :] * (_slab_load(r_ref, (0,), tm) + shared)
    if len(rest) == 2:
        y = y * lax.rsqrt(jnp.mean(y * y, axis=-1, keepdims=True) + EPS) * rest[0][...]
    rest[-1][0] = y


def _moe_out(h2, routed, xs, mod, sg, su, sd, tm, skip=0, final_g=None):
    nb, t, d = xs.shape
    tok = lambda b, j: (b, j + skip, 0)
    in_specs = [
        pl.BlockSpec((1, tm * ROW_SLAB, 128), tok),
        pl.BlockSpec((1, tm * ROW_SLAB, 128), tok),
        pl.BlockSpec((1, tm, d), tok),
        pl.BlockSpec((1, 6, d), (lambda b, j: (b, 0, 0)) if skip else _mod_row(nb)),
        pl.BlockSpec(sg.shape, lambda b, j: (0, 0)),
        pl.BlockSpec(su.shape, lambda b, j: (0, 0)),
        pl.BlockSpec(sd.shape, lambda b, j: (0, 0)),
    ]
    args = [h2, routed, xs, mod, sg, su, sd]
    if final_g is not None:
        in_specs.append(pl.BlockSpec((1, d), lambda b, j: (0, 0)))
        args.append(final_g.reshape(1, d))
    return pl.pallas_call(
        _moe_out_kernel,
        grid=(nb, t // tm - skip),
        in_specs=in_specs,
        out_specs=pl.BlockSpec((1, tm, d), lambda b, j: (b, j, 0)),
        out_shape=jax.ShapeDtypeStruct((nb, t - skip * tm, d), F32),
        compiler_params=_params(("parallel", "parallel")),
        name="moe_out",
    )(*args)


def kernel(x, c, ctx, c_ctx, w_ada, b_ada, norm_mix_g, norm_ffn_g, w_in, cm_ln_g, cm_ln_b, cm_w_s, cm_b_s,
           na_rpb, ret_decay_logit, w_branch, w_out, router_w, router_bias, exp_w_gate, exp_w_up,
           exp_w_down, sh_w_gate, sh_w_up, sh_w_down, final_g):
    nb, l, d = x.shape
    lc = ctx.shape[1]
    depth = w_ada.shape[0]
    assert d == D_MODEL and nb < MOD_ROWS and l % GRID_W == 0 and lc % CHUNK == 0 and l % CHUNK == 0
    tm = TOKEN_TILE
    assert lc % tm == 0 and l % tm == 0

    cc = jnp.zeros((MOD_ROWS, d), F32).at[:nb].set(c).at[nb].set(c_ctx)
    mod_all = _adaln(cc, w_ada, b_ada).reshape(depth, MOD_ROWS, 6, d)
    cos_t, sin_t = _rope_tables(lc, l)
    xs = jnp.concatenate([ctx, x], axis=1)
    wg_all, wu_all, wd_all = (_to_bf16(w).reshape((-1,) + w.shape[2:])
                              for w in (exp_w_gate, exp_w_up, exp_w_down))

    for layer in range(depth):
        mod = mod_all[layer]
        z = _inproj(xs, mod, norm_mix_g[layer], w_in[layer].astype(BF16), tm)
        a_lat = _gmlp(z, cm_ln_g[layer], cm_ln_b[layer], cm_w_s[layer].astype(BF16),
                      jnp.transpose(cm_b_s[layer]))
        b_lat = _nattn(z, _na_bias_table(na_rpb[layer], l // GRID_W), lc)
        ret = _retention(z, ret_decay_logit[layer], cos_t, sin_t, lc)
        xs = _merge(a_lat, b_lat, ret, z, xs, mod, w_branch[layer].astype(BF16),
                    w_out[layer].astype(BF16), tm)
        h2, eidx, wk = _router(xs, mod, norm_ffn_g[layer], jnp.transpose(router_w[layer]),
                               router_bias[layer], tm)
        slab_row, sw, cnt, off = _dispatch(eidx, wk)
        routed = _experts(h2, slab_row, sw, cnt, off, wg_all, wu_all, wd_all, layer)
        last = layer == depth - 1
        xs = _moe_out(h2, routed, xs, mod, sh_w_gate[layer].astype(BF16), sh_w_up[layer].astype(BF16),
                      sh_w_down[layer].astype(BF16), tm, skip=lc // tm if last else 0,
                      final_g=final_g if last else None)
    return xs
```

```python
import functools

import jax
import jax.numpy as jnp
import numpy as np
from jax import lax
from jax.experimental import pallas as pl
from jax.experimental.pallas import tpu as pltpu

F32 = jnp.float32
BF16 = jnp.bfloat16

D_MODEL = 1024
GRID_W = 64
CHUNK = 128
CM_GROUPS = 4
CM_WIDTH = 512
NA_HEADS = 8
NA_HEAD_DIM = 64
NA_WIDTH = NA_HEADS * NA_HEAD_DIM
WIN_R = 8
WIN_C = 16
RET_HEADS = 4
RET_DK = 64
RET_DV = 128
RET_QK_W = RET_HEADS * RET_DK
RET_V_W = RET_HEADS * RET_DV
ROPE_BASE = 10000.0
IN_W = 2 * CM_WIDTH + 3 * NA_WIDTH + 2 * RET_QK_W + 2 * RET_V_W + 3 * D_MODEL
N_EXPERTS = 64
TOP_K = 8
N_GROUPS = 8
TOPK_GROUPS = 4
EXPERT_W = 256
ROUTED_SCALE = 2.5
EPS = 1e-6
NEG = -1e30
LOG2E = 1.4426950408889634

OFF_AU, OFF_AV = 0, 512
OFF_NQ, OFF_NK, OFF_NV = 1024, 1536, 2048
OFF_RQ, OFF_RK, OFF_RV, OFF_RG = 2560, 2816, 3072, 3584
OFF_GA, OFF_GB, OFF_GC = 4096, 5120, 6144

MOD_ROWS = 16
VMEM_LIMIT = 56 * 1024 * 1024

TOKEN_TILE = 256
ADALN_COLS = 1536
INPROJ_COLS = 512
GMLP_ROWS = 2 * CHUNK
NA_ROWS = 4
MOE_TILE = 320
MOE_GROUP = 8
MOE_EPS = 1
MOE_WIN = (MOE_TILE + 127) // 128 * 128 + 128
MOE_LIST_PAD = 1024
CAST_BLOCK_BYTES = 8 * 1024 * 1024


def _params(sem, vmem=None):
    return pltpu.CompilerParams(dimension_semantics=sem, vmem_limit_bytes=vmem)


def _silu(v):
    return v * jax.nn.sigmoid(v)


def _mm(a, b):
    return jnp.dot(a, b, preferred_element_type=F32)


def _mm_nt(a, b):
    return lax.dot_general(a, b, (((1,), (1,)), ((), ())), preferred_element_type=F32)


def _mm_tn(a, b):
    return lax.dot_general(a, b, (((0,), (0,)), ((), ())), preferred_element_type=F32)


def _cast_kernel(x_ref, o_ref):
    o_ref[...] = x_ref[...].astype(BF16)


def _to_bf16(w):
    cols = w.shape[-1]
    flat = w.reshape(-1, cols)
    rows = CAST_BLOCK_BYTES // (4 * cols)
    assert flat.shape[0] % rows == 0
    out = pl.pallas_call(
        _cast_kernel,
        grid=(flat.shape[0] // rows,),
        in_specs=[pl.BlockSpec((rows, cols), lambda i: (i, 0))],
        out_specs=pl.BlockSpec((rows, cols), lambda i: (i, 0)),
        out_shape=jax.ShapeDtypeStruct(flat.shape, BF16),
        compiler_params=_params(("parallel",), VMEM_LIMIT),
        name="to_bf16",
    )(flat)
    return out.reshape(w.shape)


def _mod_row(nb):
    return lambda b, j, *_: (jnp.where(j == 0, nb, b), 0, 0)


def _adaln_kernel(c_ref, w_ref, b_ref, o_ref):
    s = _silu(c_ref[...])
    o_ref[0] = _mm(s, w_ref[0]) + b_ref[0]


def _adaln(cc, w_ada, b_ada):
    depth, d, n6 = w_ada.shape
    tn = ADALN_COLS
    return pl.pallas_call(
        _adaln_kernel,
        grid=(depth, n6 // tn),
        in_specs=[
            pl.BlockSpec((MOD_ROWS, d), lambda l, n: (0, 0)),
            pl.BlockSpec((1, d, tn), lambda l, n: (l, 0, n)),
            pl.BlockSpec((1, 1, tn), lambda l, n: (l, 0, n)),
        ],
        out_specs=pl.BlockSpec((1, MOD_ROWS, tn), lambda l, n: (l, 0, n)),
        out_shape=jax.ShapeDtypeStruct((depth, MOD_ROWS, n6), F32),
        compiler_params=_params(("parallel", "parallel"), VMEM_LIMIT),
        name="adaln",
    )(cc, w_ada, b_ada.reshape(depth, 1, n6))


def _modulated_norm(x, g, shift, scale):
    y = x * lax.rsqrt(jnp.mean(x * x, axis=-1, keepdims=True) + EPS) * g
    return y * (1.0 + scale) + shift


def _inproj_kernel(x_ref, mod_ref, g_ref, w_ref, z_ref):
    h = _modulated_norm(x_ref[0], g_ref[...], mod_ref[0, 0:1, :], mod_ref[0, 1:2, :])
    hb = h.astype(BF16)
    nw = INPROJ_COLS
    for n in range(IN_W // nw):
        z_ref[0, :, n * nw:(n + 1) * nw] = _mm(hb, w_ref[:, n * nw:(n + 1) * nw]).astype(BF16)


def _inproj(xs, mod, g, w_in_b, tm):
    nb, t, d = xs.shape
    return pl.pallas_call(
        _inproj_kernel,
        grid=(nb, t // tm),
        in_specs=[
            pl.BlockSpec((1, tm, d), lambda b, j: (b, j, 0)),
            pl.BlockSpec((1, 6, d), _mod_row(nb)),
            pl.BlockSpec((1, d), lambda b, j: (0, 0)),
            pl.BlockSpec((d, IN_W), lambda b, j: (0, 0)),
        ],
        out_specs=pl.BlockSpec((1, tm, IN_W), lambda b, j: (b, j, 0)),
        out_shape=jax.ShapeDtypeStruct((nb, t, IN_W), BF16),
        compiler_params=_params(("parallel", "parallel"), VMEM_LIMIT),
        name="inproj",
    )(xs, mod, g.reshape(1, d), w_in_b)


def _gmlp_kernel(u_ref, v_ref, lng_ref, lnb_ref, ws_ref, bs_ref, o_ref):
    u = jax.nn.gelu(u_ref[0].astype(F32))
    v = jax.nn.gelu(v_ref[0].astype(F32))
    vc = v - jnp.mean(v, axis=-1, keepdims=True)
    vn = vc * lax.rsqrt(jnp.mean(vc * vc, axis=-1, keepdims=True) + EPS) * lng_ref[...] + lnb_ref[...]
    vb = vn.astype(BF16)
    gw = CM_WIDTH // CM_GROUPS
    for c in range(u.shape[0] // CHUNK):
        rs = slice(c * CHUNK, (c + 1) * CHUNK)
        for g in range(CM_GROUPS):
            gs = slice(g * gw, (g + 1) * gw)
            s = _mm(ws_ref[g], vb[rs, gs]) + bs_ref[:, g:g + 1]
            o_ref[0, rs, gs] = (u[rs, gs] * s).astype(BF16)


def _gmlp(z, ln_g, ln_b, ws_b, bs_t):
    nb, t, _ = z.shape
    assert t % GMLP_ROWS == 0
    return pl.pallas_call(
        _gmlp_kernel,
        grid=(nb, t // GMLP_ROWS),
        in_specs=[
            pl.BlockSpec((1, GMLP_ROWS, CM_WIDTH), lambda b, j: (b, j, OFF_AU // CM_WIDTH)),
            pl.BlockSpec((1, GMLP_ROWS, CM_WIDTH), lambda b, j: (b, j, OFF_AV // CM_WIDTH)),
            pl.BlockSpec((1, CM_WIDTH), lambda b, j: (0, 0)),
            pl.BlockSpec((1, CM_WIDTH), lambda b, j: (0, 0)),
            pl.BlockSpec((CM_GROUPS, CHUNK, CHUNK), lambda b, j: (0, 0, 0)),
            pl.BlockSpec((CHUNK, CM_GROUPS), lambda b, j: (0, 0)),
        ],
        out_specs=pl.BlockSpec((1, GMLP_ROWS, CM_WIDTH), lambda b, j: (b, j, 0)),
        out_shape=jax.ShapeDtypeStruct((nb, t, CM_WIDTH), BF16),
        compiler_params=_params(("parallel", "parallel")),
        name="gmlp",
    )(z, z, ln_g.reshape(1, -1), ln_b.reshape(1, -1), ws_b, bs_t)


def _na_row_classes(rows):
    half = WIN_R // 2
    wr = min(WIN_R, rows)
    reps = list(range(half)) + [half] + list(range(rows - (wr - half) + 1, rows))
    cls = np.zeros((rows,), np.int32)
    for r in range(rows):
        if r < half:
            cls[r] = r
        elif r <= rows - (wr - half):
            cls[r] = half
        else:
            cls[r] = half + 1 + (r - (rows - (wr - half) + 1))
    return reps, cls


def _na_bias_table(rpb, rows):
    wr = min(WIN_R, rows)
    reps, _ = _na_row_classes(rows)
    nh = rpb.shape[0]
    qc = np.arange(GRID_W)[:, None]
    kc = np.arange(GRID_W)[None, :]
    col_start = np.clip(qc - WIN_C // 2, 0, GRID_W - WIN_C)
    valid = (kc >= col_start) & (kc < col_start + WIN_C)
    pad = GRID_W - WIN_C
    edge = jnp.concatenate([jnp.repeat(rpb[..., :1], pad, axis=-1), rpb.astype(F32),
                            jnp.repeat(rpb[..., -1:], pad, axis=-1)], axis=-1)
    toep = jnp.stack([edge[..., GRID_W - 1 - q:2 * GRID_W - 1 - q] for q in range(GRID_W)], axis=1)
    toep = jnp.where(valid[None, :, None, :], toep * LOG2E, NEG)
    tabs = []
    for r in reps:
        r0 = int(np.clip(r - WIN_R // 2, 0, rows - wr))
        lo = r0 - r + WIN_R - 1
        tabs.append(toep[:, :, lo:lo + wr, :].reshape(nh, GRID_W, wr * GRID_W))
    tab = jnp.stack(tabs)
    tab = tab.reshape(len(reps), nh // 2, 2, GRID_W, wr * GRID_W)
    return jnp.transpose(tab, (0, 1, 4, 2, 3)).reshape(len(reps), nh // 2, wr * GRID_W, 2 * GRID_W)


def _t_blocks(p):
    return jnp.concatenate([jnp.transpose(p[i:i + 128]) for i in range(0, p.shape[0], 128)],
                           axis=1).astype(BF16)


def _na_kernel(cls_ref, q_ref, k_ref, v_ref, *refs, lc, rows):
    del cls_ref
    j = pl.program_id(1)
    n_ctx_blk = lc // GRID_W
    wr = min(WIN_R, rows)
    sub = lax.broadcasted_iota(jnp.int32, (128, 128), 0)
    lane = lax.broadcasted_iota(jnp.int32, (128, 128), 1)
    same_head = (sub // 64) == (lane // 64)
    first_head = lax.broadcasted_iota(jnp.int32, (GRID_W, 128), 1) < 64
    scale = NA_HEAD_DIM ** -0.5 * LOG2E
    bias_refs, o_ref = refs[:-1], refs[-1]

    def attend(local, rr):
        qs = slice(rr * GRID_W, (rr + 1) * GRID_W)
        bias_ref = bias_refs[rr]
        q = q_ref[0, qs, :]
        if local:
            r0 = jnp.clip(j * NA_ROWS + rr - n_ctx_blk - WIN_R // 2, 0, rows - wr)
            start = pl.multiple_of(lc + r0 * GRID_W, GRID_W)
            k_loc = k_ref[0, pl.ds(start, wr * GRID_W), :]
            v_loc = v_ref[0, pl.ds(start, wr * GRID_W), :]
        k_ctx = k_ref[0, 0:lc, :]
        v_ctx = v_ref[0, 0:lc, :]
        for p in range(NA_HEADS // 2):
            cs = slice(p * 128, (p + 1) * 128)
            q2 = q[:, cs].astype(F32) * scale
            qbd = jnp.transpose(jnp.where(same_head, jnp.concatenate([q2, q2], axis=0), 0.0)).astype(BF16)
            s_ctx = _mm(k_ctx[:, cs], qbd)
            m = jnp.max(s_ctx, axis=0, keepdims=True)
            if local:
                s_loc = _mm(k_loc[:, cs], qbd) + bias_ref[0, p]
                m = jnp.maximum(m, jnp.max(s_loc, axis=0, keepdims=True))
            p_ctx = jnp.exp2(s_ctx - m)
            den = jnp.sum(p_ctx, axis=0, keepdims=True)
            if local:
                p_loc = jnp.exp2(s_loc - m)
                den = den + jnp.sum(p_loc, axis=0, keepdims=True)
            o = _mm(_t_blocks(p_ctx), v_ctx[:, cs])
            if local:
                o = o + _mm(_t_blocks(p_loc), v_loc[:, cs])
            o = o * jnp.transpose(jnp.broadcast_to(1.0 / den, (128, 128)))
            o_ref[0, qs, cs] = jnp.where(first_head, o[:GRID_W], o[GRID_W:]).astype(BF16)

    @pl.when(j * NA_ROWS < n_ctx_blk)
    def _():
        for rr in range(NA_ROWS):
            attend(False, rr)

    @pl.when(j * NA_ROWS >= n_ctx_blk)
    def _():
        for rr in range(NA_ROWS):
            attend(True, rr)


def _nattn(z, bias_tab, lc):
    nb, t, _ = z.shape
    rows = (t - lc) // GRID_W
    n_ctx_blk = lc // GRID_W
    _, cls = _na_row_classes(rows)
    cls_arr = jnp.asarray(np.concatenate([np.zeros((n_ctx_blk,), np.int32), cls]))
    wr = min(WIN_R, rows)
    qb = NA_ROWS * GRID_W
    assert n_ctx_blk % NA_ROWS == 0 and rows % NA_ROWS == 0

    def bias_spec(rr):
        return pl.BlockSpec((1, NA_HEADS // 2, wr * GRID_W, 2 * GRID_W),
                            lambda b, j, c: (c[j * NA_ROWS + rr], 0, 0, 0))

    grid_spec = pltpu.PrefetchScalarGridSpec(
        num_scalar_prefetch=1,
        grid=(nb, t // qb),
        in_specs=[
            pl.BlockSpec((1, qb, NA_WIDTH), lambda b, j, c: (b, j, OFF_NQ // NA_WIDTH)),
            pl.BlockSpec((1, t, NA_WIDTH), lambda b, j, c: (b, 0, OFF_NK // NA_WIDTH)),
            pl.BlockSpec((1, t, NA_WIDTH), lambda b, j, c: (b, 0, OFF_NV // NA_WIDTH)),
        ] + [bias_spec(rr) for rr in range(NA_ROWS)],
        out_specs=pl.BlockSpec((1, qb, NA_WIDTH), lambda b, j, c: (b, j, 0)),
    )
    return pl.pallas_call(
        functools.partial(_na_kernel, lc=lc, rows=rows),
        grid_spec=grid_spec,
        out_shape=jax.ShapeDtypeStruct((nb, t, NA_WIDTH), BF16),
        compiler_params=_params(("parallel", "arbitrary"), VMEM_LIMIT),
        name="nattn",
    )(cls_arr, z, z, z, *([bias_tab] * NA_ROWS))


def _rope_tables(lc, l):
    half = RET_DK // 2
    nf = half // 2
    tpos = np.arange(l)
    inv = ROPE_BASE ** (-jnp.arange(nf, dtype=F32) / nf)

    def part(pos):
        ang = jnp.asarray(pos, F32)[:, None] * inv[None, :]
        c, s = jnp.cos(ang), jnp.sin(ang)
        return jnp.concatenate([c, c], -1), jnp.concatenate([-s, s], -1)

    c_r, s_r = part(tpos // GRID_W)
    c_c, s_c = part(tpos % GRID_W)
    cos = jnp.concatenate([c_r, c_c], -1)
    sin = jnp.concatenate([s_r, s_c], -1)
    cos = jnp.concatenate([jnp.ones((lc, RET_DK), F32), cos], 0)
    sin = jnp.concatenate([jnp.zeros((lc, RET_DK), F32), sin], 0)
    return jnp.tile(cos, (1, RET_HEADS)), jnp.tile(sin, (1, RET_HEADS))


def _ret_kernel(dl_ref, *refs):
    st_ref = refs[-1]

    @pl.when(pl.program_id(1) == 0)
    def _():
        st_ref[...] = jnp.zeros_like(st_ref)

    for d in range(2):
        _ret_direction(d, dl_ref, *refs[5 * d:5 * d + 5], refs[10 + d], st_ref)


def _ret_direction(d, dl_ref, q_ref, k_ref, v_ref, cos_ref, sin_ref, o_ref, st_ref):
    cos = cos_ref[...]
    sin = sin_ref[...]
    lane_qk = lax.broadcasted_iota(jnp.int32, (CHUNK, RET_QK_W), 1)
    first = (lane_qk % (RET_DK // 2)) < (RET_DK // 4)

    def rope(a):
        a = a.astype(F32)
        partner = jnp.where(first, pltpu.roll(a, RET_QK_W - RET_DK // 4, axis=1),
                            pltpu.roll(a, RET_DK // 4, axis=1))
        return a * cos + partner * sin

    q = rope(q_ref[0])
    k = rope(k_ref[0]) * (RET_DK ** -0.5)
    v = v_ref[0]

    ri = lax.broadcasted_iota(jnp.int32, (CHUNK, 2 * CHUNK), 0)
    ci = lax.broadcasted_iota(jnp.int32, (CHUNK, 2 * CHUNK), 1) % CHUNK
    dist = (ri - ci) if d == 0 else (ci - ri)
    keep = dist >= d
    dist_f = jnp.maximum(dist, 0).astype(F32)
    q_exp = ((ri + 1) if d == 0 else (CHUNK - ri)).astype(F32)
    rk = lax.broadcasted_iota(jnp.int32, (CHUNK, CHUNK), 0)
    k_exp = ((CHUNK - 1 - rk) if d == 0 else rk).astype(F32)
    lane_half = lax.broadcasted_iota(jnp.int32, (CHUNK, CHUNK), 1) >= RET_DK
    row_half = rk >= RET_DK

    def log_gamma(h):
        xv = jnp.full((CHUNK, CHUNK), dl_ref[d, h], F32)
        return jnp.minimum(xv, 0.0) - jnp.log1p(jnp.exp(-jnp.abs(xv)))

    def by_head(m):
        return jnp.concatenate([jnp.where(row_half, 0.0, m), jnp.where(row_half, m, 0.0)], axis=1).astype(BF16)

    zero = jnp.zeros((CHUNK, RET_DV), BF16)
    for p in range(RET_HEADS // 2):
        cs = slice(p * 128, (p + 1) * 128)
        lg0, lg1 = log_gamma(2 * p), log_gamma(2 * p + 1)
        lg2 = jnp.concatenate([lg0, lg1], axis=1)
        q2b = q[:, cs].astype(BF16)
        k2 = k[:, cs]
        kd_t = jnp.transpose(k2 * jnp.exp(k_exp * jnp.where(lane_half, lg1, lg0)))
        v0 = v[:, 2 * p * RET_DV:(2 * p + 1) * RET_DV]
        v1 = v[:, (2 * p + 1) * RET_DV:(2 * p + 2) * RET_DV]
        s2 = st_ref[d, p]
        a = _mm(q2b, by_head(jnp.transpose(k2)))
        a = jnp.where(keep, a * jnp.exp(dist_f * lg2), 0.0)
        v_diag = jnp.concatenate([jnp.concatenate([v0, zero], axis=1),
                                  jnp.concatenate([zero, v1], axis=1)], axis=0)
        inner = _mm(a.astype(BF16), v_diag)
        cross = _mm(q2b, by_head(s2)) * jnp.exp(q_exp * lg2)
        o_ref[0, :, 2 * p * RET_DV:(2 * p + 2) * RET_DV] = (inner + cross).astype(o_ref.dtype)
        st_ref[d, p] = (s2 * jnp.exp(float(CHUNK) * jnp.where(row_half, lg1, lg0))
                        + _mm(by_head(kd_t), jnp.concatenate([v0, v1], axis=0)))


def _retention(z, decay_logit, cos_t, sin_t, lc):
    nb, t, _ = z.shape
    nc = t // CHUNK
    ncc = lc // CHUNK

    def chunk(d, s):
        return s if d == 0 else jnp.where(s < ncc, ncc - 1 - s, nc - 1 - (s - ncc))

    def in_specs(d):
        return [
            pl.BlockSpec((1, CHUNK, RET_QK_W), lambda b, s, dl: (b, chunk(d, s), OFF_RQ // RET_QK_W)),
            pl.BlockSpec((1, CHUNK, RET_QK_W), lambda b, s, dl: (b, chunk(d, s), OFF_RK // RET_QK_W)),
            pl.BlockSpec((1, CHUNK, RET_V_W), lambda b, s, dl: (b, chunk(d, s), OFF_RV // RET_V_W)),
            pl.BlockSpec((CHUNK, RET_QK_W), lambda b, s, dl: (chunk(d, s), 0)),
            pl.BlockSpec((CHUNK, RET_QK_W), lambda b, s, dl: (chunk(d, s), 0)),
        ]

    def out_spec(d):
        return pl.BlockSpec((1, CHUNK, RET_V_W), lambda b, s, dl: (b, chunk(d, s), 0))

    grid_spec = pltpu.PrefetchScalarGridSpec(
        num_scalar_prefetch=1,
        grid=(nb, nc),
        in_specs=in_specs(0) + in_specs(1),
        out_specs=[out_spec(0), out_spec(1)],
        scratch_shapes=[pltpu.VMEM((2, RET_HEADS // 2, 2 * RET_DK, RET_DV), F32)],
    )
    return pl.pallas_call(
        _ret_kernel,
        grid_spec=grid_spec,
        out_shape=[jax.ShapeDtypeStruct((nb, t, RET_V_W), BF16)] * 2,
        compiler_params=_params(("parallel", "arbitrary")),
        name="retention",
    )(decay_logit, *([z, z, z, cos_t, sin_t] * 2))


def _merge_kernel(a_ref, b_ref, rf_ref, rb_ref, rg_ref, ga_ref, gb_ref, gc_ref, x_ref, mod_ref,
                  wb_ref, wo_ref, o_ref):
    o = rf_ref[0].astype(F32) + rb_ref[0].astype(F32)
    parts = []
    for h in range(RET_HEADS):
        oh = o[:, h * RET_DV:(h + 1) * RET_DV]
        parts.append(oh * lax.rsqrt(jnp.mean(oh * oh, axis=-1, keepdims=True) + EPS))
    c = jnp.concatenate(parts, axis=-1) * _silu(rg_ref[0].astype(F32))
    m = (jax.nn.sigmoid(ga_ref[0].astype(F32)) * _mm(a_ref[0], wb_ref[0])
         + jax.nn.sigmoid(gb_ref[0].astype(F32)) * _mm(b_ref[0], wb_ref[1])
         + jax.nn.sigmoid(gc_ref[0].astype(F32)) * _mm(c.astype(BF16), wb_ref[2]))
    mix = _mm(m.astype(BF16), wo_ref[...])
    o_ref[0] = x_ref[0] + mod_ref[0, 2:3, :] * mix


def _merge(a_lat, b_lat, ret, z, xs, mod, wb_b, wo_b, tm):
    nb, t, d = xs.shape
    bw = CM_WIDTH
    return pl.pallas_call(
        _merge_kernel,
        grid=(nb, t // tm),
        in_specs=[
            pl.BlockSpec((1, tm, bw), lambda b, j: (b, j, 0)),
            pl.BlockSpec((1, tm, bw), lambda b, j: (b, j, 0)),
            pl.BlockSpec((1, tm, bw), lambda b, j: (b, j, 0)),
            pl.BlockSpec((1, tm, bw), lambda b, j: (b, j, 0)),
            pl.BlockSpec((1, tm, RET_V_W), lambda b, j: (b, j, OFF_RG // RET_V_W)),
            pl.BlockSpec((1, tm, d), lambda b, j: (b, j, OFF_GA // d)),
            pl.BlockSpec((1, tm, d), lambda b, j: (b, j, OFF_GB // d)),
            pl.BlockSpec((1, tm, d), lambda b, j: (b, j, OFF_GC // d)),
            pl.BlockSpec((1, tm, d), lambda b, j: (b, j, 0)),
            pl.BlockSpec((1, 6, d), _mod_row(nb)),
            pl.BlockSpec((3, bw, d), lambda b, j: (0, 0, 0)),
            pl.BlockSpec((d, d), lambda b, j: (0, 0)),
        ],
        out_specs=pl.BlockSpec((1, tm, d), lambda b, j: (b, j, 0)),
        out_shape=jax.ShapeDtypeStruct((nb, t, d), F32),
        compiler_params=_params(("parallel", "parallel"), VMEM_LIMIT),
        name="merge",
    )(a_lat, b_lat, ret[0], ret[1], z, z, z, z, xs, mod, wb_b, wo_b)


ROW_SLAB = D_MODEL // 128


def _slab_load(ref, lead, n, row0=0):
    return jnp.concatenate(
        [ref[lead + (pl.ds(row0 * ROW_SLAB + j, n, stride=ROW_SLAB), slice(None))] for j in range(ROW_SLAB)],
        axis=-1)


def _slab_store(ref, lead, val, row0=0):
    for j in range(ROW_SLAB):
        ref[lead + (pl.ds(row0 * ROW_SLAB + j, val.shape[0], stride=ROW_SLAB), slice(None))] = (
            val[:, j * 128:(j + 1) * 128])


def _first_index(mask, idx, big):
    return jnp.min(jnp.where(mask, idx, big), axis=0, keepdims=True)


def _router_kernel(x_ref, mod_ref, g_ref, rwt_ref, rb_ref, h_ref, ei_ref, wk_ref):
    h = _modulated_norm(x_ref[0], g_ref[...], mod_ref[0, 3:4, :], mod_ref[0, 4:5, :])
    _slab_store(h_ref, (0,), h)
    tm = h.shape[0]
    logits = _mm_nt(rwt_ref[...], h)
    scores = jax.nn.sigmoid(logits)
    biased = scores + rb_ref[:, 0:1]
    gsz = N_EXPERTS // N_GROUPS
    sub = lax.broadcasted_iota(jnp.int32, (gsz, tm), 0)
    gscore = []
    for g in range(N_GROUPS):
        blk = biased[g * gsz:(g + 1) * gsz]
        m1 = jnp.max(blk, axis=0, keepdims=True)
        i1 = _first_index(blk == m1, sub, gsz)
        m2 = jnp.max(jnp.where(sub == i1, -jnp.inf, blk), axis=0, keepdims=True)
        gscore.append(m1 + m2)
    gs = jnp.concatenate(gscore, axis=0)
    gidx = lax.broadcasted_iota(jnp.int32, (N_GROUPS, tm), 0)
    gsel = jnp.zeros((N_GROUPS, tm), jnp.int32)
    for _ in range(TOPK_GROUPS):
        m = jnp.max(gs, axis=0, keepdims=True)
        pick = gidx == _first_index(gs == m, gidx, N_GROUPS)
        gsel = jnp.where(pick, 1, gsel)
        gs = jnp.where(pick, -jnp.inf, gs)
    cand = jnp.concatenate(
        [jnp.where(gsel[g:g + 1] > 0, biased[g * gsz:(g + 1) * gsz], -jnp.inf) for g in range(N_GROUPS)],
        axis=0)
    eidx = lax.broadcasted_iota(jnp.int32, (N_EXPERTS, tm), 0)
    picks, ws = [], []
    for _ in range(TOP_K):
        m = jnp.max(cand, axis=0, keepdims=True)
        first = _first_index(cand == m, eidx, N_EXPERTS)
        pick = eidx == first
        picks.append(first)
        ws.append(jnp.sum(jnp.where(pick, scores, 0.0), axis=0, keepdims=True))
        cand = jnp.where(pick, -jnp.inf, cand)
    w = jnp.concatenate(ws, axis=0)
    ei_ref[0] = jnp.concatenate(picks, axis=0)
    wk_ref[0] = w / jnp.sum(w, axis=0, keepdims=True) * ROUTED_SCALE


def _router(xs, mod, g, rw_t, rb, tm):
    nb, t, d = xs.shape
    return pl.pallas_call(
        _router_kernel,
        grid=(nb, t // tm),
        in_specs=[
            pl.BlockSpec((1, tm, d), lambda b, j: (b, j, 0)),
            pl.BlockSpec((1, 6, d), _mod_row(nb)),
            pl.BlockSpec((1, d), lambda b, j: (0, 0)),
            pl.BlockSpec((N_EXPERTS, d), lambda b, j: (0, 0)),
            pl.BlockSpec((N_EXPERTS, 1), lambda b, j: (0, 0)),
        ],
        out_specs=[
            pl.BlockSpec((1, tm * ROW_SLAB, 128), lambda b, j: (b, j, 0)),
            pl.BlockSpec((1, TOP_K, tm), lambda b, j: (b, 0, j)),
            pl.BlockSpec((1, TOP_K, tm), lambda b, j: (b, 0, j)),
        ],
        out_shape=[jax.ShapeDtypeStruct((nb, t * ROW_SLAB, 128), F32),
                   jax.ShapeDtypeStruct((nb, TOP_K, t), jnp.int32), jax.ShapeDtypeStruct((nb, TOP_K, t), F32)],
        compiler_params=_params(("parallel", "parallel")),
        name="router",
    )(xs, mod, g.reshape(1, d), rw_t, rb.reshape(N_EXPERTS, 1))


def _dispatch(eidx, wk):
    nb, k, t = eidx.shape
    pos = lax.broadcasted_iota(jnp.int32, eidx.shape, 2)
    skey, sw = lax.sort(((eidx * t + pos).reshape(nb, k * t), wk.reshape(nb, k * t)), dimension=1, num_keys=1)
    slab_row = (skey % t) * ROW_SLAB
    cnt = jnp.sum(eidx[:, None] == jnp.arange(N_EXPERTS, dtype=jnp.int32)[None, :, None, None],
                  axis=(2, 3), dtype=jnp.int32)
    off = jnp.cumsum(cnt, axis=1) - cnt
    pad = ((0, 0), (0, MOE_LIST_PAD))
    return jnp.pad(slab_row, pad), jnp.pad(sw, pad), cnt, off


def _slab(row):
    return pl.ds(pl.multiple_of(row, ROW_SLAB), ROW_SLAB)


def _expert_ffn(base, w_ref, stage_ref, ostage_ref, wg_ref, wu_ref, wd_ref, k):
    x = _slab_load(stage_ref, (), MOE_TILE, k * MOE_TILE).astype(BF16)
    act = _silu(_mm(x, wg_ref[k])) * _mm(x, wu_ref[k])
    aligned = pl.multiple_of((base // 128) * 128, 128)
    wrow = pltpu.roll(w_ref[0, :, pl.ds(aligned, MOE_WIN)], (MOE_WIN - (base - aligned)) % MOE_WIN, axis=1)
    wcol = jnp.concatenate(
        [jnp.transpose(jnp.broadcast_to(wrow[:, k * 128:(k + 1) * 128], (128, 128)))
         for k in range(MOE_WIN // 128 - 1)], axis=0)[:MOE_TILE]
    act = act * jnp.concatenate([wcol] * (EXPERT_W // 128), axis=-1)
    _slab_store(ostage_ref, (), _mm(act.astype(BF16), wd_ref[k]), k * MOE_TILE)


def _experts_kernel(cnt_ref, off_ref, idx_ref, w_ref, src_ref, wg_ref, wu_ref, wd_ref, acc_ref,
                    stage0, stage1, ostage0, ostage1, *, ne, spare):
    b = pl.program_id(0)
    s = pl.program_id(1)
    ns = ne // MOE_EPS

    def expert(stage, k):
        return b * ne + stage * MOE_EPS + k

    st_g = jnp.minimum(s, ns - 1)
    st_m = jnp.clip(s - 1, 0, ns - 1)
    st_s = jnp.maximum(s - 2, 0)
    ffn_live = (s >= 1) & (s <= ns)
    start_g = [off_ref[expert(st_g, k)] for k in range(MOE_EPS)]
    start_m = [off_ref[expert(st_m, k)] for k in range(MOE_EPS)]
    count_m = [jnp.where(ffn_live, cnt_ref[expert(st_m, k)], 0) for k in range(MOE_EPS)]
    start_s = [off_ref[expert(st_s, k)] for k in range(MOE_EPS)]
    count_s = [jnp.where(s >= 2, jnp.minimum(cnt_ref[expert(st_s, k)], MOE_TILE), 0) for k in range(MOE_EPS)]

    @pl.when(s == 0)
    def _():
        acc_ref[...] = jnp.zeros_like(acc_ref)
        stage1[...] = jnp.zeros_like(stage1)
        ostage0[...] = jnp.zeros_like(ostage0)

    def step(stage_w, stage_r, ostage_w, ostage_r):
        for k in range(MOE_EPS):
            for r in range(MOE_TILE):
                stage_w[_slab((k * MOE_TILE + r) * ROW_SLAB), :] = (
                    src_ref[0, _slab(idx_ref[0, 0, start_g[k] + r]), :])
        for k in range(MOE_EPS):
            _expert_ffn(start_m[k], w_ref, stage_r, ostage_w, wg_ref, wu_ref, wd_ref, k)
        for k in range(MOE_EPS):
            for g in range(MOE_TILE // MOE_GROUP):
                rws = range(g * MOE_GROUP, (g + 1) * MOE_GROUP)
                dst = [_slab(jnp.where(r < count_s[k], idx_ref[0, 0, start_s[k] + r], spare)) for r in rws]
                vals = [acc_ref[0, d, :] + ostage_r[_slab((k * MOE_TILE + r) * ROW_SLAB), :]
                        for d, r in zip(dst, rws)]
                for d, v in zip(dst, vals):
                    acc_ref[0, d, :] = v

    @pl.when(s % 2 == 0)
    def _():
        step(stage0, stage1, ostage1, ostage0)

    @pl.when(s % 2 == 1)
    def _():
        step(stage1, stage0, ostage0, ostage1)

    def overflow(stage_x, ostage_x, k):
        def tile(ti, carry):
            base = start_m[k] + ti * MOE_TILE
            rows = jnp.minimum(count_m[k] - ti * MOE_TILE, MOE_TILE)

            def gather(r, c):
                stage_x[_slab((k * MOE_TILE + r) * ROW_SLAB), :] = src_ref[0, _slab(idx_ref[0, 0, base + r]), :]
                return c

            def scatter(r, c):
                d = _slab(idx_ref[0, 0, base + r])
                acc_ref[0, d, :] = acc_ref[0, d, :] + ostage_x[_slab((k * MOE_TILE + r) * ROW_SLAB), :]
                return c

            lax.fori_loop(0, rows, gather, 0)
            _expert_ffn(base, w_ref, stage_x, ostage_x, wg_ref, wu_ref, wd_ref, k)
            lax.fori_loop(0, rows, scatter, 0)
            return carry

        lax.fori_loop(1, (count_m[k] + MOE_TILE - 1) // MOE_TILE, tile, 0)

    for k in range(MOE_EPS):
        @pl.when((count_m[k] > MOE_TILE) & (s % 2 == 0))
        def _(k=k):
            overflow(stage1, ostage0, k)

        @pl.when((count_m[k] > MOE_TILE) & (s % 2 == 1))
        def _(k=k):
            overflow(stage0, ostage1, k)


def _experts(h2, slab_row, sw, cnt, off, wg, wu, wd, layer):
    nb = h2.shape[0]
    t = h2.shape[1] // ROW_SLAB
    ne = cnt.shape[1]
    _, d, ew = wg.shape
    n_list = slab_row.shape[1]
    assert h2.shape == (nb, t * ROW_SLAB, 128) and d == D_MODEL and ew == EXPERT_W and MOE_TILE % 16 == 0
    assert MOE_LIST_PAD >= MOE_WIN and ne % MOE_EPS == 0
    ns = ne // MOE_EPS
    ffn = lambda b, s, c, o: (layer * ns + jnp.clip(s - 1, 0, ns - 1), 0, 0)
    acc_rows = (t + 1) * ROW_SLAB
    grid_spec = pltpu.PrefetchScalarGridSpec(
        num_scalar_prefetch=2,
        grid=(nb, ns + 2),
        in_specs=[
            pl.BlockSpec((1, 1, n_list), lambda b, s, c, o: (b, 0, 0), memory_space=pltpu.SMEM),
            pl.BlockSpec((1, 1, n_list), lambda b, s, c, o: (b, 0, 0)),
            pl.BlockSpec((1, t * ROW_SLAB, 128), lambda b, s, c, o: (b, 0, 0)),
            pl.BlockSpec((MOE_EPS, d, ew), ffn),
            pl.BlockSpec((MOE_EPS, d, ew), ffn),
            pl.BlockSpec((MOE_EPS, ew, d), ffn),
        ],
        out_specs=pl.BlockSpec((1, acc_rows, 128), lambda b, s, c, o: (b, 0, 0)),
        scratch_shapes=[pltpu.VMEM((MOE_EPS * MOE_TILE * ROW_SLAB, 128), F32)] * 4,
    )
    return pl.pallas_call(
        functools.partial(_experts_kernel, ne=ne, spare=t * ROW_SLAB),
        grid_spec=grid_spec,
        out_shape=jax.ShapeDtypeStruct((nb, acc_rows, 128), F32),
        compiler_params=_params(("parallel", "arbitrary"), VMEM_LIMIT),
        name="experts",
    )(cnt.reshape(nb * ne), off.reshape(nb * ne), slab_row.reshape(nb, 1, n_list), sw.reshape(nb, 1, n_list),
      h2, wg, wu, wd)


def _moe_out_kernel(h_ref, r_ref, x_ref, mod_ref, sg_ref, su_ref, sd_ref, *rest):
    tm = x_ref.shape[1]
    h = _slab_load(h_ref, (0,), tm).astype(BF16)
    shared = _mm((_silu(_mm(h, sg_ref[...])) * _mm(h, su_ref[...])).astype(BF16), sd_ref[...])
    y = x_ref[0] + mod_ref[0, 5:6, :] * (_slab_load(r_ref, (0,), tm) + shared)
    if len(rest) == 2:
        y = y * lax.rsqrt(jnp.mean(y * y, axis=-1, keepdims=True) + EPS) * rest[0][...]
    rest[-1][0] = y


def _moe_out(h2, routed, xs, mod, sg, su, sd, tm, skip=0, final_g=None):
    nb, t, d = xs.shape
    tok = lambda b, j: (b, j + skip, 0)
    in_specs = [
        pl.BlockSpec((1, tm * ROW_SLAB, 128), tok),
        pl.BlockSpec((1, tm * ROW_SLAB, 128), tok),
        pl.BlockSpec((1, tm, d), tok),
        pl.BlockSpec((1, 6, d), (lambda b, j: (b, 0, 0)) if skip else _mod_row(nb)),
        pl.BlockSpec(sg.shape, lambda b, j: (0, 0)),
        pl.BlockSpec(su.shape, lambda b, j: (0, 0)),
        pl.BlockSpec(sd.shape, lambda b, j: (0, 0)),
    ]
    args = [h2, routed, xs, mod, sg, su, sd]
    if final_g is not None:
        in_specs.append(pl.BlockSpec((1, d), lambda b, j: (0, 0)))
        args.append(final_g.reshape(1, d))
    return pl.pallas_call(
        _moe_out_kernel,
        grid=(nb, t // tm - skip),
        in_specs=in_specs,
        out_specs=pl.BlockSpec((1, tm, d), lambda b, j: (b, j, 0)),
        out_shape=jax.ShapeDtypeStruct((nb, t - skip * tm, d), F32),
        compiler_params=_params(("parallel", "parallel")),
        name="moe_out",
    )(*args)


def kernel(x, c, ctx, c_ctx, w_ada, b_ada, norm_mix_g, norm_ffn_g, w_in, cm_ln_g, cm_ln_b, cm_w_s, cm_b_s,
           na_rpb, ret_decay_logit, w_branch, w_out, router_w, router_bias, exp_w_gate, exp_w_up,
           exp_w_down, sh_w_gate, sh_w_up, sh_w_down, final_g):
    nb, l, d = x.shape
    lc = ctx.shape[1]
    depth = w_ada.shape[0]
    assert d == D_MODEL and nb < MOD_ROWS and l % GRID_W == 0 and lc % CHUNK == 0 and l % CHUNK == 0
    tm = TOKEN_TILE
    assert lc % tm == 0 and l % tm == 0

    cc = jnp.zeros((MOD_ROWS, d), F32).at[:nb].set(c).at[nb].set(c_ctx)
    mod_all = _adaln(cc, w_ada, b_ada).reshape(depth, MOD_ROWS, 6, d)
    cos_t, sin_t = _rope_tables(lc, l)
    xs = jnp.concatenate([ctx, x], axis=1)
    wg_all, wu_all, wd_all = (_to_bf16(w).reshape((-1,) + w.shape[2:])
                              for w in (exp_w_gate, exp_w_up, exp_w_down))

    for layer in range(depth):
        mod = mod_all[layer]
        z = _inproj(xs, mod, norm_mix_g[layer], w_in[layer].astype(BF16), tm)
        a_lat = _gmlp(z, cm_ln_g[layer], cm_ln_b[layer], cm_w_s[layer].astype(BF16),
                      jnp.transpose(cm_b_s[layer]))
        b_lat = _nattn(z, _na_bias_table(na_rpb[layer], l // GRID_W), lc)
        ret = _retention(z, ret_decay_logit[layer], cos_t, sin_t, lc)
        xs = _merge(a_lat, b_lat, ret, z, xs, mod, w_branch[layer].astype(BF16),
                    w_out[layer].astype(BF16), tm)
        h2, eidx, wk = _router(xs, mod, norm_ffn_g[layer], jnp.transpose(router_w[layer]),
                               router_bias[layer], tm)
        slab_row, sw, cnt, off = _dispatch(eidx, wk)
        routed = _experts(h2, slab_row, sw, cnt, off, wg_all, wu_all, wd_all, layer)
        last = layer == depth - 1
        xs = _moe_out(h2, routed, xs, mod, sh_w_gate[layer].astype(BF16), sh_w_up[layer].astype(BF16),
                      sh_w_down[layer].astype(BF16), tm, skip=lc // tm if last else 0,
                      final_g=final_g if last else None)
    return xs
```

```python
import functools

import jax
import jax.numpy as jnp
import numpy as np
from jax import lax
from jax.experimental import pallas as pl
from jax.experimental.pallas import tpu as pltpu

F32 = jnp.float32
BF16 = jnp.bfloat16

D_MODEL = 1024
GRID_W = 64
CHUNK = 128
CM_GROUPS = 4
CM_WIDTH = 512
NA_HEADS = 8
NA_HEAD_DIM = 64
NA_WIDTH = NA_HEADS * NA_HEAD_DIM
WIN_R = 8
WIN_C = 16
RET_HEADS = 4
RET_DK = 64
RET_DV = 128
RET_QK_W = RET_HEADS * RET_DK
RET_V_W = RET_HEADS * RET_DV
ROPE_BASE = 10000.0
IN_W = 2 * CM_WIDTH + 3 * NA_WIDTH + 2 * RET_QK_W + 2 * RET_V_W + 3 * D_MODEL
N_EXPERTS = 64
TOP_K = 8
N_GROUPS = 8
TOPK_GROUPS = 4
EXPERT_W = 256
ROUTED_SCALE = 2.5
EPS = 1e-6
NEG = -1e30
LOG2E = 1.4426950408889634

OFF_AU, OFF_AV = 0, 512
OFF_NQ, OFF_NK, OFF_NV = 1024, 1536, 2048
OFF_RQ, OFF_RK, OFF_RV, OFF_RG = 2560, 2816, 3072, 3584
OFF_GA, OFF_GB, OFF_GC = 4096, 5120, 6144

MOD_ROWS = 16
VMEM_LIMIT = 56 * 1024 * 1024

TOKEN_TILE = 256
ADALN_COLS = 1536
INPROJ_COLS = 512
GMLP_ROWS = 6 * CHUNK
NA_ROWS = 4
MOE_TILE = 320
MOE_GROUP = 8
MOE_EPS = 1
MOE_WIN = (MOE_TILE + 127) // 128 * 128 + 128
MOE_LIST_PAD = 1024
CAST_BLOCK_BYTES = 8 * 1024 * 1024


def _params(sem, vmem=None):
    return pltpu.CompilerParams(dimension_semantics=sem, vmem_limit_bytes=vmem)


def _silu(v):
    return v * jax.nn.sigmoid(v)


def _mm(a, b):
    return jnp.dot(a, b, preferred_element_type=F32)


def _mm_nt(a, b):
    return lax.dot_general(a, b, (((1,), (1,)), ((), ())), preferred_element_type=F32)


def _mm_tn(a, b):
    return lax.dot_general(a, b, (((0,), (0,)), ((), ())), preferred_element_type=F32)


def _cast_kernel(x_ref, o_ref):
    o_ref[...] = x_ref[...].astype(BF16)


def _to_bf16(w):
    cols = w.shape[-1]
    flat = w.reshape(-1, cols)
    rows = CAST_BLOCK_BYTES // (4 * cols)
    assert flat.shape[0] % rows == 0
    out = pl.pallas_call(
        _cast_kernel,
        grid=(flat.shape[0] // rows,),
        in_specs=[pl.BlockSpec((rows, cols), lambda i: (i, 0))],
        out_specs=pl.BlockSpec((rows, cols), lambda i: (i, 0)),
        out_shape=jax.ShapeDtypeStruct(flat.shape, BF16),
        compiler_params=_params(("parallel",), VMEM_LIMIT),
        name="to_bf16",
    )(flat)
    return out.reshape(w.shape)


def _mod_row(nb):
    return lambda b, j, *_: (jnp.where(j == 0, nb, b), 0, 0)


def _adaln_kernel(c_ref, w_ref, b_ref, o_ref):
    s = _silu(c_ref[...])
    o_ref[0] = _mm(s, w_ref[0]) + b_ref[0]


def _adaln(cc, w_ada, b_ada):
    depth, d, n6 = w_ada.shape
    tn = ADALN_COLS
    return pl.pallas_call(
        _adaln_kernel,
        grid=(depth, n6 // tn),
        in_specs=[
            pl.BlockSpec((MOD_ROWS, d), lambda l, n: (0, 0)),
            pl.BlockSpec((1, d, tn), lambda l, n: (l, 0, n)),
            pl.BlockSpec((1, 1, tn), lambda l, n: (l, 0, n)),
        ],
        out_specs=pl.BlockSpec((1, MOD_ROWS, tn), lambda l, n: (l, 0, n)),
        out_shape=jax.ShapeDtypeStruct((depth, MOD_ROWS, n6), F32),
        compiler_params=_params(("parallel", "parallel"), VMEM_LIMIT),
        name="adaln",
    )(cc, w_ada, b_ada.reshape(depth, 1, n6))


def _modulated_norm(x, g, shift, scale):
    y = x * lax.rsqrt(jnp.mean(x * x, axis=-1, keepdims=True) + EPS) * g
    return y * (1.0 + scale) + shift


def _inproj_kernel(x_ref, mod_ref, g_ref, w_ref, z_ref):
    h = _modulated_norm(x_ref[0], g_ref[...], mod_ref[0, 0:1, :], mod_ref[0, 1:2, :])
    hb = h.astype(BF16)
    nw = INPROJ_COLS
    for n in range(IN_W // nw):
        z_ref[0, :, n * nw:(n + 1) * nw] = _mm(hb, w_ref[:, n * nw:(n + 1) * nw]).astype(BF16)


def _inproj(xs, mod, g, w_in_b, tm):
    nb, t, d = xs.shape
    return pl.pallas_call(
        _inproj_kernel,
        grid=(nb, t // tm),
        in_specs=[
            pl.BlockSpec((1, tm, d), lambda b, j: (b, j, 0)),
            pl.BlockSpec((1, 6, d), _mod_row(nb)),
            pl.BlockSpec((1, d), lambda b, j: (0, 0)),
            pl.BlockSpec((d, IN_W), lambda b, j: (0, 0)),
        ],
        out_specs=pl.BlockSpec((1, tm, IN_W), lambda b, j: (b, j, 0)),
        out_shape=jax.ShapeDtypeStruct((nb, t, IN_W), BF16),
        compiler_params=_params(("parallel", "parallel"), VMEM_LIMIT),
        name="inproj",
    )(xs, mod, g.reshape(1, d), w_in_b)


def _gmlp_kernel(u_ref, v_ref, lng_ref, lnb_ref, ws_ref, bs_ref, o_ref):
    u = jax.nn.gelu(u_ref[0].astype(F32))
    v = jax.nn.gelu(v_ref[0].astype(F32))
    vc = v - jnp.mean(v, axis=-1, keepdims=True)
    vn = vc * lax.rsqrt(jnp.mean(vc * vc, axis=-1, keepdims=True) + EPS) * lng_ref[...] + lnb_ref[...]
    vb = vn.astype(BF16)
    gw = CM_WIDTH // CM_GROUPS
    for c in range(u.shape[0] // CHUNK):
        rs = slice(c * CHUNK, (c + 1) * CHUNK)
        for g in range(CM_GROUPS):
            gs = slice(g * gw, (g + 1) * gw)
            s = _mm(ws_ref[g], vb[rs, gs]) + bs_ref[:, g:g + 1]
            o_ref[0, rs, gs] = (u[rs, gs] * s).astype(BF16)


def _gmlp(z, ln_g, ln_b, ws_b, bs_t):
    nb, t, _ = z.shape
    assert t % GMLP_ROWS == 0
    return pl.pallas_call(
        _gmlp_kernel,
        grid=(nb, t // GMLP_ROWS),
        in_specs=[
            pl.BlockSpec((1, GMLP_ROWS, CM_WIDTH), lambda b, j: (b, j, OFF_AU // CM_WIDTH)),
            pl.BlockSpec((1, GMLP_ROWS, CM_WIDTH), lambda b, j: (b, j, OFF_AV // CM_WIDTH)),
            pl.BlockSpec((1, CM_WIDTH), lambda b, j: (0, 0)),
            pl.BlockSpec((1, CM_WIDTH), lambda b, j: (0, 0)),
            pl.BlockSpec((CM_GROUPS, CHUNK, CHUNK), lambda b, j: (0, 0, 0)),
            pl.BlockSpec((CHUNK, CM_GROUPS), lambda b, j: (0, 0)),
        ],
        out_specs=pl.BlockSpec((1, GMLP_ROWS, CM_WIDTH), lambda b, j: (b, j, 0)),
        out_shape=jax.ShapeDtypeStruct((nb, t, CM_WIDTH), BF16),
        compiler_params=_params(("parallel", "parallel")),
        name="gmlp",
    )(z, z, ln_g.reshape(1, -1), ln_b.reshape(1, -1), ws_b, bs_t)


def _na_row_classes(rows):
    half = WIN_R // 2
    wr = min(WIN_R, rows)
    reps = list(range(half)) + [half] + list(range(rows - (wr - half) + 1, rows))
    cls = np.zeros((rows,), np.int32)
    for r in range(rows):
        if r < half:
            cls[r] = r
        elif r <= rows - (wr - half):
            cls[r] = half
        else:
            cls[r] = half + 1 + (r - (rows - (wr - half) + 1))
    return reps, cls


def _na_bias_table(rpb, rows):
    wr = min(WIN_R, rows)
    reps, _ = _na_row_classes(rows)
    nh = rpb.shape[0]
    qc = np.arange(GRID_W)[:, None]
    kc = np.arange(GRID_W)[None, :]
    col_start = np.clip(qc - WIN_C // 2, 0, GRID_W - WIN_C)
    valid = (kc >= col_start) & (kc < col_start + WIN_C)
    pad = GRID_W - WIN_C
    edge = jnp.concatenate([jnp.repeat(rpb[..., :1], pad, axis=-1), rpb.astype(F32),
                            jnp.repeat(rpb[..., -1:], pad, axis=-1)], axis=-1)
    toep = jnp.stack([edge[..., GRID_W - 1 - q:2 * GRID_W - 1 - q] for q in range(GRID_W)], axis=1)
    toep = jnp.where(valid[None, :, None, :], toep * LOG2E, NEG)
    tabs = []
    for r in reps:
        r0 = int(np.clip(r - WIN_R // 2, 0, rows - wr))
        lo = r0 - r + WIN_R - 1
        tabs.append(toep[:, :, lo:lo + wr, :].reshape(nh, GRID_W, wr * GRID_W))
    tab = jnp.stack(tabs)
    tab = tab.reshape(len(reps), nh // 2, 2, GRID_W, wr * GRID_W)
    return jnp.transpose(tab, (0, 1, 4, 2, 3)).reshape(len(reps), nh // 2, wr * GRID_W, 2 * GRID_W)


def _t_blocks(p):
    return jnp.concatenate([jnp.transpose(p[i:i + 128]) for i in range(0, p.shape[0], 128)],
                           axis=1).astype(BF16)


def _na_kernel(cls_ref, q_ref, k_ref, v_ref, *refs, lc, rows):
    del cls_ref
    j = pl.program_id(1)
    n_ctx_blk = lc // GRID_W
    wr = min(WIN_R, rows)
    sub = lax.broadcasted_iota(jnp.int32, (128, 128), 0)
    lane = lax.broadcasted_iota(jnp.int32, (128, 128), 1)
    same_head = (sub // 64) == (lane // 64)
    first_head = lax.broadcasted_iota(jnp.int32, (GRID_W, 128), 1) < 64
    scale = NA_HEAD_DIM ** -0.5 * LOG2E
    bias_refs, o_ref = refs[:-1], refs[-1]

    def attend(local, rr):
        qs = slice(rr * GRID_W, (rr + 1) * GRID_W)
        bias_ref = bias_refs[rr]
        q = q_ref[0, qs, :]
        if local:
            r0 = jnp.clip(j * NA_ROWS + rr - n_ctx_blk - WIN_R // 2, 0, rows - wr)
            start = pl.multiple_of(lc + r0 * GRID_W, GRID_W)
            k_loc = k_ref[0, pl.ds(start, wr * GRID_W), :]
            v_loc = v_ref[0, pl.ds(start, wr * GRID_W), :]
        k_ctx = k_ref[0, 0:lc, :]
        v_ctx = v_ref[0, 0:lc, :]
        for p in range(NA_HEADS // 2):
            cs = slice(p * 128, (p + 1) * 128)
            q2 = q[:, cs].astype(F32) * scale
            qbd = jnp.transpose(jnp.where(same_head, jnp.concatenate([q2, q2], axis=0), 0.0)).astype(BF16)
            s_ctx = _mm(k_ctx[:, cs], qbd)
            m = jnp.max(s_ctx, axis=0, keepdims=True)
            if local:
                s_loc = _mm(k_loc[:, cs], qbd) + bias_ref[0, p]
                m = jnp.maximum(m, jnp.max(s_loc, axis=0, keepdims=True))
            p_ctx = jnp.exp2(s_ctx - m)
            den = jnp.sum(p_ctx, axis=0, keepdims=True)
            if local:
                p_loc = jnp.exp2(s_loc - m)
                den = den + jnp.sum(p_loc, axis=0, keepdims=True)
            o = _mm(_t_blocks(p_ctx), v_ctx[:, cs])
            if local:
                o = o + _mm(_t_blocks(p_loc), v_loc[:, cs])
            o = o * jnp.transpose(jnp.broadcast_to(1.0 / den, (128, 128)))
            o_ref[0, qs, cs] = jnp.where(first_head, o[:GRID_W], o[GRID_W:]).astype(BF16)

    @pl.when(j * NA_ROWS < n_ctx_blk)
    def _():
        for rr in range(NA_ROWS):
            attend(False, rr)

    @pl.when(j * NA_ROWS >= n_ctx_blk)
    def _():
        for rr in range(NA_ROWS):
            attend(True, rr)


def _nattn(z, bias_tab, lc):
    nb, t, _ = z.shape
    rows = (t - lc) // GRID_W
    n_ctx_blk = lc // GRID_W
    _, cls = _na_row_classes(rows)
    cls_arr = jnp.asarray(np.concatenate([np.zeros((n_ctx_blk,), np.int32), cls]))
    wr = min(WIN_R, rows)
    qb = NA_ROWS * GRID_W
    assert n_ctx_blk % NA_ROWS == 0 and rows % NA_ROWS == 0

    def bias_spec(rr):
        return pl.BlockSpec((1, NA_HEADS // 2, wr * GRID_W, 2 * GRID_W),
                            lambda b, j, c: (c[j * NA_ROWS + rr], 0, 0, 0))

    grid_spec = pltpu.PrefetchScalarGridSpec(
        num_scalar_prefetch=1,
        grid=(nb, t // qb),
        in_specs=[
            pl.BlockSpec((1, qb, NA_WIDTH), lambda b, j, c: (b, j, OFF_NQ // NA_WIDTH)),
            pl.BlockSpec((1, t, NA_WIDTH), lambda b, j, c: (b, 0, OFF_NK // NA_WIDTH)),
            pl.BlockSpec((1, t, NA_WIDTH), lambda b, j, c: (b, 0, OFF_NV // NA_WIDTH)),
        ] + [bias_spec(rr) for rr in range(NA_ROWS)],
        out_specs=pl.BlockSpec((1, qb, NA_WIDTH), lambda b, j, c: (b, j, 0)),
    )
    return pl.pallas_call(
        functools.partial(_na_kernel, lc=lc, rows=rows),
        grid_spec=grid_spec,
        out_shape=jax.ShapeDtypeStruct((nb, t, NA_WIDTH), BF16),
        compiler_params=_params(("parallel", "arbitrary"), VMEM_LIMIT),
        name="nattn",
    )(cls_arr, z, z, z, *([bias_tab] * NA_ROWS))


def _rope_tables(lc, l):
    half = RET_DK // 2
    nf = half // 2
    tpos = np.arange(l)
    inv = ROPE_BASE ** (-jnp.arange(nf, dtype=F32) / nf)

    def part(pos):
        ang = jnp.asarray(pos, F32)[:, None] * inv[None, :]
        c, s = jnp.cos(ang), jnp.sin(ang)
        return jnp.concatenate([c, c], -1), jnp.concatenate([-s, s], -1)

    c_r, s_r = part(tpos // GRID_W)
    c_c, s_c = part(tpos % GRID_W)
    cos = jnp.concatenate([c_r, c_c], -1)
    sin = jnp.concatenate([s_r, s_c], -1)
    cos = jnp.concatenate([jnp.ones((lc, RET_DK), F32), cos], 0)
    sin = jnp.concatenate([jnp.zeros((lc, RET_DK), F32), sin], 0)
    return jnp.tile(cos, (1, RET_HEADS)), jnp.tile(sin, (1, RET_HEADS))


def _ret_kernel(dl_ref, *refs):
    st_ref = refs[-1]

    @pl.when(pl.program_id(1) == 0)
    def _():
        st_ref[...] = jnp.zeros_like(st_ref)

    for d in range(2):
        _ret_direction(d, dl_ref, *refs[5 * d:5 * d + 5], refs[10 + d], st_ref)


def _ret_direction(d, dl_ref, q_ref, k_ref, v_ref, cos_ref, sin_ref, o_ref, st_ref):
    cos = cos_ref[...]
    sin = sin_ref[...]
    lane_qk = lax.broadcasted_iota(jnp.int32, (CHUNK, RET_QK_W), 1)
    first = (lane_qk % (RET_DK // 2)) < (RET_DK // 4)

    def rope(a):
        a = a.astype(F32)
        partner = jnp.where(first, pltpu.roll(a, RET_QK_W - RET_DK // 4, axis=1),
                            pltpu.roll(a, RET_DK // 4, axis=1))
        return a * cos + partner * sin

    q = rope(q_ref[0])
    k = rope(k_ref[0]) * (RET_DK ** -0.5)
    v = v_ref[0]

    ri = lax.broadcasted_iota(jnp.int32, (CHUNK, 2 * CHUNK), 0)
    ci = lax.broadcasted_iota(jnp.int32, (CHUNK, 2 * CHUNK), 1) % CHUNK
    dist = (ri - ci) if d == 0 else (ci - ri)
    keep = dist >= d
    dist_f = jnp.maximum(dist, 0).astype(F32)
    q_exp = ((ri + 1) if d == 0 else (CHUNK - ri)).astype(F32)
    rk = lax.broadcasted_iota(jnp.int32, (CHUNK, CHUNK), 0)
    k_exp = ((CHUNK - 1 - rk) if d == 0 else rk).astype(F32)
    lane_half = lax.broadcasted_iota(jnp.int32, (CHUNK, CHUNK), 1) >= RET_DK
    row_half = rk >= RET_DK

    def log_gamma(h):
        xv = jnp.full((CHUNK, CHUNK), dl_ref[d, h], F32)
        return jnp.minimum(xv, 0.0) - jnp.log1p(jnp.exp(-jnp.abs(xv)))

    def by_head(m):
        return jnp.concatenate([jnp.where(row_half, 0.0, m), jnp.where(row_half, m, 0.0)], axis=1).astype(BF16)

    zero = jnp.zeros((CHUNK, RET_DV), BF16)
    for p in range(RET_HEADS // 2):
        cs = slice(p * 128, (p + 1) * 128)
        lg0, lg1 = log_gamma(2 * p), log_gamma(2 * p + 1)
        lg2 = jnp.concatenate([lg0, lg1], axis=1)
        q2b = q[:, cs].astype(BF16)
        k2 = k[:, cs]
        kd_t = jnp.transpose(k2 * jnp.exp(k_exp * jnp.where(lane_half, lg1, lg0)))
        v0 = v[:, 2 * p * RET_DV:(2 * p + 1) * RET_DV]
        v1 = v[:, (2 * p + 1) * RET_DV:(2 * p + 2) * RET_DV]
        s2 = st_ref[d, p]
        a = _mm(q2b, by_head(jnp.transpose(k2)))
        a = jnp.where(keep, a * jnp.exp(dist_f * lg2), 0.0)
        v_diag = jnp.concatenate([jnp.concatenate([v0, zero], axis=1),
                                  jnp.concatenate([zero, v1], axis=1)], axis=0)
        inner = _mm(a.astype(BF16), v_diag)
        cross = _mm(q2b, by_head(s2)) * jnp.exp(q_exp * lg2)
        o_ref[0, :, 2 * p * RET_DV:(2 * p + 2) * RET_DV] = (inner + cross).astype(o_ref.dtype)
        st_ref[d, p] = (s2 * jnp.exp(float(CHUNK) * jnp.where(row_half, lg1, lg0))
                        + _mm(by_head(kd_t), jnp.concatenate([v0, v1], axis=0)))


def _retention(z, decay_logit, cos_t, sin_t, lc):
    nb, t, _ = z.shape
    nc = t // CHUNK
    ncc = lc // CHUNK

    def chunk(d, s):
        return s if d == 0 else jnp.where(s < ncc, ncc - 1 - s, nc - 1 - (s - ncc))

    def in_specs(d):
        return [
            pl.BlockSpec((1, CHUNK, RET_QK_W), lambda b, s, dl: (b, chunk(d, s), OFF_RQ // RET_QK_W)),
            pl.BlockSpec((1, CHUNK, RET_QK_W), lambda b, s, dl: (b, chunk(d, s), OFF_RK // RET_QK_W)),
            pl.BlockSpec((1, CHUNK, RET_V_W), lambda b, s, dl: (b, chunk(d, s), OFF_RV // RET_V_W)),
            pl.BlockSpec((CHUNK, RET_QK_W), lambda b, s, dl: (chunk(d, s), 0)),
            pl.BlockSpec((CHUNK, RET_QK_W), lambda b, s, dl: (chunk(d, s), 0)),
        ]

    def out_spec(d):
        return pl.BlockSpec((1, CHUNK, RET_V_W), lambda b, s, dl: (b, chunk(d, s), 0))

    grid_spec = pltpu.PrefetchScalarGridSpec(
        num_scalar_prefetch=1,
        grid=(nb, nc),
        in_specs=in_specs(0) + in_specs(1),
        out_specs=[out_spec(0), out_spec(1)],
        scratch_shapes=[pltpu.VMEM((2, RET_HEADS // 2, 2 * RET_DK, RET_DV), F32)],
    )
    return pl.pallas_call(
        _ret_kernel,
        grid_spec=grid_spec,
        out_shape=[jax.ShapeDtypeStruct((nb, t, RET_V_W), BF16)] * 2,
        compiler_params=_params(("parallel", "arbitrary")),
        name="retention",
    )(decay_logit, *([z, z, z, cos_t, sin_t] * 2))


def _merge_kernel(a_ref, b_ref, rf_ref, rb_ref, rg_ref, ga_ref, gb_ref, gc_ref, x_ref, mod_ref,
                  wb_ref, wo_ref, o_ref):
    o = rf_ref[0].astype(F32) + rb_ref[0].astype(F32)
    parts = []
    for h in range(RET_HEADS):
        oh = o[:, h * RET_DV:(h + 1) * RET_DV]
        parts.append(oh * lax.rsqrt(jnp.mean(oh * oh, axis=-1, keepdims=True) + EPS))
    c = jnp.concatenate(parts, axis=-1) * _silu(rg_ref[0].astype(F32))
    m = (jax.nn.sigmoid(ga_ref[0].astype(F32)) * _mm(a_ref[0], wb_ref[0])
         + jax.nn.sigmoid(gb_ref[0].astype(F32)) * _mm(b_ref[0], wb_ref[1])
         + jax.nn.sigmoid(gc_ref[0].astype(F32)) * _mm(c.astype(BF16), wb_ref[2]))
    mix = _mm(m.astype(BF16), wo_ref[...])
    o_ref[0] = x_ref[0] + mod_ref[0, 2:3, :] * mix


def _merge(a_lat, b_lat, ret, z, xs, mod, wb_b, wo_b, tm):
    nb, t, d = xs.shape
    bw = CM_WIDTH
    return pl.pallas_call(
        _merge_kernel,
        grid=(nb, t // tm),
        in_specs=[
            pl.BlockSpec((1, tm, bw), lambda b, j: (b, j, 0)),
            pl.BlockSpec((1, tm, bw), lambda b, j: (b, j, 0)),
            pl.BlockSpec((1, tm, bw), lambda b, j: (b, j, 0)),
            pl.BlockSpec((1, tm, bw), lambda b, j: (b, j, 0)),
            pl.BlockSpec((1, tm, RET_V_W), lambda b, j: (b, j, OFF_RG // RET_V_W)),
            pl.BlockSpec((1, tm, d), lambda b, j: (b, j, OFF_GA // d)),
            pl.BlockSpec((1, tm, d), lambda b, j: (b, j, OFF_GB // d)),
            pl.BlockSpec((1, tm, d), lambda b, j: (b, j, OFF_GC // d)),
            pl.BlockSpec((1, tm, d), lambda b, j: (b, j, 0)),
            pl.BlockSpec((1, 6, d), _mod_row(nb)),
            pl.BlockSpec((3, bw, d), lambda b, j: (0, 0, 0)),
            pl.BlockSpec((d, d), lambda b, j: (0, 0)),
        ],
        out_specs=pl.BlockSpec((1, tm, d), lambda b, j: (b, j, 0)),
        out_shape=jax.ShapeDtypeStruct((nb, t, d), F32),
        compiler_params=_params(("parallel", "parallel"), VMEM_LIMIT),
        name="merge",
    )(a_lat, b_lat, ret[0], ret[1], z, z, z, z, xs, mod, wb_b, wo_b)


ROW_SLAB = D_MODEL // 128


def _slab_load(ref, lead, n, row0=0):
    return jnp.concatenate(
        [ref[lead + (pl.ds(row0 * ROW_SLAB + j, n, stride=ROW_SLAB), slice(None))] for j in range(ROW_SLAB)],
        axis=-1)


def _slab_store(ref, lead, val, row0=0):
    for j in range(ROW_SLAB):
        ref[lead + (pl.ds(row0 * ROW_SLAB + j, val.shape[0], stride=ROW_SLAB), slice(None))] = (
            val[:, j * 128:(j + 1) * 128])


def _first_index(mask, idx, big):
    return jnp.min(jnp.where(mask, idx, big), axis=0, keepdims=True)


def _router_kernel(x_ref, mod_ref, g_ref, rwt_ref, rb_ref, h_ref, ei_ref, wk_ref):
    h = _modulated_norm(x_ref[0], g_ref[...], mod_ref[0, 3:4, :], mod_ref[0, 4:5, :])
    _slab_store(h_ref, (0,), h)
    tm = h.shape[0]
    logits = _mm_nt(rwt_ref[...], h)
    scores = jax.nn.sigmoid(logits)
    biased = scores + rb_ref[:, 0:1]
    gsz = N_EXPERTS // N_GROUPS
    sub = lax.broadcasted_iota(jnp.int32, (gsz, tm), 0)
    gscore = []
    for g in range(N_GROUPS):
        blk = biased[g * gsz:(g + 1) * gsz]
        m1 = jnp.max(blk, axis=0, keepdims=True)
        i1 = _first_index(blk == m1, sub, gsz)
        m2 = jnp.max(jnp.where(sub == i1, -jnp.inf, blk), axis=0, keepdims=True)
        gscore.append(m1 + m2)
    gs = jnp.concatenate(gscore, axis=0)
    gidx = lax.broadcasted_iota(jnp.int32, (N_GROUPS, tm), 0)
    gsel = jnp.zeros((N_GROUPS, tm), jnp.int32)
    for _ in range(TOPK_GROUPS):
        m = jnp.max(gs, axis=0, keepdims=True)
        pick = gidx == _first_index(gs == m, gidx, N_GROUPS)
        gsel = jnp.where(pick, 1, gsel)
        gs = jnp.where(pick, -jnp.inf, gs)
    cand = jnp.concatenate(
        [jnp.where(gsel[g:g + 1] > 0, biased[g * gsz:(g + 1) * gsz], -jnp.inf) for g in range(N_GROUPS)],
        axis=0)
    eidx = lax.broadcasted_iota(jnp.int32, (N_EXPERTS, tm), 0)
    picks, ws = [], []
    for _ in range(TOP_K):
        m = jnp.max(cand, axis=0, keepdims=True)
        first = _first_index(cand == m, eidx, N_EXPERTS)
        pick = eidx == first
        picks.append(first)
        ws.append(jnp.sum(jnp.where(pick, scores, 0.0), axis=0, keepdims=True))
        cand = jnp.where(pick, -jnp.inf, cand)
    w = jnp.concatenate(ws, axis=0)
    ei_ref[0] = jnp.concatenate(picks, axis=0)
    wk_ref[0] = w / jnp.sum(w, axis=0, keepdims=True) * ROUTED_SCALE


def _router(xs, mod, g, rw_t, rb, tm):
    nb, t, d = xs.shape
    return pl.pallas_call(
        _router_kernel,
        grid=(nb, t // tm),
        in_specs=[
            pl.BlockSpec((1, tm, d), lambda b, j: (b, j, 0)),
            pl.BlockSpec((1, 6, d), _mod_row(nb)),
            pl.BlockSpec((1, d), lambda b, j: (0, 0)),
            pl.BlockSpec((N_EXPERTS, d), lambda b, j: (0, 0)),
            pl.BlockSpec((N_EXPERTS, 1), lambda b, j: (0, 0)),
        ],
        out_specs=[
            pl.BlockSpec((1, tm * ROW_SLAB, 128), lambda b, j: (b, j, 0)),
            pl.BlockSpec((1, TOP_K, tm), lambda b, j: (b, 0, j)),
            pl.BlockSpec((1, TOP_K, tm), lambda b, j: (b, 0, j)),
        ],
        out_shape=[jax.ShapeDtypeStruct((nb, t * ROW_SLAB, 128), F32),
                   jax.ShapeDtypeStruct((nb, TOP_K, t), jnp.int32), jax.ShapeDtypeStruct((nb, TOP_K, t), F32)],
        compiler_params=_params(("parallel", "parallel")),
        name="router",
    )(xs, mod, g.reshape(1, d), rw_t, rb.reshape(N_EXPERTS, 1))


def _dispatch(eidx, wk):
    nb, k, t = eidx.shape
    pos = lax.broadcasted_iota(jnp.int32, eidx.shape, 2)
    skey, sw = lax.sort(((eidx * t + pos).reshape(nb, k * t), wk.reshape(nb, k * t)), dimension=1, num_keys=1)
    slab_row = (skey % t) * ROW_SLAB
    cnt = jnp.sum(eidx[:, None] == jnp.arange(N_EXPERTS, dtype=jnp.int32)[None, :, None, None],
                  axis=(2, 3), dtype=jnp.int32)
    off = jnp.cumsum(cnt, axis=1) - cnt
    pad = ((0, 0), (0, MOE_LIST_PAD))
    return jnp.pad(slab_row, pad), jnp.pad(sw, pad), cnt, off


def _slab(row):
    return pl.ds(pl.multiple_of(row, ROW_SLAB), ROW_SLAB)


def _expert_ffn(base, w_ref, stage_ref, ostage_ref, wg_ref, wu_ref, wd_ref, k):
    x = _slab_load(stage_ref, (), MOE_TILE, k * MOE_TILE).astype(BF16)
    act = _silu(_mm(x, wg_ref[k])) * _mm(x, wu_ref[k])
    aligned = pl.multiple_of((base // 128) * 128, 128)
    wrow = pltpu.roll(w_ref[0, :, pl.ds(aligned, MOE_WIN)], (MOE_WIN - (base - aligned)) % MOE_WIN, axis=1)
    wcol = jnp.concatenate(
        [jnp.transpose(jnp.broadcast_to(wrow[:, k * 128:(k + 1) * 128], (128, 128)))
         for k in range(MOE_WIN // 128 - 1)], axis=0)[:MOE_TILE]
    act = act * jnp.concatenate([wcol] * (EXPERT_W // 128), axis=-1)
    _slab_store(ostage_ref, (), _mm(act.astype(BF16), wd_ref[k]), k * MOE_TILE)


def _experts_kernel(cnt_ref, off_ref, idx_ref, w_ref, src_ref, wg_ref, wu_ref, wd_ref, acc_ref,
                    stage0, stage1, ostage0, ostage1, *, ne, spare):
    b = pl.program_id(0)
    s = pl.program_id(1)
    ns = ne // MOE_EPS

    def expert(stage, k):
        return b * ne + stage * MOE_EPS + k

    st_g = jnp.minimum(s, ns - 1)
    st_m = jnp.clip(s - 1, 0, ns - 1)
    st_s = jnp.maximum(s - 2, 0)
    ffn_live = (s >= 1) & (s <= ns)
    start_g = [off_ref[expert(st_g, k)] for k in range(MOE_EPS)]
    start_m = [off_ref[expert(st_m, k)] for k in range(MOE_EPS)]
    count_m = [jnp.where(ffn_live, cnt_ref[expert(st_m, k)], 0) for k in range(MOE_EPS)]
    start_s = [off_ref[expert(st_s, k)] for k in range(MOE_EPS)]
    count_s = [jnp.where(s >= 2, jnp.minimum(cnt_ref[expert(st_s, k)], MOE_TILE), 0) for k in range(MOE_EPS)]

    @pl.when(s == 0)
    def _():
        acc_ref[...] = jnp.zeros_like(acc_ref)
        stage1[...] = jnp.zeros_like(stage1)
        ostage0[...] = jnp.zeros_like(ostage0)

    def step(stage_w, stage_r, ostage_w, ostage_r):
        for k in range(MOE_EPS):
            for r in range(MOE_TILE):
                stage_w[_slab((k * MOE_TILE + r) * ROW_SLAB), :] = (
                    src_ref[0, _slab(idx_ref[0, 0, start_g[k] + r]), :])
        for k in range(MOE_EPS):
            _expert_ffn(start_m[k], w_ref, stage_r, ostage_w, wg_ref, wu_ref, wd_ref, k)
        for k in range(MOE_EPS):
            for g in range(MOE_TILE // MOE_GROUP):
                rws = range(g * MOE_GROUP, (g + 1) * MOE_GROUP)
                dst = [_slab(jnp.where(r < count_s[k], idx_ref[0, 0, start_s[k] + r], spare)) for r in rws]
                vals = [acc_ref[0, d, :] + ostage_r[_slab((k * MOE_TILE + r) * ROW_SLAB), :]
                        for d, r in zip(dst, rws)]
                for d, v in zip(dst, vals):
                    acc_ref[0, d, :] = v

    @pl.when(s % 2 == 0)
    def _():
        step(stage0, stage1, ostage1, ostage0)

    @pl.when(s % 2 == 1)
    def _():
        step(stage1, stage0, ostage0, ostage1)

    def overflow(stage_x, ostage_x, k):
        def tile(ti, carry):
            base = start_m[k] + ti * MOE_TILE
            rows = jnp.minimum(count_m[k] - ti * MOE_TILE, MOE_TILE)

            def gather(r, c):
                stage_x[_slab((k * MOE_TILE + r) * ROW_SLAB), :] = src_ref[0, _slab(idx_ref[0, 0, base + r]), :]
                return c

            def scatter(r, c):
                d = _slab(idx_ref[0, 0, base + r])
                acc_ref[0, d, :] = acc_ref[0, d, :] + ostage_x[_slab((k * MOE_TILE + r) * ROW_SLAB), :]
                return c

            lax.fori_loop(0, rows, gather, 0)
            _expert_ffn(base, w_ref, stage_x, ostage_x, wg_ref, wu_ref, wd_ref, k)
            lax.fori_loop(0, rows, scatter, 0)
            return carry

        lax.fori_loop(1, (count_m[k] + MOE_TILE - 1) // MOE_TILE, tile, 0)

    for k in range(MOE_EPS):
        @pl.when((count_m[k] > MOE_TILE) & (s % 2 == 0))
        def _(k=k):
            overflow(stage1, ostage0, k)

        @pl.when((count_m[k] > MOE_TILE) & (s % 2 == 1))
        def _(k=k):
            overflow(stage0, ostage1, k)


def _experts(h2, slab_row, sw, cnt, off, wg, wu, wd, layer):
    nb = h2.shape[0]
    t = h2.shape[1] // ROW_SLAB
    ne = cnt.shape[1]
    _, d, ew = wg.shape
    n_list = slab_row.shape[1]
    assert h2.shape == (nb, t * ROW_SLAB, 128) and d == D_MODEL and ew == EXPERT_W and MOE_TILE % 16 == 0
    assert MOE_LIST_PAD >= MOE_WIN and ne % MOE_EPS == 0
    ns = ne // MOE_EPS
    ffn = lambda b, s, c, o: (layer * ns + jnp.clip(s - 1, 0, ns - 1), 0, 0)
    acc_rows = (t + 1) * ROW_SLAB
    grid_spec = pltpu.PrefetchScalarGridSpec(
        num_scalar_prefetch=2,
        grid=(nb, ns + 2),
        in_specs=[
            pl.BlockSpec((1, 1, n_list), lambda b, s, c, o: (b, 0, 0), memory_space=pltpu.SMEM),
            pl.BlockSpec((1, 1, n_list), lambda b, s, c, o: (b, 0, 0)),
            pl.BlockSpec((1, t * ROW_SLAB, 128), lambda b, s, c, o: (b, 0, 0)),
            pl.BlockSpec((MOE_EPS, d, ew), ffn),
            pl.BlockSpec((MOE_EPS, d, ew), ffn),
            pl.BlockSpec((MOE_EPS, ew, d), ffn),
        ],
        out_specs=pl.BlockSpec((1, acc_rows, 128), lambda b, s, c, o: (b, 0, 0)),
        scratch_shapes=[pltpu.VMEM((MOE_EPS * MOE_TILE * ROW_SLAB, 128), F32)] * 4,
    )
    return pl.pallas_call(
        functools.partial(_experts_kernel, ne=ne, spare=t * ROW_SLAB),
        grid_spec=grid_spec,
        out_shape=jax.ShapeDtypeStruct((nb, acc_rows, 128), F32),
        compiler_params=_params(("parallel", "arbitrary"), VMEM_LIMIT),
        name="experts",
    )(cnt.reshape(nb * ne), off.reshape(nb * ne), slab_row.reshape(nb, 1, n_list), sw.reshape(nb, 1, n_list),
      h2, wg, wu, wd)


def _moe_out_kernel(h_ref, r_ref, x_ref, mod_ref, sg_ref, su_ref, sd_ref, *rest):
    tm = x_ref.shape[1]
    h = _slab_load(h_ref, (0,), tm).astype(BF16)
    shared = _mm((_silu(_mm(h, sg_ref[...])) * _mm(h, su_ref[...])).astype(BF16), sd_ref[...])
    y = x_ref[0] + mod_ref[0, 5:6, :] * (_slab_load(r_ref, (0,), tm) + shared)
    if len(rest) == 2:
        y = y * lax.rsqrt(jnp.mean(y * y, axis=-1, keepdims=True) + EPS) * rest[0][...]
    rest[-1][0] = y


def _moe_out(h2, routed, xs, mod, sg, su, sd, tm, skip=0, final_g=None):
    nb, t, d = xs.shape
    tok = lambda b, j: (b, j + skip, 0)
    in_specs = [
        pl.BlockSpec((1, tm * ROW_SLAB, 128), tok),
        pl.BlockSpec((1, tm * ROW_SLAB, 128), tok),
        pl.BlockSpec((1, tm, d), tok),
        pl.BlockSpec((1, 6, d), (lambda b, j: (b, 0, 0)) if skip else _mod_row(nb)),
        pl.BlockSpec(sg.shape, lambda b, j: (0, 0)),
        pl.BlockSpec(su.shape, lambda b, j: (0, 0)),
        pl.BlockSpec(sd.shape, lambda b, j: (0, 0)),
    ]
    args = [h2, routed, xs, mod, sg, su, sd]
    if final_g is not None:
        in_specs.append(pl.BlockSpec((1, d), lambda b, j: (0, 0)))
        args.append(final_g.reshape(1, d))
    return pl.pallas_call(
        _moe_out_kernel,
        grid=(nb, t // tm - skip),
        in_specs=in_specs,
        out_specs=pl.BlockSpec((1, tm, d), lambda b, j: (b, j, 0)),
        out_shape=jax.ShapeDtypeStruct((nb, t - skip * tm, d), F32),
        compiler_params=_params(("parallel", "parallel")),
        name="moe_out",
    )(*args)


def kernel(x, c, ctx, c_ctx, w_ada, b_ada, norm_mix_g, norm_ffn_g, w_in, cm_ln_g, cm_ln_b, cm_w_s, cm_b_s,
           na_rpb, ret_decay_logit, w_branch, w_out, router_w, router_bias, exp_w_gate, exp_w_up,
           exp_w_down, sh_w_gate, sh_w_up, sh_w_down, final_g):
    nb, l, d = x.shape
    lc = ctx.shape[1]
    depth = w_ada.shape[0]
    assert d == D_MODEL and nb < MOD_ROWS and l % GRID_W == 0 and lc % CHUNK == 0 and l % CHUNK == 0
    tm = TOKEN_TILE
    assert lc % tm == 0 and l % tm == 0

    cc = jnp.zeros((MOD_ROWS, d), F32).at[:nb].set(c).at[nb].set(c_ctx)
    mod_all = _adaln(cc, w_ada, b_ada).reshape(depth, MOD_ROWS, 6, d)
    cos_t, sin_t = _rope_tables(lc, l)
    xs = jnp.concatenate([ctx, x], axis=1)
    wg_all, wu_all, wd_all = (_to_bf16(w).reshape((-1,) + w.shape[2:])
                              for w in (exp_w_gate, exp_w_up, exp_w_down))

    for layer in range(depth):
        mod = mod_all[layer]
        z = _inproj(xs, mod, norm_mix_g[layer], w_in[layer].astype(BF16), tm)
        a_lat = _gmlp(z, cm_ln_g[layer], cm_ln_b[layer], cm_w_s[layer].astype(BF16),
                      jnp.transpose(cm_b_s[layer]))
        b_lat = _nattn(z, _na_bias_table(na_rpb[layer], l // GRID_W), lc)
        ret = _retention(z, ret_decay_logit[layer], cos_t, sin_t, lc)
        xs = _merge(a_lat, b_lat, ret, z, xs, mod, w_branch[layer].astype(BF16),
                    w_out[layer].astype(BF16), tm)
        h2, eidx, wk = _router(xs, mod, norm_ffn_g[layer], jnp.transpose(router_w[layer]),
                               router_bias[layer], tm)
        slab_row, sw, cnt, off = _dispatch(eidx, wk)
        routed = _experts(h2, slab_row, sw, cnt, off, wg_all, wu_all, wd_all, layer)
        last = layer == depth - 1
        xs = _moe_out(h2, routed, xs, mod, sh_w_gate[layer].astype(BF16), sh_w_up[layer].astype(BF16),
                      sh_w_down[layer].astype(BF16), tm, skip=lc // tm if last else 0,
                      final_g=final_g if last else None)
    return xs
```

```python
import functools

import jax
import jax.numpy as jnp
import numpy as np
from jax import lax
from jax.experimental import pallas as pl
from jax.experimental.pallas import tpu as pltpu

F32 = jnp.float32
BF16 = jnp.bfloat16

D_MODEL = 1024
GRID_W = 64
CHUNK = 128
CM_GROUPS = 4
CM_WIDTH = 512
NA_HEADS = 8
NA_HEAD_DIM = 64
NA_WIDTH = NA_HEADS * NA_HEAD_DIM
WIN_R = 8
WIN_C = 16
RET_HEADS = 4
RET_DK = 64
RET_DV = 128
RET_QK_W = RET_HEADS * RET_DK
RET_V_W = RET_HEADS * RET_DV
ROPE_BASE = 10000.0
IN_W = 2 * CM_WIDTH + 3 * NA_WIDTH + 2 * RET_QK_W + 2 * RET_V_W + 3 * D_MODEL
N_EXPERTS = 64
TOP_K = 8
N_GROUPS = 8
TOPK_GROUPS = 4
EXPERT_W = 256
ROUTED_SCALE = 2.5
EPS = 1e-6
NEG = -1e30
LOG2E = 1.4426950408889634

OFF_AU, OFF_AV = 0, 512
OFF_NQ, OFF_NK, OFF_NV = 1024, 1536, 2048
OFF_RQ, OFF_RK, OFF_RV, OFF_RG = 2560, 2816, 3072, 3584
OFF_GA, OFF_GB, OFF_GC = 4096, 5120, 6144

MOD_ROWS = 16
VMEM_LIMIT = 56 * 1024 * 1024

TOKEN_TILE = 256
ADALN_COLS = 1536
INPROJ_COLS = 512
GMLP_ROWS = 6 * CHUNK
NA_ROWS = 4
MOE_TILE = 320
MOE_GROUP = 8
MOE_EPS = 1
MOE_WIN = (MOE_TILE + 127) // 128 * 128 + 128
MOE_LIST_PAD = 1024
CAST_BLOCK_BYTES = 8 * 1024 * 1024


def _params(sem, vmem=None):
    return pltpu.CompilerParams(dimension_semantics=sem, vmem_limit_bytes=vmem)


def _silu(v):
    return v * jax.nn.sigmoid(v)


def _mm(a, b):
    return jnp.dot(a, b, preferred_element_type=F32)


def _mm_nt(a, b):
    return lax.dot_general(a, b, (((1,), (1,)), ((), ())), preferred_element_type=F32)


def _mm_tn(a, b):
    return lax.dot_general(a, b, (((0,), (0,)), ((), ())), preferred_element_type=F32)


def _cast_kernel(x_ref, o_ref):
    o_ref[...] = x_ref[...].astype(BF16)


def _to_bf16(w):
    cols = w.shape[-1]
    flat = w.reshape(-1, cols)
    rows = CAST_BLOCK_BYTES // (4 * cols)
    assert flat.shape[0] % rows == 0
    out = pl.pallas_call(
        _cast_kernel,
        grid=(flat.shape[0] // rows,),
        in_specs=[pl.BlockSpec((rows, cols), lambda i: (i, 0))],
        out_specs=pl.BlockSpec((rows, cols), lambda i: (i, 0)),
        out_shape=jax.ShapeDtypeStruct(flat.shape, BF16),
        compiler_params=_params(("parallel",), VMEM_LIMIT),
        name="to_bf16",
    )(flat)
    return out.reshape(w.shape)


def _mod_row(nb):
    return lambda b, j, *_: (jnp.where(j == 0, nb, b), 0, 0)


def _adaln_kernel(c_ref, w_ref, b_ref, o_ref):
    s = _silu(c_ref[...])
    o_ref[0] = _mm(s, w_ref[0]) + b_ref[0]


def _adaln(cc, w_ada, b_ada):
    depth, d, n6 = w_ada.shape
    tn = ADALN_COLS
    return pl.pallas_call(
        _adaln_kernel,
        grid=(depth, n6 // tn),
        in_specs=[
            pl.BlockSpec((MOD_ROWS, d), lambda l, n: (0, 0)),
            pl.BlockSpec((1, d, tn), lambda l, n: (l, 0, n)),
            pl.BlockSpec((1, 1, tn), lambda l, n: (l, 0, n)),
        ],
        out_specs=pl.BlockSpec((1, MOD_ROWS, tn), lambda l, n: (l, 0, n)),
        out_shape=jax.ShapeDtypeStruct((depth, MOD_ROWS, n6), F32),
        compiler_params=_params(("parallel", "parallel"), VMEM_LIMIT),
        name="adaln",
    )(cc, w_ada, b_ada.reshape(depth, 1, n6))


def _modulated_norm(x, g, shift, scale):
    y = x * lax.rsqrt(jnp.mean(x * x, axis=-1, keepdims=True) + EPS) * g
    return y * (1.0 + scale) + shift


def _inproj_kernel(x_ref, mod_ref, g_ref, w_ref, z_ref):
    h = _modulated_norm(x_ref[0], g_ref[...], mod_ref[0, 0:1, :], mod_ref[0, 1:2, :])
    hb = h.astype(BF16)
    nw = INPROJ_COLS
    for n in range(IN_W // nw):
        z_ref[0, :, n * nw:(n + 1) * nw] = _mm(hb, w_ref[:, n * nw:(n + 1) * nw]).astype(BF16)


def _inproj(xs, mod, g, w_in_b, tm):
    nb, t, d = xs.shape
    return pl.pallas_call(
        _inproj_kernel,
        grid=(nb, t // tm),
        in_specs=[
            pl.BlockSpec((1, tm, d), lambda b, j: (b, j, 0)),
            pl.BlockSpec((1, 6, d), _mod_row(nb)),
            pl.BlockSpec((1, d), lambda b, j: (0, 0)),
            pl.BlockSpec((d, IN_W), lambda b, j: (0, 0)),
        ],
        out_specs=pl.BlockSpec((1, tm, IN_W), lambda b, j: (b, j, 0)),
        out_shape=jax.ShapeDtypeStruct((nb, t, IN_W), BF16),
        compiler_params=_params(("parallel", "parallel"), VMEM_LIMIT),
        name="inproj",
    )(xs, mod, g.reshape(1, d), w_in_b)


def _gmlp_kernel(u_ref, v_ref, lng_ref, lnb_ref, ws_ref, bs_ref, o_ref):
    u = jax.nn.gelu(u_ref[0].astype(F32))
    v = jax.nn.gelu(v_ref[0].astype(F32))
    vc = v - jnp.mean(v, axis=-1, keepdims=True)
    vn = vc * lax.rsqrt(jnp.mean(vc * vc, axis=-1, keepdims=True) + EPS) * lng_ref[...] + lnb_ref[...]
    vb = vn.astype(BF16)
    gw = CM_WIDTH // CM_GROUPS
    for c in range(u.shape[0] // CHUNK):
        rs = slice(c * CHUNK, (c + 1) * CHUNK)
        for g in range(CM_GROUPS):
            gs = slice(g * gw, (g + 1) * gw)
            s = _mm(ws_ref[g], vb[rs, gs]) + bs_ref[:, g:g + 1]
            o_ref[0, rs, gs] = (u[rs, gs] * s).astype(BF16)


def _gmlp(z, ln_g, ln_b, ws_b, bs_t):
    nb, t, _ = z.shape
    assert t % GMLP_ROWS == 0
    return pl.pallas_call(
        _gmlp_kernel,
        grid=(nb, t // GMLP_ROWS),
        in_specs=[
            pl.BlockSpec((1, GMLP_ROWS, CM_WIDTH), lambda b, j: (b, j, OFF_AU // CM_WIDTH)),
            pl.BlockSpec((1, GMLP_ROWS, CM_WIDTH), lambda b, j: (b, j, OFF_AV // CM_WIDTH)),
            pl.BlockSpec((1, CM_WIDTH), lambda b, j: (0, 0)),
            pl.BlockSpec((1, CM_WIDTH), lambda b, j: (0, 0)),
            pl.BlockSpec((CM_GROUPS, CHUNK, CHUNK), lambda b, j: (0, 0, 0)),
            pl.BlockSpec((CHUNK, CM_GROUPS), lambda b, j: (0, 0)),
        ],
        out_specs=pl.BlockSpec((1, GMLP_ROWS, CM_WIDTH), lambda b, j: (b, j, 0)),
        out_shape=jax.ShapeDtypeStruct((nb, t, CM_WIDTH), BF16),
        compiler_params=_params(("parallel", "parallel")),
        name="gmlp",
    )(z, z, ln_g.reshape(1, -1), ln_b.reshape(1, -1), ws_b, bs_t)


def _na_row_classes(rows):
    half = WIN_R // 2
    wr = min(WIN_R, rows)
    reps = list(range(half)) + [half] + list(range(rows - (wr - half) + 1, rows))
    cls = np.zeros((rows,), np.int32)
    for r in range(rows):
        if r < half:
            cls[r] = r
        elif r <= rows - (wr - half):
            cls[r] = half
        else:
            cls[r] = half + 1 + (r - (rows - (wr - half) + 1))
    return reps, cls


def _na_bias_table(rpb, rows):
    wr = min(WIN_R, rows)
    reps, _ = _na_row_classes(rows)
    nh = rpb.shape[0]
    qc = np.arange(GRID_W)[:, None]
    kc = np.arange(GRID_W)[None, :]
    col_start = np.clip(qc - WIN_C // 2, 0, GRID_W - WIN_C)
    valid = (kc >= col_start) & (kc < col_start + WIN_C)
    pad = GRID_W - WIN_C
    edge = jnp.concatenate([jnp.repeat(rpb[..., :1], pad, axis=-1), rpb.astype(F32),
                            jnp.repeat(rpb[..., -1:], pad, axis=-1)], axis=-1)
    toep = jnp.stack([edge[..., GRID_W - 1 - q:2 * GRID_W - 1 - q] for q in range(GRID_W)], axis=1)
    toep = jnp.where(valid[None, :, None, :], toep * LOG2E, NEG)
    tabs = []
    for r in reps:
        r0 = int(np.clip(r - WIN_R // 2, 0, rows - wr))
        lo = r0 - r + WIN_R - 1
        tabs.append(toep[:, :, lo:lo + wr, :].reshape(nh, GRID_W, wr * GRID_W))
    tab = jnp.stack(tabs)
    tab = tab.reshape(len(reps), nh // 2, 2, GRID_W, wr * GRID_W)
    return jnp.transpose(tab, (0, 1, 4, 2, 3)).reshape(len(reps), nh // 2, wr * GRID_W, 2 * GRID_W)


def _t_blocks(p):
    return jnp.concatenate([jnp.transpose(p[i:i + 128]) for i in range(0, p.shape[0], 128)],
                           axis=1).astype(BF16)


def _na_kernel(cls_ref, q_ref, k_ref, v_ref, *refs, lc, rows):
    del cls_ref
    j = pl.program_id(1)
    n_ctx_blk = lc // GRID_W
    wr = min(WIN_R, rows)
    sub = lax.broadcasted_iota(jnp.int32, (128, 128), 0)
    lane = lax.broadcasted_iota(jnp.int32, (128, 128), 1)
    same_head = (sub // 64) == (lane // 64)
    first_head = lax.broadcasted_iota(jnp.int32, (GRID_W, 128), 1) < 64
    scale = NA_HEAD_DIM ** -0.5 * LOG2E
    bias_refs, o_ref = refs[:-1], refs[-1]

    def attend(local, rr):
        qs = slice(rr * GRID_W, (rr + 1) * GRID_W)
        bias_ref = bias_refs[rr]
        q = q_ref[0, qs, :]
        if local:
            r0 = jnp.clip(j * NA_ROWS + rr - n_ctx_blk - WIN_R // 2, 0, rows - wr)
            start = pl.multiple_of(lc + r0 * GRID_W, GRID_W)
            k_loc = k_ref[0, pl.ds(start, wr * GRID_W), :]
            v_loc = v_ref[0, pl.ds(start, wr * GRID_W), :]
        k_ctx = k_ref[0, 0:lc, :]
        v_ctx = v_ref[0, 0:lc, :]
        for p in range(NA_HEADS // 2):
            cs = slice(p * 128, (p + 1) * 128)
            q2 = q[:, cs].astype(F32) * scale
            qbd = jnp.transpose(jnp.where(same_head, jnp.concatenate([q2, q2], axis=0), 0.0)).astype(BF16)
            s_ctx = _mm(k_ctx[:, cs], qbd)
            m = jnp.max(s_ctx, axis=0, keepdims=True)
            if local:
                s_loc = _mm(k_loc[:, cs], qbd) + bias_ref[0, p]
                m = jnp.maximum(m, jnp.max(s_loc, axis=0, keepdims=True))
            p_ctx = jnp.exp2(s_ctx - m)
            den = jnp.sum(p_ctx, axis=0, keepdims=True)
            if local:
                p_loc = jnp.exp2(s_loc - m)
                den = den + jnp.sum(p_loc, axis=0, keepdims=True)
            o = _mm(_t_blocks(p_ctx), v_ctx[:, cs])
            if local:
                o = o + _mm(_t_blocks(p_loc), v_loc[:, cs])
            o = o * jnp.transpose(jnp.broadcast_to(1.0 / den, (128, 128)))
            o_ref[0, qs, cs] = jnp.where(first_head, o[:GRID_W], o[GRID_W:]).astype(BF16)

    def attend_all(local):
        k_ctx = k_ref[0, 0:lc, :]
        v_ctx = v_ref[0, 0:lc, :]
        k_loc, v_loc = [], []
        if local:
            for rr in range(NA_ROWS):
                r0 = jnp.clip(j * NA_ROWS + rr - n_ctx_blk - WIN_R // 2, 0, rows - wr)
                start = pl.multiple_of(lc + r0 * GRID_W, GRID_W)
                k_loc.append(k_ref[0, pl.ds(start, wr * GRID_W), :])
                v_loc.append(v_ref[0, pl.ds(start, wr * GRID_W), :])
        for p in range(NA_HEADS // 2):
            cs = slice(p * 128, (p + 1) * 128)
            qbds = []
            for rr in range(NA_ROWS):
                q2 = q_ref[0, rr * GRID_W:(rr + 1) * GRID_W, cs].astype(F32) * scale
                qbds.append(jnp.transpose(
                    jnp.where(same_head, jnp.concatenate([q2, q2], axis=0), 0.0)).astype(BF16))
            s_ctx_all = _mm(k_ctx[:, cs], jnp.concatenate(qbds, axis=1))
            pts, dens, olocs = [], [], []
            for rr in range(NA_ROWS):
                s_ctx = s_ctx_all[:, rr * 128:(rr + 1) * 128]
                m = jnp.max(s_ctx, axis=0, keepdims=True)
                if local:
                    s_loc = _mm(k_loc[rr][:, cs], qbds[rr]) + bias_refs[rr][0, p]
                    m = jnp.maximum(m, jnp.max(s_loc, axis=0, keepdims=True))
                p_ctx = jnp.exp2(s_ctx - m)
                den = jnp.sum(p_ctx, axis=0, keepdims=True)
                if local:
                    p_loc = jnp.exp2(s_loc - m)
                    den = den + jnp.sum(p_loc, axis=0, keepdims=True)
                    olocs.append(_mm(_t_blocks(p_loc), v_loc[rr][:, cs]))
                pts.append(_t_blocks(p_ctx))
                dens.append(den)
            o_all = _mm(jnp.concatenate(pts, axis=0), v_ctx[:, cs])
            for rr in range(NA_ROWS):
                o = o_all[rr * 128:(rr + 1) * 128]
                if local:
                    o = o + olocs[rr]
                o = o * jnp.transpose(jnp.broadcast_to(1.0 / dens[rr], (128, 128)))
                o_ref[0, rr * GRID_W:(rr + 1) * GRID_W, cs] = (
                    jnp.where(first_head, o[:GRID_W], o[GRID_W:]).astype(BF16))

    @pl.when(j * NA_ROWS < n_ctx_blk)
    def _():
        attend_all(False)

    @pl.when(j * NA_ROWS >= n_ctx_blk)
    def _():
        attend_all(True)


def _nattn(z, bias_tab, lc):
    nb, t, _ = z.shape
    rows = (t - lc) // GRID_W
    n_ctx_blk = lc // GRID_W
    _, cls = _na_row_classes(rows)
    cls_arr = jnp.asarray(np.concatenate([np.zeros((n_ctx_blk,), np.int32), cls]))
    wr = min(WIN_R, rows)
    qb = NA_ROWS * GRID_W
    assert n_ctx_blk % NA_ROWS == 0 and rows % NA_ROWS == 0

    def bias_spec(rr):
        return pl.BlockSpec((1, NA_HEADS // 2, wr * GRID_W, 2 * GRID_W),
                            lambda b, j, c: (c[j * NA_ROWS + rr], 0, 0, 0))

    grid_spec = pltpu.PrefetchScalarGridSpec(
        num_scalar_prefetch=1,
        grid=(nb, t // qb),
        in_specs=[
            pl.BlockSpec((1, qb, NA_WIDTH), lambda b, j, c: (b, j, OFF_NQ // NA_WIDTH)),
            pl.BlockSpec((1, t, NA_WIDTH), lambda b, j, c: (b, 0, OFF_NK // NA_WIDTH)),
            pl.BlockSpec((1, t, NA_WIDTH), lambda b, j, c: (b, 0, OFF_NV // NA_WIDTH)),
        ] + [bias_spec(rr) for rr in range(NA_ROWS)],
        out_specs=pl.BlockSpec((1, qb, NA_WIDTH), lambda b, j, c: (b, j, 0)),
    )
    return pl.pallas_call(
        functools.partial(_na_kernel, lc=lc, rows=rows),
        grid_spec=grid_spec,
        out_shape=jax.ShapeDtypeStruct((nb, t, NA_WIDTH), BF16),
        compiler_params=_params(("parallel", "arbitrary"), VMEM_LIMIT),
        name="nattn",
    )(cls_arr, z, z, z, *([bias_tab] * NA_ROWS))


def _rope_tables(lc, l):
    half = RET_DK // 2
    nf = half // 2
    tpos = np.arange(l)
    inv = ROPE_BASE ** (-jnp.arange(nf, dtype=F32) / nf)

    def part(pos):
        ang = jnp.asarray(pos, F32)[:, None] * inv[None, :]
        c, s = jnp.cos(ang), jnp.sin(ang)
        return jnp.concatenate([c, c], -1), jnp.concatenate([-s, s], -1)

    c_r, s_r = part(tpos // GRID_W)
    c_c, s_c = part(tpos % GRID_W)
    cos = jnp.concatenate([c_r, c_c], -1)
    sin = jnp.concatenate([s_r, s_c], -1)
    cos = jnp.concatenate([jnp.ones((lc, RET_DK), F32), cos], 0)
    sin = jnp.concatenate([jnp.zeros((lc, RET_DK), F32), sin], 0)
    return jnp.tile(cos, (1, RET_HEADS)), jnp.tile(sin, (1, RET_HEADS))


def _ret_kernel(dl_ref, *refs):
    st_ref = refs[-1]

    @pl.when(pl.program_id(1) == 0)
    def _():
        st_ref[...] = jnp.zeros_like(st_ref)

    for d in range(2):
        _ret_direction(d, dl_ref, *refs[5 * d:5 * d + 5], refs[10 + d], st_ref)


def _ret_direction(d, dl_ref, q_ref, k_ref, v_ref, cos_ref, sin_ref, o_ref, st_ref):
    cos = cos_ref[...]
    sin = sin_ref[...]
    lane_qk = lax.broadcasted_iota(jnp.int32, (CHUNK, RET_QK_W), 1)
    first = (lane_qk % (RET_DK // 2)) < (RET_DK // 4)

    def rope(a):
        a = a.astype(F32)
        partner = jnp.where(first, pltpu.roll(a, RET_QK_W - RET_DK // 4, axis=1),
                            pltpu.roll(a, RET_DK // 4, axis=1))
        return a * cos + partner * sin

    q = rope(q_ref[0])
    k = rope(k_ref[0]) * (RET_DK ** -0.5)
    v = v_ref[0]

    ri = lax.broadcasted_iota(jnp.int32, (CHUNK, 2 * CHUNK), 0)
    ci = lax.broadcasted_iota(jnp.int32, (CHUNK, 2 * CHUNK), 1) % CHUNK
    dist = (ri - ci) if d == 0 else (ci - ri)
    keep = dist >= d
    dist_f = jnp.maximum(dist, 0).astype(F32)
    q_exp = ((ri + 1) if d == 0 else (CHUNK - ri)).astype(F32)
    rk = lax.broadcasted_iota(jnp.int32, (CHUNK, CHUNK), 0)
    k_exp = ((CHUNK - 1 - rk) if d == 0 else rk).astype(F32)
    lane_half = lax.broadcasted_iota(jnp.int32, (CHUNK, CHUNK), 1) >= RET_DK
    row_half = rk >= RET_DK

    def log_gamma(h):
        xv = jnp.full((CHUNK, CHUNK), dl_ref[d, h], F32)
        return jnp.minimum(xv, 0.0) - jnp.log1p(jnp.exp(-jnp.abs(xv)))

    def by_head(m):
        return jnp.concatenate([jnp.where(row_half, 0.0, m), jnp.where(row_half, m, 0.0)], axis=1).astype(BF16)

    zero = jnp.zeros((CHUNK, RET_DV), BF16)
    for p in range(RET_HEADS // 2):
        cs = slice(p * 128, (p + 1) * 128)
        lg0, lg1 = log_gamma(2 * p), log_gamma(2 * p + 1)
        lg2 = jnp.concatenate([lg0, lg1], axis=1)
        q2b = q[:, cs].astype(BF16)
        k2 = k[:, cs]
        kd_t = jnp.transpose(k2 * jnp.exp(k_exp * jnp.where(lane_half, lg1, lg0)))
        v0 = v[:, 2 * p * RET_DV:(2 * p + 1) * RET_DV]
        v1 = v[:, (2 * p + 1) * RET_DV:(2 * p + 2) * RET_DV]
        s2 = st_ref[d, p]
        a = _mm(q2b, by_head(jnp.transpose(k2)))
        a = jnp.where(keep, a * jnp.exp(dist_f * lg2), 0.0)
        v_diag = jnp.concatenate([jnp.concatenate([v0, zero], axis=1),
                                  jnp.concatenate([zero, v1], axis=1)], axis=0)
        inner = _mm(a.astype(BF16), v_diag)
        cross = _mm(q2b, by_head(s2)) * jnp.exp(q_exp * lg2)
        o_ref[0, :, 2 * p * RET_DV:(2 * p + 2) * RET_DV] = (inner + cross).astype(o_ref.dtype)
        st_ref[d, p] = (s2 * jnp.exp(float(CHUNK) * jnp.where(row_half, lg1, lg0))
                        + _mm(by_head(kd_t), jnp.concatenate([v0, v1], axis=0)))


def _retention(z, decay_logit, cos_t, sin_t, lc):
    nb, t, _ = z.shape
    nc = t // CHUNK
    ncc = lc // CHUNK

    def chunk(d, s):
        return s if d == 0 else jnp.where(s < ncc, ncc - 1 - s, nc - 1 - (s - ncc))

    def in_specs(d):
        return [
            pl.BlockSpec((1, CHUNK, RET_QK_W), lambda b, s, dl: (b, chunk(d, s), OFF_RQ // RET_QK_W)),
            pl.BlockSpec((1, CHUNK, RET_QK_W), lambda b, s, dl: (b, chunk(d, s), OFF_RK // RET_QK_W)),
            pl.BlockSpec((1, CHUNK, RET_V_W), lambda b, s, dl: (b, chunk(d, s), OFF_RV // RET_V_W)),
            pl.BlockSpec((CHUNK, RET_QK_W), lambda b, s, dl: (chunk(d, s), 0)),
            pl.BlockSpec((CHUNK, RET_QK_W), lambda b, s, dl: (chunk(d, s), 0)),
        ]

    def out_spec(d):
        return pl.BlockSpec((1, CHUNK, RET_V_W), lambda b, s, dl: (b, chunk(d, s), 0))

    grid_spec = pltpu.PrefetchScalarGridSpec(
        num_scalar_prefetch=1,
        grid=(nb, nc),
        in_specs=in_specs(0) + in_specs(1),
        out_specs=[out_spec(0), out_spec(1)],
        scratch_shapes=[pltpu.VMEM((2, RET_HEADS // 2, 2 * RET_DK, RET_DV), F32)],
    )
    return pl.pallas_call(
        _ret_kernel,
        grid_spec=grid_spec,
        out_shape=[jax.ShapeDtypeStruct((nb, t, RET_V_W), BF16)] * 2,
        compiler_params=_params(("parallel", "arbitrary")),
        name="retention",
    )(decay_logit, *([z, z, z, cos_t, sin_t] * 2))


def _merge_kernel(a_ref, b_ref, rf_ref, rb_ref, rg_ref, ga_ref, gb_ref, gc_ref, x_ref, mod_ref,
                  wb_ref, wo_ref, o_ref):
    o = rf_ref[0].astype(F32) + rb_ref[0].astype(F32)
    parts = []
    for h in range(RET_HEADS):
        oh = o[:, h * RET_DV:(h + 1) * RET_DV]
        parts.append(oh * lax.rsqrt(jnp.mean(oh * oh, axis=-1, keepdims=True) + EPS))
    c = jnp.concatenate(parts, axis=-1) * _silu(rg_ref[0].astype(F32))
    m = (jax.nn.sigmoid(ga_ref[0].astype(F32)) * _mm(a_ref[0], wb_ref[0])
         + jax.nn.sigmoid(gb_ref[0].astype(F32)) * _mm(b_ref[0], wb_ref[1])
         + jax.nn.sigmoid(gc_ref[0].astype(F32)) * _mm(c.astype(BF16), wb_ref[2]))
    mix = _mm(m.astype(BF16), wo_ref[...])
    o_ref[0] = x_ref[0] + mod_ref[0, 2:3, :] * mix


def _merge(a_lat, b_lat, ret, z, xs, mod, wb_b, wo_b, tm):
    nb, t, d = xs.shape
    bw = CM_WIDTH
    return pl.pallas_call(
        _merge_kernel,
        grid=(nb, t // tm),
        in_specs=[
            pl.BlockSpec((1, tm, bw), lambda b, j: (b, j, 0)),
            pl.BlockSpec((1, tm, bw), lambda b, j: (b, j, 0)),
            pl.BlockSpec((1, tm, bw), lambda b, j: (b, j, 0)),
            pl.BlockSpec((1, tm, bw), lambda b, j: (b, j, 0)),
            pl.BlockSpec((1, tm, RET_V_W), lambda b, j: (b, j, OFF_RG // RET_V_W)),
            pl.BlockSpec((1, tm, d), lambda b, j: (b, j, OFF_GA // d)),
            pl.BlockSpec((1, tm, d), lambda b, j: (b, j, OFF_GB // d)),
            pl.BlockSpec((1, tm, d), lambda b, j: (b, j, OFF_GC // d)),
            pl.BlockSpec((1, tm, d), lambda b, j: (b, j, 0)),
            pl.BlockSpec((1, 6, d), _mod_row(nb)),
            pl.BlockSpec((3, bw, d), lambda b, j: (0, 0, 0)),
            pl.BlockSpec((d, d), lambda b, j: (0, 0)),
        ],
        out_specs=pl.BlockSpec((1, tm, d), lambda b, j: (b, j, 0)),
        out_shape=jax.ShapeDtypeStruct((nb, t, d), F32),
        compiler_params=_params(("parallel", "parallel"), VMEM_LIMIT),
        name="merge",
    )(a_lat, b_lat, ret[0], ret[1], z, z, z, z, xs, mod, wb_b, wo_b)


ROW_SLAB = D_MODEL // 128


def _slab_load(ref, lead, n, row0=0):
    return jnp.concatenate(
        [ref[lead + (pl.ds(row0 * ROW_SLAB + j, n, stride=ROW_SLAB), slice(None))] for j in range(ROW_SLAB)],
        axis=-1)


def _slab_store(ref, lead, val, row0=0):
    for j in range(ROW_SLAB):
        ref[lead + (pl.ds(row0 * ROW_SLAB + j, val.shape[0], stride=ROW_SLAB), slice(None))] = (
            val[:, j * 128:(j + 1) * 128])


def _first_index(mask, idx, big):
    return jnp.min(jnp.where(mask, idx, big), axis=0, keepdims=True)


def _router_kernel(x_ref, mod_ref, g_ref, rwt_ref, rb_ref, h_ref, ei_ref, wk_ref):
    h = _modulated_norm(x_ref[0], g_ref[...], mod_ref[0, 3:4, :], mod_ref[0, 4:5, :])
    _slab_store(h_ref, (0,), h)
    tm = h.shape[0]
    logits = _mm_nt(rwt_ref[...], h)
    scores = jax.nn.sigmoid(logits)
    biased = scores + rb_ref[:, 0:1]
    gsz = N_EXPERTS // N_GROUPS
    sub = lax.broadcasted_iota(jnp.int32, (gsz, tm), 0)
    gscore = []
    for g in range(N_GROUPS):
        blk = biased[g * gsz:(g + 1) * gsz]
        m1 = jnp.max(blk, axis=0, keepdims=True)
        i1 = _first_index(blk == m1, sub, gsz)
        m2 = jnp.max(jnp.where(sub == i1, -jnp.inf, blk), axis=0, keepdims=True)
        gscore.append(m1 + m2)
    gs = jnp.concatenate(gscore, axis=0)
    gidx = lax.broadcasted_iota(jnp.int32, (N_GROUPS, tm), 0)
    gsel = jnp.zeros((N_GROUPS, tm), jnp.int32)
    for _ in range(TOPK_GROUPS):
        m = jnp.max(gs, axis=0, keepdims=True)
        pick = gidx == _first_index(gs == m, gidx, N_GROUPS)
        gsel = jnp.where(pick, 1, gsel)
        gs = jnp.where(pick, -jnp.inf, gs)
    cand = jnp.concatenate(
        [jnp.where(gsel[g:g + 1] > 0, biased[g * gsz:(g + 1) * gsz], -jnp.inf) for g in range(N_GROUPS)],
        axis=0)
    eidx = lax.broadcasted_iota(jnp.int32, (N_EXPERTS, tm), 0)
    picks, ws = [], []
    for _ in range(TOP_K):
        m = jnp.max(cand, axis=0, keepdims=True)
        first = _first_index(cand == m, eidx, N_EXPERTS)
        pick = eidx == first
        picks.append(first)
        ws.append(jnp.sum(jnp.where(pick, scores, 0.0), axis=0, keepdims=True))
        cand = jnp.where(pick, -jnp.inf, cand)
    w = jnp.concatenate(ws, axis=0)
    ei_ref[0] = jnp.concatenate(picks, axis=0)
    wk_ref[0] = w / jnp.sum(w, axis=0, keepdims=True) * ROUTED_SCALE


def _router(xs, mod, g, rw_t, rb, tm):
    nb, t, d = xs.shape
    return pl.pallas_call(
        _router_kernel,
        grid=(nb, t // tm),
        in_specs=[
            pl.BlockSpec((1, tm, d), lambda b, j: (b, j, 0)),
            pl.BlockSpec((1, 6, d), _mod_row(nb)),
            pl.BlockSpec((1, d), lambda b, j: (0, 0)),
            pl.BlockSpec((N_EXPERTS, d), lambda b, j: (0, 0)),
            pl.BlockSpec((N_EXPERTS, 1), lambda b, j: (0, 0)),
        ],
        out_specs=[
            pl.BlockSpec((1, tm * ROW_SLAB, 128), lambda b, j: (b, j, 0)),
            pl.BlockSpec((1, TOP_K, tm), lambda b, j: (b, 0, j)),
            pl.BlockSpec((1, TOP_K, tm), lambda b, j: (b, 0, j)),
        ],
        out_shape=[jax.ShapeDtypeStruct((nb, t * ROW_SLAB, 128), F32),
                   jax.ShapeDtypeStruct((nb, TOP_K, t), jnp.int32), jax.ShapeDtypeStruct((nb, TOP_K, t), F32)],
        compiler_params=_params(("parallel", "parallel")),
        name="router",
    )(xs, mod, g.reshape(1, d), rw_t, rb.reshape(N_EXPERTS, 1))


def _dispatch(eidx, wk):
    nb, k, t = eidx.shape
    pos = lax.broadcasted_iota(jnp.int32, eidx.shape, 2)
    skey, sw = lax.sort(((eidx * t + pos).reshape(nb, k * t), wk.reshape(nb, k * t)), dimension=1, num_keys=1)
    slab_row = (skey % t) * ROW_SLAB
    cnt = jnp.sum(eidx[:, None] == jnp.arange(N_EXPERTS, dtype=jnp.int32)[None, :, None, None],
                  axis=(2, 3), dtype=jnp.int32)
    off = jnp.cumsum(cnt, axis=1) - cnt
    pad = ((0, 0), (0, MOE_LIST_PAD))
    return jnp.pad(slab_row, pad), jnp.pad(sw, pad), cnt, off


def _slab(row):
    return pl.ds(pl.multiple_of(row, ROW_SLAB), ROW_SLAB)


def _expert_ffn(base, w_ref, stage_ref, ostage_ref, wg_ref, wu_ref, wd_ref, k):
    x = _slab_load(stage_ref, (), MOE_TILE, k * MOE_TILE).astype(BF16)
    act = _silu(_mm(x, wg_ref[k])) * _mm(x, wu_ref[k])
    aligned = pl.multiple_of((base // 128) * 128, 128)
    wrow = pltpu.roll(w_ref[0, :, pl.ds(aligned, MOE_WIN)], (MOE_WIN - (base - aligned)) % MOE_WIN, axis=1)
    wcol = jnp.concatenate(
        [jnp.transpose(jnp.broadcast_to(wrow[:, k * 128:(k + 1) * 128], (128, 128)))
         for k in range(MOE_WIN // 128 - 1)], axis=0)[:MOE_TILE]
    act = act * jnp.concatenate([wcol] * (EXPERT_W // 128), axis=-1)
    _slab_store(ostage_ref, (), _mm(act.astype(BF16), wd_ref[k]), k * MOE_TILE)


def _experts_kernel(cnt_ref, off_ref, idx_ref, w_ref, src_ref, wg_ref, wu_ref, wd_ref, acc_ref,
                    stage0, stage1, ostage0, ostage1, *, ne, spare):
    b = pl.program_id(0)
    s = pl.program_id(1)
    ns = ne // MOE_EPS

    def expert(stage, k):
        return b * ne + stage * MOE_EPS + k

    st_g = jnp.minimum(s, ns - 1)
    st_m = jnp.clip(s - 1, 0, ns - 1)
    st_s = jnp.maximum(s - 2, 0)
    ffn_live = (s >= 1) & (s <= ns)
    start_g = [off_ref[expert(st_g, k)] for k in range(MOE_EPS)]
    start_m = [off_ref[expert(st_m, k)] for k in range(MOE_EPS)]
    count_m = [jnp.where(ffn_live, cnt_ref[expert(st_m, k)], 0) for k in range(MOE_EPS)]
    start_s = [off_ref[expert(st_s, k)] for k in range(MOE_EPS)]
    count_s = [jnp.where(s >= 2, jnp.minimum(cnt_ref[expert(st_s, k)], MOE_TILE), 0) for k in range(MOE_EPS)]

    @pl.when(s == 0)
    def _():
        acc_ref[...] = jnp.zeros_like(acc_ref)
        stage1[...] = jnp.zeros_like(stage1)
        ostage0[...] = jnp.zeros_like(ostage0)

    def step(stage_w, stage_r, ostage_w, ostage_r):
        for k in range(MOE_EPS):
            for r in range(MOE_TILE):
                stage_w[_slab((k * MOE_TILE + r) * ROW_SLAB), :] = (
                    src_ref[0, _slab(idx_ref[0, 0, start_g[k] + r]), :])
        for k in range(MOE_EPS):
            _expert_ffn(start_m[k], w_ref, stage_r, ostage_w, wg_ref, wu_ref, wd_ref, k)
        for k in range(MOE_EPS):
            for g in range(MOE_TILE // MOE_GROUP):
                rws = range(g * MOE_GROUP, (g + 1) * MOE_GROUP)
                dst = [_slab(jnp.where(r < count_s[k], idx_ref[0, 0, start_s[k] + r], spare)) for r in rws]
                vals = [acc_ref[0, d, :] + ostage_r[_slab((k * MOE_TILE + r) * ROW_SLAB), :]
                        for d, r in zip(dst, rws)]
                for d, v in zip(dst, vals):
                    acc_ref[0, d, :] = v

    @pl.when(s % 2 == 0)
    def _():
        step(stage0, stage1, ostage1, ostage0)

    @pl.when(s % 2 == 1)
    def _():
        step(stage1, stage0, ostage0, ostage1)

    def overflow(stage_x, ostage_x, k):
        def tile(ti, carry):
            base = start_m[k] + ti * MOE_TILE
            rows = jnp.minimum(count_m[k] - ti * MOE_TILE, MOE_TILE)

            def gather(r, c):
                stage_x[_slab((k * MOE_TILE + r) * ROW_SLAB), :] = src_ref[0, _slab(idx_ref[0, 0, base + r]), :]
                return c

            def scatter(r, c):
                d = _slab(idx_ref[0, 0, base + r])
                acc_ref[0, d, :] = acc_ref[0, d, :] + ostage_x[_slab((k * MOE_TILE + r) * ROW_SLAB), :]
                return c

            lax.fori_loop(0, rows, gather, 0)
            _expert_ffn(base, w_ref, stage_x, ostage_x, wg_ref, wu_ref, wd_ref, k)
            lax.fori_loop(0, rows, scatter, 0)
            return carry

        lax.fori_loop(1, (count_m[k] + MOE_TILE - 1) // MOE_TILE, tile, 0)

    for k in range(MOE_EPS):
        @pl.when((count_m[k] > MOE_TILE) & (s % 2 == 0))
        def _(k=k):
            overflow(stage1, ostage0, k)

        @pl.when((count_m[k] > MOE_TILE) & (s % 2 == 1))
        def _(k=k):
            overflow(stage0, ostage1, k)


def _experts(h2, slab_row, sw, cnt, off, wg, wu, wd, layer):
    nb = h2.shape[0]
    t = h2.shape[1] // ROW_SLAB
    ne = cnt.shape[1]
    _, d, ew = wg.shape
    n_list = slab_row.shape[1]
    assert h2.shape == (nb, t * ROW_SLAB, 128) and d == D_MODEL and ew == EXPERT_W and MOE_TILE % 16 == 0
    assert MOE_LIST_PAD >= MOE_WIN and ne % MOE_EPS == 0
    ns = ne // MOE_EPS
    ffn = lambda b, s, c, o: (layer * ns + jnp.clip(s - 1, 0, ns - 1), 0, 0)
    acc_rows = (t + 1) * ROW_SLAB
    grid_spec = pltpu.PrefetchScalarGridSpec(
        num_scalar_prefetch=2,
        grid=(nb, ns + 2),
        in_specs=[
            pl.BlockSpec((1, 1, n_list), lambda b, s, c, o: (b, 0, 0), memory_space=pltpu.SMEM),
            pl.BlockSpec((1, 1, n_list), lambda b, s, c, o: (b, 0, 0)),
            pl.BlockSpec((1, t * ROW_SLAB, 128), lambda b, s, c, o: (b, 0, 0)),
            pl.BlockSpec((MOE_EPS, d, ew), ffn),
            pl.BlockSpec((MOE_EPS, d, ew), ffn),
            pl.BlockSpec((MOE_EPS, ew, d), ffn),
        ],
        out_specs=pl.BlockSpec((1, acc_rows, 128), lambda b, s, c, o: (b, 0, 0)),
        scratch_shapes=[pltpu.VMEM((MOE_EPS * MOE_TILE * ROW_SLAB, 128), F32)] * 4,
    )
    return pl.pallas_call(
        functools.partial(_experts_kernel, ne=ne, spare=t * ROW_SLAB),
        grid_spec=grid_spec,
        out_shape=jax.ShapeDtypeStruct((nb, acc_rows, 128), F32),
        compiler_params=_params(("parallel", "arbitrary"), VMEM_LIMIT),
        name="experts",
    )(cnt.reshape(nb * ne), off.reshape(nb * ne), slab_row.reshape(nb, 1, n_list), sw.reshape(nb, 1, n_list),
      h2, wg, wu, wd)


def _moe_out_kernel(h_ref, r_ref, x_ref, mod_ref, sg_ref, su_ref, sd_ref, *rest):
    tm = x_ref.shape[1]
    h = _slab_load(h_ref, (0,), tm).astype(BF16)
    shared = _mm((_silu(_mm(h, sg_ref[...])) * _mm(h, su_ref[...])).astype(BF16), sd_ref[...])
    y = x_ref[0] + mod_ref[0, 5:6, :] * (_slab_load(r_ref, (0,), tm) + shared)
    if len(rest) == 2:
        y = y * lax.rsqrt(jnp.mean(y * y, axis=-1, keepdims=True) + EPS) * rest[0][...]
    rest[-1][0] = y


def _moe_out(h2, routed, xs, mod, sg, su, sd, tm, skip=0, final_g=None):
    nb, t, d = xs.shape
    tok = lambda b, j: (b, j + skip, 0)
    in_specs = [
        pl.BlockSpec((1, tm * ROW_SLAB, 128), tok),
        pl.BlockSpec((1, tm * ROW_SLAB, 128), tok),
        pl.BlockSpec((1, tm, d), tok),
        pl.BlockSpec((1, 6, d), (lambda b, j: (b, 0, 0)) if skip else _mod_row(nb)),
        pl.BlockSpec(sg.shape, lambda b, j: (0, 0)),
        pl.BlockSpec(su.shape, lambda b, j: (0, 0)),
        pl.BlockSpec(sd.shape, lambda b, j: (0, 0)),
    ]
    args = [h2, routed, xs, mod, sg, su, sd]
    if final_g is not None:
        in_specs.append(pl.BlockSpec((1, d), lambda b, j: (0, 0)))
        args.append(final_g.reshape(1, d))
    return pl.pallas_call(
        _moe_out_kernel,
        grid=(nb, t // tm - skip),
        in_specs=in_specs,
        out_specs=pl.BlockSpec((1, tm, d), lambda b, j: (b, j, 0)),
        out_shape=jax.ShapeDtypeStruct((nb, t - skip * tm, d), F32),
        compiler_params=_params(("parallel", "parallel")),
        name="moe_out",
    )(*args)


def kernel(x, c, ctx, c_ctx, w_ada, b_ada, norm_mix_g, norm_ffn_g, w_in, cm_ln_g, cm_ln_b, cm_w_s, cm_b_s,
           na_rpb, ret_decay_logit, w_branch, w_out, router_w, router_bias, exp_w_gate, exp_w_up,
           exp_w_down, sh_w_gate, sh_w_up, sh_w_down, final_g):
    nb, l, d = x.shape
    lc = ctx.shape[1]
    depth = w_ada.shape[0]
    assert d == D_MODEL and nb < MOD_ROWS and l % GRID_W == 0 and lc % CHUNK == 0 and l % CHUNK == 0
    tm = TOKEN_TILE
    assert lc % tm == 0 and l % tm == 0

    cc = jnp.zeros((MOD_ROWS, d), F32).at[:nb].set(c).at[nb].set(c_ctx)
    mod_all = _adaln(cc, w_ada, b_ada).reshape(depth, MOD_ROWS, 6, d)
    cos_t, sin_t = _rope_tables(lc, l)
    xs = jnp.concatenate([ctx, x], axis=1)
    wg_all, wu_all, wd_all = (_to_bf16(w).reshape((-1,) + w.shape[2:])
                              for w in (exp_w_gate, exp_w_up, exp_w_down))

    for layer in range(depth):
        mod = mod_all[layer]
        z = _inproj(xs, mod, norm_mix_g[layer], w_in[layer].astype(BF16), tm)
        a_lat = _gmlp(z, cm_ln_g[layer], cm_ln_b[layer], cm_w_s[layer].astype(BF16),
                      jnp.transpose(cm_b_s[layer]))
        b_lat = _nattn(z, _na_bias_table(na_rpb[layer], l // GRID_W), lc)
        ret = _retention(z, ret_decay_logit[layer], cos_t, sin_t, lc)
        xs = _merge(a_lat, b_lat, ret, z, xs, mod, w_branch[layer].astype(BF16),
                    w_out[layer].astype(BF16), tm)
        h2, eidx, wk = _router(xs, mod, norm_ffn_g[layer], jnp.transpose(router_w[layer]),
                               router_bias[layer], tm)
        slab_row, sw, cnt, off = _dispatch(eidx, wk)
        routed = _experts(h2, slab_row, sw, cnt, off, wg_all, wu_all, wd_all, layer)
        last = layer == depth - 1
        xs = _moe_out(h2, routed, xs, mod, sh_w_gate[layer].astype(BF16), sh_w_up[layer].astype(BF16),
                      sh_w_down[layer].astype(BF16), tm, skip=lc // tm if last else 0,
                      final_g=final_g if last else None)
    return xs
```
